```python
import jax, jax.numpy as jnp
from jax import lax
import numpy as np

D_MODEL = 1024
BATCH = 8
SEQ = 2048
DEPTH = 2
DEC_BATCH = 128
DEC_SEQ = 4
PAST_LEN = 16384
PAGE_SIZE = 128

D_PLE = 256
CHUNK = 64
CONV_W = 4
EPS = 1e-6
N_BRANCH = 4
GLA_H = 4
GLA_DK = 64
GLA_DV = 128
GLA_RANK = 16
GLA_GATE_NORM = 16.0
RET_H = 4
RET_DK = 64
RET_DV = 128
ROPE_BASE = 10000.0
GDN_H = 4
GDN_DK = 64
GDN_DV = 128
GDN_CONV_C = 2 * GDN_H * GDN_DK + GDN_H * GDN_DV
SSD_H = 8
SSD_P = 64
SSD_N = 64
SSD_G = 2
SSD_DI = SSD_H * SSD_P
SSD_CONV_C = SSD_DI + 2 * SSD_G * SSD_N
BRANCH_W = 512
D_FF = 2816
N_EXPERTS = 8
TOP_K = 2
N_DENSE = (DEPTH + 1) // 2
N_MOE = DEPTH // 2
SPLITS = (GLA_H * GLA_DK, GLA_H * GLA_DK, GLA_H * GLA_DV, GLA_RANK, GLA_H * GLA_DV,
          RET_H * RET_DK, RET_H * RET_DK, RET_H * RET_DV, RET_H * RET_DV,
          GDN_CONV_C, GDN_H, GDN_H, GDN_H * GDN_DV,
          SSD_DI, SSD_CONV_C, SSD_H,
          N_BRANCH * D_MODEL)
D_IN = sum(SPLITS)

kernel_name = "hybrid_gla_ret_gdn_ssd_decode_step"


def rmsnorm(x, g):
    x32 = x.astype(jnp.float32)
    y = x32 * lax.rsqrt(jnp.mean(x32 * x32, axis=-1, keepdims=True) + EPS)
    return (y * g.astype(jnp.float32)).astype(x.dtype)


def l2norm(x):
    x32 = x.astype(jnp.float32)
    return (x32 * lax.rsqrt(jnp.sum(x32 * x32, axis=-1, keepdims=True) + EPS)).astype(x.dtype)


def causal_conv(u, buf, w):
    L = u.shape[1]
    full = jnp.concatenate([buf.astype(u.dtype), u], axis=1)
    out = full[:, 0:L] * w[0]
    for t in range(1, CONV_W):
        out = out + full[:, t:t + L] * w[t]
    return out, full[:, L:]


def rope(t, pos):
    half = t.shape[-1] // 2
    inv = ROPE_BASE ** (-jnp.arange(half, dtype=jnp.float32) / half)
    ang = pos.astype(jnp.float32)[:, None] * inv[None, :]
    cos = jnp.cos(ang)[:, None, :]
    sin = jnp.sin(ang)[:, None, :]
    t32 = t.astype(jnp.float32)
    t1, t2 = t32[..., :half], t32[..., half:]
    return jnp.concatenate([t1 * cos - t2 * sin, t1 * sin + t2 * cos], axis=-1).astype(t.dtype)


def _chunk_size(L):
    return CHUNK if L % CHUNK == 0 else L


def _to_chunks(t, c):
    B_, L, H = t.shape[:3]
    rest = t.shape[3:]
    t = t.reshape((B_, L // c, c, H) + rest)
    perm = (1, 0, 3, 2) + tuple(range(4, t.ndim))
    return jnp.transpose(t, perm).astype(jnp.float32)


def _from_chunks(o):
    n, B_, H, c, d = o.shape
    return jnp.transpose(o, (1, 0, 3, 2, 4)).reshape(B_, n * c, H, d)


def _masks(c):
    idx = jnp.arange(c)
    return idx[:, None] >= idx[None, :], idx[:, None] > idx[None, :]


def _pair_decay(G, mask):
    diff = G[..., :, None] - G[..., None, :]
    return jnp.where(mask, jnp.exp(jnp.where(mask, diff, 0.0)), 0.0)


def chunked_scalar_decay(q, k, v, logd, S0):
    c = _chunk_size(q.shape[1])
    incl, _ = _masks(c)

    def step(S, blk):
        qc, kc, vc, lc = blk
        G = jnp.cumsum(lc, axis=-1)
        A = jnp.einsum('bhid,bhjd->bhij', qc, kc) * _pair_decay(G, incl)
        o = (jnp.einsum('bhij,bhjv->bhiv', A, vc)
             + jnp.einsum('bhid,bhdv->bhiv', qc * jnp.exp(G)[..., None], S))
        S = (S * jnp.exp(G[..., -1])[..., None, None]
             + jnp.einsum('bhjd,bhjv->bhdv', kc * jnp.exp(G[..., -1:] - G)[..., None], vc))
        return S, o

    S, o = lax.scan(step, S0.astype(jnp.float32),
                    (_to_chunks(q, c), _to_chunks(k, c), _to_chunks(v, c), _to_chunks(logd, c)))
    return _from_chunks(o).astype(v.dtype), S.astype(S0.dtype)


def chunked_vector_decay(q, k, v, logd, S0):
    c = _chunk_size(q.shape[1])
    incl, _ = _masks(c)
    m = incl[:, :, None]

    def step(S, blk):
        qc, kc, vc, lc = blk
        G = jnp.cumsum(lc, axis=-2)
        diff = G[:, :, :, None, :] - G[:, :, None, :, :]
        D = jnp.where(m, jnp.exp(jnp.where(m, diff, 0.0)), 0.0)
        A = jnp.einsum('bhid,bhjd,bhijd->bhij', qc, kc, D)
        o = (jnp.einsum('bhij,bhjv->bhiv', A, vc)
             + jnp.einsum('bhid,bhdv->bhiv', qc * jnp.exp(G), S))
        S = (S * jnp.exp(G[:, :, -1, :])[..., None]
             + jnp.einsum('bhjd,bhjv->bhdv', kc * jnp.exp(G[:, :, -1:, :] - G), vc))
        return S, o

    S, o = lax.scan(step, S0.astype(jnp.float32),
                    (_to_chunks(q, c), _to_chunks(k, c), _to_chunks(v, c), _to_chunks(logd, c)))
    return _from_chunks(o).astype(v.dtype), S.astype(S0.dtype)


def chunked_gated_delta(q, k, v, beta, logd, S0):
    c = _chunk_size(q.shape[1])
    incl, strict = _masks(c)
    eye = jnp.eye(c, dtype=jnp.float32)

    def step(S, blk):
        qc, kc, vc, bc, lc = blk
        G = jnp.cumsum(lc, axis=-1)
        Dm = _pair_decay(G, incl)
        kb = kc * bc[..., None]
        M = jnp.einsum('bhid,bhjd->bhij', kb, kc) * jnp.where(strict, Dm, 0.0)
        T = eye + M
        u = lax.linalg.triangular_solve(T, vc * bc[..., None], left_side=True, lower=True)
        w = lax.linalg.triangular_solve(T, kb * jnp.exp(G)[..., None], left_side=True, lower=True)
        vn = u - jnp.einsum('bhid,bhdv->bhiv', w, S)
        A = jnp.einsum('bhid,bhjd->bhij', qc, kc) * Dm
        o = (jnp.einsum('bhij,bhjv->bhiv', A, vn)
             + jnp.einsum('bhid,bhdv->bhiv', qc * jnp.exp(G)[..., None], S))
        S = (S * jnp.exp(G[..., -1])[..., None, None]
             + jnp.einsum('bhjd,bhjv->bhdv', kc * jnp.exp(G[..., -1:] - G)[..., None], vn))
        return S, o

    S, o = lax.scan(step, S0.astype(jnp.float32),
                    (_to_chunks(q, c), _to_chunks(k, c), _to_chunks(v, c),
                     _to_chunks(beta, c), _to_chunks(logd, c)))
    return _from_chunks(o).astype(v.dtype), S.astype(S0.dtype)


def token_mixers(u, gla0, ret0, gdn0, gdnc0, ssd0, ssdc0, prm, i, pos0):
    B_, L, _ = u.shape
    f32 = jnp.float32
    z = u @ prm['w_in'][i]
    cuts = [int(s) for s in np.cumsum(SPLITS)[:-1]]
    (gq, gk, gv, glr, gg, rq, rk, rv, rg, dqkv, db, da, dg, sz, sxbc, sdt, mg) = jnp.split(z, cuts, axis=-1)

    def heads(t, h):
        return t.reshape(B_, L, h, -1)

    q = heads(gq, GLA_H) * GLA_DK ** -0.5
    k = heads(gk, GLA_H)
    v = heads(gv, GLA_H)
    lg = jax.nn.log_sigmoid((glr @ prm['gla_w_gk'][i] + prm['gla_b_gk'][i]).astype(f32)) / GLA_GATE_NORM
    oa, gla1 = chunked_vector_decay(q, k, v, heads(lg, GLA_H), gla0)
    oa = (rmsnorm(oa, prm['gla_norm'][i]) * jax.nn.silu(heads(gg, GLA_H))).reshape(B_, L, -1)

    pos = pos0 + jnp.arange(L)
    q = rope(heads(rq, RET_H), pos)
    k = rope(heads(rk, RET_H), pos) * RET_DK ** -0.5
    v = heads(rv, RET_H)
    log_gamma = jnp.log(1.0 - 2.0 ** (-5.0 - jnp.arange(RET_H, dtype=f32)))
    ob, ret1 = chunked_scalar_decay(q, k, v, jnp.broadcast_to(log_gamma, (B_, L, RET_H)), ret0)
    ob = (rmsnorm(ob, prm['ret_norm'][i]) * jax.nn.silu(heads(rg, RET_H))).reshape(B_, L, -1)

    cqkv, gdnc1 = causal_conv(dqkv, gdnc0, prm['gdn_conv_w'][i])
    cqkv = jax.nn.silu(cqkv)
    cq, ck, cv = jnp.split(cqkv, [GDN_H * GDN_DK, 2 * GDN_H * GDN_DK], axis=-1)
    q = l2norm(heads(cq, GDN_H)) * GDN_DK ** -0.5
    k = l2norm(heads(ck, GDN_H))
    v = heads(cv, GDN_H)
    beta = jax.nn.sigmoid(db.astype(f32))
    lgd = -jnp.exp(prm['gdn_a_log'][i].astype(f32)) * jax.nn.softplus(da.astype(f32) + prm['gdn_dt_bias'][i].astype(f32))
    oc, gdn1 = chunked_gated_delta(q, k, v, beta, lgd, gdn0)
    oc = (rmsnorm(oc, prm['gdn_norm'][i]) * jax.nn.silu(heads(dg, GDN_H))).reshape(B_, L, -1)

    xbc, ssdc1 = causal_conv(sxbc, ssdc0, prm['ssd_conv_w'][i])
    xbc = jax.nn.silu(xbc + prm['ssd_conv_b'][i])
    sx, sB, sC = jnp.split(xbc, [SSD_DI, SSD_DI + SSD_G * SSD_N], axis=-1)
    xh = heads(sx, SSD_H)
    rep = SSD_H // SSD_G
    Bh = jnp.repeat(heads(sB, SSD_G), rep, axis=2)
    Ch = jnp.repeat(heads(sC, SSD_G), rep, axis=2)
    dt = jax.nn.softplus(sdt.astype(f32) + prm['ssd_dt_bias'][i].astype(f32))
    lsd = dt * -jnp.exp(prm['ssd_a_log'][i].astype(f32))
    yd, ssd1 = chunked_scalar_decay(Ch, Bh, xh * dt[..., None].astype(xh.dtype), lsd, ssd0)
    yd = yd + xh * prm['ssd_d'][i][:, None]
    yd = (yd.reshape(B_, L, SSD_DI) * jax.nn.silu(sz)).reshape(B_, L, SSD_G, SSD_DI // SSD_G)
    od = rmsnorm(yd, prm['ssd_norm'][i].reshape(SSD_G, -1)).reshape(B_, L, SSD_DI)

    br = jnp.stack([oa, ob, oc, od], axis=2)
    proj = jnp.einsum('blnv,nvd->blnd', br, prm['w_branch'][i])
    gates = jax.nn.sigmoid(mg.reshape(B_, L, N_BRANCH, D_MODEL))
    out = jnp.sum(gates * proj, axis=2) @ prm['w_out'][i]
    return out, (gla1, ret1, gdn1, gdnc1, ssd1, ssdc1)


def swiglu(x, wg, wu, wd):
    return (jax.nn.silu(x @ wg) * (x @ wu)) @ wd


def moe_swiglu(x, router, wg, wu, wd):
    logits = (x @ router).astype(jnp.float32)
    top_v, top_i = lax.top_k(logits, TOP_K)
    w = jax.nn.softmax(top_v, axis=-1)
    comb = jnp.einsum('blk,blke->ble', w, jax.nn.one_hot(top_i, N_EXPERTS, dtype=jnp.float32)).astype(x.dtype)
    y = jnp.zeros_like(x)
    for e in range(N_EXPERTS):
        y = y + comb[..., e:e + 1] * swiglu(x, wg[e], wu[e], wd[e])
    return y


def trunk(x, p, gla0, ret0, gdn0, gdnc0, ssd0, ssdc0, prm, pos0):
    h = x
    new = ([], [], [], [], [], [])
    for i in range(DEPTH):
        mix, st = token_mixers(rmsnorm(h, prm['norm_mix'][i]), gla0[i], ret0[i], gdn0[i],
                               gdnc0[i], ssd0[i], ssdc0[i], prm, i, pos0)
        h = h + mix
        u = rmsnorm(h, prm['norm_ffn'][i])
        j = i // 2
        if i % 2 == 0:
            h = h + swiglu(u, prm['ffn_w_gate'][j], prm['ffn_w_up'][j], prm['ffn_w_down'][j])
        else:
            h = h + moe_swiglu(u, prm['moe_router'][j], prm['moe_w_gate'][j], prm['moe_w_up'][j], prm['moe_w_down'][j])
        gate = jax.nn.sigmoid(rmsnorm(h, prm['norm_ple'][i]) @ prm['ple_w_gate'][i])
        h = h + gate * (p[i] @ prm['ple_w_proj'][i])
        for lst, s in zip(new, st):
            lst.append(s)
    y = rmsnorm(h, prm['norm_final'])
    return y, [jnp.stack(lst) for lst in new]


def setup_inputs(seed: int = 0) -> dict:
    key = jax.random.key(seed)
    ks = iter(jax.random.split(key, 64))
    f32 = jnp.float32

    def nrm(shape, scale):
        return jax.random.normal(next(ks), shape, f32) * scale

    def gain(shape):
        return 1.0 + nrm(shape, 0.01)

    def dt_bias(shape):
        dt = jnp.exp(jax.random.uniform(next(ks), shape, f32, minval=np.log(1e-3), maxval=np.log(1e-1)))
        return dt + jnp.log(-jnp.expm1(-dt))

    def a_log(shape):
        return jnp.log(jax.random.uniform(next(ks), shape, f32, minval=1.0, maxval=16.0))

    return {
        'x_prompt': nrm((BATCH, SEQ, D_MODEL), 1.0),
        'x_sample': nrm((DEC_BATCH, DEC_SEQ, D_MODEL), 1.0),
        'state_gla': nrm((DEPTH, DEC_BATCH, GLA_H, GLA_DK, GLA_DV), 1.0),
        'state_ret': nrm((DEPTH, DEC_BATCH, RET_H, RET_DK, RET_DV), 1.0),
        'state_gdn': nrm((DEPTH, DEC_BATCH, GDN_H, GDN_DK, GDN_DV), 1.0),
        'state_gdn_conv': nrm((DEPTH, DEC_BATCH, CONV_W - 1, GDN_CONV_C), 1.0),
        'state_ssd': nrm((DEPTH, DEC_BATCH, SSD_H, SSD_N, SSD_P), 1.0),
        'state_ssd_conv': nrm((DEPTH, DEC_BATCH, CONV_W - 1, SSD_CONV_C), 1.0),
        'p_prompt': nrm((DEPTH, BATCH, SEQ, D_PLE), 1.0),
        'p_sample': nrm((DEPTH, DEC_BATCH, DEC_SEQ, D_PLE), 1.0),
        'norm_mix': gain((DEPTH, D_MODEL)),
        'w_in': nrm((DEPTH, D_MODEL, D_IN), D_MODEL ** -0.5),
        'gla_w_gk': nrm((DEPTH, GLA_RANK, GLA_H * GLA_DK), GLA_RANK ** -0.5),
        'gla_b_gk': nrm((DEPTH, GLA_H * GLA_DK), 0.1),
        'gla_norm': gain((DEPTH, GLA_DV)),
        'ret_norm': gain((DEPTH, RET_DV)),
        'gdn_conv_w': nrm((DEPTH, CONV_W, GDN_CONV_C), CONV_W ** -0.5),
        'gdn_a_log': a_log((DEPTH, GDN_H)),
        'gdn_dt_bias': dt_bias((DEPTH, GDN_H)),
        'gdn_norm': gain((DEPTH, GDN_DV)),
        'ssd_conv_w': nrm((DEPTH, CONV_W, SSD_CONV_C), CONV_W ** -0.5),
        'ssd_conv_b': nrm((DEPTH, SSD_CONV_C), 0.02),
        'ssd_a_log': a_log((DEPTH, SSD_H)),
        'ssd_dt_bias': dt_bias((DEPTH, SSD_H)),
        'ssd_d': gain((DEPTH, SSD_H)),
        'ssd_norm': gain((DEPTH, SSD_DI)),
        'w_branch': nrm((DEPTH, N_BRANCH, BRANCH_W, D_MODEL), BRANCH_W ** -0.5),
        'w_out': nrm((DEPTH, D_MODEL, D_MODEL), D_MODEL ** -0.5),
        'norm_ffn': gain((DEPTH, D_MODEL)),
        'ffn_w_gate': nrm((N_DENSE, D_MODEL, D_FF), D_MODEL ** -0.5),
        'ffn_w_up': nrm((N_DENSE, D_MODEL, D_FF), D_MODEL ** -0.5),
        'ffn_w_down': nrm((N_DENSE, D_FF, D_MODEL), D_FF ** -0.5),
        'moe_router': nrm((N_MOE, D_MODEL, N_EXPERTS), D_MODEL ** -0.5),
        'moe_w_gate': nrm((N_MOE, N_EXPERTS, D_MODEL, D_FF), D_MODEL ** -0.5),
        'moe_w_up': nrm((N_MOE, N_EXPERTS, D_MODEL, D_FF), D_MODEL ** -0.5),
        'moe_w_down': nrm((N_MOE, N_EXPERTS, D_FF, D_MODEL), D_FF ** -0.5),
        'norm_ple': gain((DEPTH, D_MODEL)),
        'ple_w_gate': nrm((DEPTH, D_MODEL, D_MODEL), D_MODEL ** -0.5),
        'ple_w_proj': nrm((DEPTH, D_PLE, D_MODEL), D_PLE ** -0.5),
        'norm_final': gain((D_MODEL,)),
    }


def reference(x_prompt, x_sample, state_gla, state_ret, state_gdn, state_gdn_conv, state_ssd,
              state_ssd_conv, p_prompt, p_sample, norm_mix, w_in, gla_w_gk, gla_b_gk, gla_norm,
              ret_norm, gdn_conv_w, gdn_a_log, gdn_dt_bias, gdn_norm, ssd_conv_w, ssd_conv_b,
              ssd_a_log, ssd_dt_bias, ssd_d, ssd_norm, w_branch, w_out, norm_ffn, ffn_w_gate,
              ffn_w_up, ffn_w_down, moe_router, moe_w_gate, moe_w_up, moe_w_down, norm_ple,
              ple_w_gate, ple_w_proj, norm_final):
    prm = dict(norm_mix=norm_mix, w_in=w_in, gla_w_gk=gla_w_gk, gla_b_gk=gla_b_gk, gla_norm=gla_norm,
               ret_norm=ret_norm, gdn_conv_w=gdn_conv_w, gdn_a_log=gdn_a_log, gdn_dt_bias=gdn_dt_bias,
               gdn_norm=gdn_norm, ssd_conv_w=ssd_conv_w, ssd_conv_b=ssd_conv_b, ssd_a_log=ssd_a_log,
               ssd_dt_bias=ssd_dt_bias, ssd_d=ssd_d, ssd_norm=ssd_norm, w_branch=w_branch, w_out=w_out,
               norm_ffn=norm_ffn, ffn_w_gate=ffn_w_gate, ffn_w_up=ffn_w_up, ffn_w_down=ffn_w_down,
               moe_router=moe_router, moe_w_gate=moe_w_gate, moe_w_up=moe_w_up, moe_w_down=moe_w_down,
               norm_ple=norm_ple, ple_w_gate=ple_w_gate, ple_w_proj=ple_w_proj, norm_final=norm_final)
    bp = x_prompt.shape[0]
    dtp = x_prompt.dtype
    y_prompt, sp = trunk(
        x_prompt, p_prompt,
        jnp.zeros((DEPTH, bp, GLA_H, GLA_DK, GLA_DV), dtp),
        jnp.zeros((DEPTH, bp, RET_H, RET_DK, RET_DV), dtp),
        jnp.zeros((DEPTH, bp, GDN_H, GDN_DK, GDN_DV), dtp),
        jnp.zeros((DEPTH, bp, CONV_W - 1, GDN_CONV_C), dtp),
        jnp.zeros((DEPTH, bp, SSD_H, SSD_N, SSD_P), dtp),
        jnp.zeros((DEPTH, bp, CONV_W - 1, SSD_CONV_C), dtp),
        prm, 0)
    y_sample, ss = trunk(x_sample, p_sample, state_gla, state_ret, state_gdn, state_gdn_conv,
                         state_ssd, state_ssd_conv, prm, PAST_LEN)
    return (y_prompt, y_sample, sp[0], sp[1], sp[2], sp[3], sp[4], sp[5],
            ss[0], ss[1], ss[2], ss[3], ss[4], ss[5])
```

```python
import functools
import math

import numpy as np
import jax
import jax.numpy as jnp
from jax import lax
from jax.experimental import pallas as pl
from jax.experimental.pallas import tpu as pltpu

F32 = jnp.float32
BF16 = jnp.bfloat16
EPS = 1e-6

D_MODEL = 1024
D_PLE = 256
CONV_W = 4
N_BRANCH = 4
BRANCH_W = 512
MIX_H = 4
MIX_DK = 64
MIX_DV = 128
GLA_RANK = 16
GLA_GATE_NORM = 16.0
ROPE_BASE = 10000.0
GDN_CONV_C = 2 * MIX_H * MIX_DK + MIX_H * MIX_DV
SSD_H = 8
SSD_P = 64
SSD_N = 64
SSD_G = 2
SSD_DI = SSD_H * SSD_P
SSD_CONV_C = SSD_DI + 2 * SSD_G * SSD_N
D_FF = 2816
N_EXPERTS = 8
SPLITS = (256, 256, 512, GLA_RANK, 512,
          256, 256, 512, 512,
          GDN_CONV_C, MIX_H, MIX_H, 512,
          SSD_DI, SSD_CONV_C, SSD_H,
          N_BRANCH * D_MODEL)

LANE_GLR = 0
LANE_BETA = 16
LANE_GDA = 20
LANE_SDT = 24
SMALL_W = 128

CHUNK = 64
SEQ_BLOCK_CHUNKS = 4
BATCH_SEQS = 8
ROW_TILE = 512
FF_BLOCKS = 2
VMEM_LIMIT = 56 * 1024 * 1024
PAST_LEN = 16384


def _row_tile(n):
    for tm in (ROW_TILE, 256, 128, 64, 32, 16):
        if n % tm == 0:
            return tm
    raise ValueError(f"token count {n} is not a multiple of 16")


def _mm(a, b):
    return jnp.dot(a, b, preferred_element_type=F32)


def _mm_nt(a, b):
    return lax.dot_general(a, b, (((1,), (1,)), ((), ())), preferred_element_type=F32)


def _mm_tn(a, b):
    return lax.dot_general(a, b, (((0,), (0,)), ((), ())), preferred_element_type=F32)


def _split(x):
    hi = x.astype(BF16)
    lo = (x - hi.astype(F32)).astype(BF16)
    return hi, lo


def _mm_xl(x, m):
    hi, lo = _split(x)
    return _mm(hi, m) + _mm(lo, m)


def _mm_xr(m, x):
    hi, lo = _split(x)
    return _mm(m, hi) + _mm(m, lo)


def _sigmoid(x):
    return 1.0 / (1.0 + jnp.exp(-x))


def _silu(x):
    return x * _sigmoid(x)


def _softplus(x):
    return jnp.maximum(x, 0.0) + jnp.log1p(jnp.exp(-jnp.abs(x)))


def _rms(x, gain):
    ms = jnp.mean(x * x, axis=-1, keepdims=True)
    return x * lax.rsqrt(ms + EPS) * gain


def _log2(n):
    k = int(round(math.log2(n)))
    assert (1 << k) == n, n
    return k


class _Masks:
    def __init__(self, c, ls):
        self.c, self.ls = c, ls
        sh = _log2(ls)
        r = lax.broadcasted_iota(jnp.int32, (c, c), 0)
        q = lax.broadcasted_iota(jnp.int32, (c, c), 1)
        same = (r >> sh) == (q >> sh)
        self.incl = jnp.logical_and(same, q <= r)
        self.strict = jnp.logical_and(same, q < r)
        self.tri = jnp.where(self.incl, 1.0, 0.0).astype(BF16)
        self.tri_t = jnp.where(jnp.logical_and(same, r <= q), 1.0, 0.0).astype(BF16)
        last = ((r >> sh) << sh) + (ls - 1)
        self.lastsel = jnp.where(q == last, 1.0, 0.0).astype(BF16)
        self.eye = jnp.where(r == q, 1.0, 0.0).astype(F32)
        self.r, self.q = r, q
        rc = lax.broadcasted_iota(jnp.int32, (c, 1), 0)
        self.seq_of_row = rc >> sh
        self.t_col = (rc & (ls - 1)).astype(F32)
        self.nseq = c // ls

    def rows_of(self, b, x):
        if self.nseq == 1:
            return x
        return jnp.where(self.seq_of_row == b, x, jnp.zeros_like(x))

    def shift_matrix(self, nprev):
        c, ls = self.c, self.ls
        t = self.r & (ls - 1)
        blocks = []
        for s in (1, 2, 3):
            cur = jnp.logical_and(self.q == self.r - s, t >= s)
            prev = jnp.logical_and(self.q == self.r + (ls - s), t < s)
            cur = jnp.where(cur, 1.0, 0.0).astype(BF16)
            prev = jnp.where(prev, 1.0, 0.0).astype(BF16)
            blocks.append(jnp.concatenate([prev] * nprev + [cur], axis=1))
        return jnp.concatenate(blocks, axis=0)


def _expander(lane0, group, width):
    r = lax.broadcasted_iota(jnp.int32, (SMALL_W, width), 0)
    q = lax.broadcasted_iota(jnp.int32, (SMALL_W, width), 1)
    return jnp.where(r == lane0 + (q >> _log2(group)), 1.0, 0.0).astype(BF16)


def _causal_conv(u, prev_parts, w, masks, shift_m):
    c = masks.c
    x = jnp.concatenate(list(prev_parts) + [u], axis=0)
    y = _mm(shift_m, x)
    out = u.astype(F32) * w[3:4, :]
    for s in (1, 2, 3):
        out = out + y[(s - 1) * c:s * c, :] * w[3 - s:4 - s, :]
    return out


def _tri_inverse(m, masks):
    p = masks.eye - m
    mp = m
    n = 2
    while n < masks.ls:
        mpb = mp.astype(BF16)
        mp = _mm(mpb, mpb)
        p = p + _mm(p.astype(BF16), mp.astype(BF16))
        n *= 2
    return p


def _pair_decay(gc_col, gt_row, incl):
    d = gc_col - gt_row
    return jnp.where(incl, jnp.exp(jnp.where(incl, d, 0.0)), 0.0)


class _States:
    def __init__(self, seq_mode, s_scr, s0_ref, s1_ref):
        self.seq_mode, self.s_scr, self.s0_ref, self.s1_ref = seq_mode, s_scr, s0_ref, s1_ref

    def get(self, b, h):
        if self.seq_mode:
            return self.s_scr[h]
        return self.s0_ref[b, h]

    def put(self, b, h, val):
        if self.seq_mode:
            self.s_scr[h] = val
        else:
            self.s1_ref[b, h] = val


def _seq_prologue(s_scr, prev_scr=None):
    @pl.when(pl.program_id(1) == 0)
    def _():
        s_scr[...] = jnp.zeros_like(s_scr)
        if prev_scr is not None:
            prev_scr[...] = jnp.zeros_like(prev_scr)


def _seq_epilogue(s_scr, s1_ref):
    @pl.when(pl.program_id(1) == pl.num_programs(1) - 1)
    def _():
        s1_ref[0] = s_scr[...]


def _in_proj_kernel(h_ref, g_ref, wa, wb, wc, wd, wg, ws, wst, za, zb, zc, zd, zg, zs, zst):
    xn = _rms(h_ref[...], g_ref[...]).astype(BF16)
    for w_ref, o_ref in ((wa, za), (wb, zb), (wc, zc), (wd, zd), (wg, zg)):
        width = o_ref.shape[1]
        for j in range(0, width, 512):
            jw = min(512, width - j)
            o_ref[:, j:j + jw] = _mm(xn, w_ref[:, j:j + jw]).astype(o_ref.dtype)
    zs[...] = _mm(xn, ws[...])
    zst[...] = _mm_nt(wst[...], xn)


def _in_proj(h, gain, wts):
    n = h.shape[0]
    tm = _row_tile(n)
    wa, wb, wc, wd, wg, ws, wst = wts
    const = lambda i: (0, 0)
    row = lambda i: (i, 0)

    def wspec(w):
        return pl.BlockSpec(w.shape, const, pipeline_mode=pl.Buffered(1))

    outs = [jax.ShapeDtypeStruct((n, w.shape[1]), BF16) for w in (wa, wb, wc, wd, wg)]
    outs += [jax.ShapeDtypeStruct((n, SMALL_W), F32), jax.ShapeDtypeStruct((SMALL_W, n), F32)]
    out_specs = [pl.BlockSpec((tm, w.shape[1]), row) for w in (wa, wb, wc, wd, wg)]
    out_specs += [pl.BlockSpec((tm, SMALL_W), row), pl.BlockSpec((SMALL_W, tm), lambda i: (0, i))]
    return pl.pallas_call(
        _in_proj_kernel,
        grid=(n // tm,),
        in_specs=[pl.BlockSpec((tm, D_MODEL), row), pl.BlockSpec((1, D_MODEL), const)]
        + [wspec(w) for w in wts],
        out_specs=out_specs,
        out_shape=outs,
        compiler_params=pltpu.CompilerParams(dimension_semantics=("parallel",), vmem_limit_bytes=VMEM_LIMIT),
        name="in_proj",
    )(h, gain, *wts)


def _gla_kernel(*refs, c, cpb, ls, seq_mode):
    if seq_mode:
        z_ref, zs_ref, wgk_ref, bgk_ref, gain_ref, o_ref, s1_ref, s_scr = refs
        s0_ref = None
        _seq_prologue(s_scr)
    else:
        z_ref, zs_ref, wgk_ref, bgk_ref, gain_ref, s0_ref, o_ref, s1_ref = refs
        s_scr = None
    st = _States(seq_mode, s_scr, s0_ref, s1_ref)
    mk = _Masks(c, ls)
    r64 = lax.broadcasted_iota(jnp.int32, (MIX_DK, MIX_DK), 0)
    q64 = lax.broadcasted_iota(jnp.int32, (MIX_DK, MIX_DK), 1)
    eye64 = r64 == q64
    ones_dv = jnp.ones((MIX_DK, MIX_DV), BF16)
    for ci in range(cpb):
        rows = slice(ci * c, (ci + 1) * c)
        z = z_ref[rows, :]
        q = z[:, 0:256].astype(F32) * (MIX_DK ** -0.5)
        k = z[:, 256:512].astype(F32)
        v = z[:, 512:1024]
        gg = z[:, 1024:1536].astype(F32)
        pre = _mm(zs_ref[rows, :].astype(BF16), wgk_ref[...]) + bgk_ref[...]
        lg = -_softplus(-pre) * (1.0 / GLA_GATE_NORM)
        g = _mm_xr(mk.tri, lg)
        eg = jnp.exp(g)
        qe = (q * eg).astype(BF16)
        ke = (k * jnp.exp(-g)).astype(BF16)
        glast = _mm_xr(mk.lastsel, g)
        kd = (k * jnp.exp(glast - g)).astype(BF16)
        el = jnp.exp(glast)
        for h in range(MIX_H):
            sl = slice(MIX_DK * h, MIX_DK * (h + 1))
            vl = slice(MIX_DV * h, MIX_DV * (h + 1))
            a = jnp.where(mk.incl, _mm_nt(qe[:, sl], ke[:, sl]), 0.0).astype(BF16)
            o = _mm(a, v[:, vl])
            for b in range(mk.nseq):
                s = st.get(b, h)
                o = o + _mm(mk.rows_of(b, qe[:, sl]), s.astype(BF16))
                el_row = el[b * ls:b * ls + 1, sl]
                diag = jnp.where(eye64, jnp.broadcast_to(el_row, (MIX_DK, MIX_DK)), 0.0)
                el_col = _mm_xl(diag, ones_dv)
                st.put(b, h, s * el_col + _mm_tn(mk.rows_of(b, kd[:, sl]), v[:, vl]))
            y = _rms(o, gain_ref[...]) * _silu(gg[:, vl])
            o_ref[rows, vl] = y.astype(o_ref.dtype)
    if seq_mode:
        _seq_epilogue(s_scr, s1_ref)


def _ret_kernel(*refs, c, cpb, ls, seq_mode):
    if seq_mode:
        z_ref, cos_ref, sin_ref, gain_ref, o_ref, s1_ref, s_scr = refs
        s0_ref = None
        _seq_prologue(s_scr)
    else:
        z_ref, cos_ref, sin_ref, gain_ref, s0_ref, o_ref, s1_ref = refs
        s_scr = None
    st = _States(seq_mode, s_scr, s0_ref, s1_ref)
    mk = _Masks(c, ls)
    lgam = [math.log(1.0 - 2.0 ** (-5.0 - h)) for h in range(MIX_H)]
    lane = lax.broadcasted_iota(jnp.int32, (1, MIX_H * MIX_DK), 1)
    lg_row = jnp.zeros((1, MIX_H * MIX_DK), F32)
    for h in range(MIX_H):
        lg_row = jnp.where((lane >> 6) == h, lgam[h], lg_row)
    first_half = (lane & 63) < 32
    eg = jnp.exp((mk.t_col + 1.0) * lg_row)
    ed = jnp.exp((ls - 1.0 - mk.t_col) * lg_row)
    dt_pos = ((mk.r & (ls - 1)) - (mk.q & (ls - 1))).astype(F32)
    dm = [jnp.where(mk.incl, jnp.exp(jnp.where(mk.incl, dt_pos * lgam[h], 0.0)), 0.0) for h in range(MIX_H)]

    def rope(x, cs, sn):
        sw = jnp.where(first_half, pltpu.roll(x, 256 - 32, 1), pltpu.roll(x, 32, 1))
        return x * cs + sw * sn

    for ci in range(cpb):
        rows = slice(ci * c, (ci + 1) * c)
        z = z_ref[rows, :]
        cs = cos_ref[rows, :]
        sn = sin_ref[rows, :]
        qr = rope(z[:, 0:256].astype(F32), cs, sn)
        kr = rope(z[:, 256:512].astype(F32), cs, sn) * (MIX_DK ** -0.5)
        v = z[:, 512:1024]
        rg = z[:, 1024:1536].astype(F32)
        qb = qr.astype(BF16)
        kb = kr.astype(BF16)
        qe = (qr * eg).astype(BF16)
        kd = (kr * ed).astype(BF16)
        for h in range(MIX_H):
            sl = slice(MIX_DK * h, MIX_DK * (h + 1))
            vl = slice(MIX_DV * h, MIX_DV * (h + 1))
            a = (_mm_nt(qb[:, sl], kb[:, sl]) * dm[h]).astype(BF16)
            o = _mm(a, v[:, vl])
            for b in range(mk.nseq):
                s = st.get(b, h)
                o = o + _mm(mk.rows_of(b, qe[:, sl]), s.astype(BF16))
                st.put(b, h, s * math.exp(lgam[h] * ls) + _mm_tn(mk.rows_of(b, kd[:, sl]), v[:, vl]))
            y = _rms(o, gain_ref[...]) * _silu(rg[:, vl])
            o_ref[rows, vl] = y.astype(o_ref.dtype)
    if seq_mode:
        _seq_epilogue(s_scr, s1_ref)


def _gdn_kernel(*refs, c, cpb, ls, seq_mode):
    if seq_mode:
        (z_ref, zs_ref, zst_ref, cw_ref, prow_ref, pcol_ref, gain_ref,
         o_ref, s1_ref, s_scr, prev_scr) = refs
        s0_ref = prev_ref = None
        _seq_prologue(s_scr, prev_scr)
    else:
        (z_ref, zs_ref, zst_ref, cw_ref, prow_ref, pcol_ref, gain_ref, prev_ref, s0_ref,
         o_ref, s1_ref) = refs
        s_scr = prev_scr = None
    st = _States(seq_mode, s_scr, s0_ref, s1_ref)
    mk = _Masks(c, ls)
    shift_m = mk.shift_matrix(1 if seq_mode else 2)
    r = lax.broadcasted_iota(jnp.int32, (256, 256), 0)
    q = lax.broadcasted_iota(jnp.int32, (256, 256), 1)
    ones_bd = jnp.where((r >> 6) == (q >> 6), 1.0, 0.0).astype(BF16)
    e_beta_k = _expander(LANE_BETA, MIX_DK, MIX_H * MIX_DK)
    e_beta_v = _expander(LANE_BETA, MIX_DV, MIX_H * MIX_DV)
    e_g_k = _expander(LANE_GDA, MIX_DK, MIX_H * MIX_DK)
    e_g_v = _expander(LANE_GDA, MIX_DV, MIX_H * MIX_DV)
    a_row = -jnp.exp(prow_ref[0:1, :])
    b_row = prow_ref[1:2, :]
    a_col = -jnp.exp(pcol_ref[:, 0:1])
    b_col = pcol_ref[:, 1:2]
    cw = cw_ref[...]
    for ci in range(cpb):
        rows = slice(ci * c, (ci + 1) * c)
        z = z_ref[rows, :]
        u = z[:, 0:GDN_CONV_C]
        dg = z[:, GDN_CONV_C:GDN_CONV_C + 512].astype(F32)
        if seq_mode:
            prev_parts = [prev_scr[...].astype(BF16)]
        else:
            prev_parts = list(_split(prev_ref[...]))
        cqkv = _silu(_causal_conv(u, prev_parts, cw, mk, shift_m))
        if seq_mode:
            prev_scr[...] = u.astype(F32)
        cq = cqkv[:, 0:256]
        ck = cqkv[:, 256:512]
        cv = cqkv[:, 512:1024]
        qn = cq * lax.rsqrt(_mm_xl(cq * cq, ones_bd) + EPS) * (MIX_DK ** -0.5)
        kn = ck * lax.rsqrt(_mm_xl(ck * ck, ones_bd) + EPS)
        zs = zs_ref[rows, :]
        zst = zst_ref[0, :, ci * c:(ci + 1) * c]
        beta = _sigmoid(zs)
        lgd = a_row * _softplus(zs + b_row)
        lgd_t = a_col * _softplus(zst + b_col)
        gc = _mm_xr(mk.tri, lgd)
        gt = _mm_xl(lgd_t, mk.tri_t)
        glast = _mm_xr(mk.lastsel, gc)
        beta_k = _mm_xl(beta, e_beta_k)
        beta_v = _mm_xl(beta, e_beta_v)
        eg_k = _mm_xl(jnp.exp(gc), e_g_k)
        dl_k = _mm_xl(jnp.exp(glast - gc), e_g_k)
        el_v = _mm_xl(jnp.exp(glast), e_g_v)
        kbeta = kn * beta_k
        knb = kn.astype(BF16)
        kbb = kbeta.astype(BF16)
        qnb = qn.astype(BF16)
        vbeta = (cv * beta_v).astype(BF16)
        kbe = (kbeta * eg_k).astype(BF16)
        qe = (qn * eg_k).astype(BF16)
        kd = (kn * dl_k).astype(BF16)
        for h in range(MIX_H):
            sl = slice(MIX_DK * h, MIX_DK * (h + 1))
            vl = slice(MIX_DV * h, MIX_DV * (h + 1))
            lane = LANE_GDA + h
            dm = _pair_decay(gc[:, lane:lane + 1], gt[lane:lane + 1, :], mk.incl)
            m = _mm_nt(kbb[:, sl], knb[:, sl]) * jnp.where(mk.strict, dm, 0.0)
            tinv = _tri_inverse(m, mk).astype(BF16)
            uu = _mm(tinv, vbeta[:, vl])
            ww = _mm(tinv, kbe[:, sl]).astype(BF16)
            vn = uu
            for b in range(mk.nseq):
                vn = vn - _mm(mk.rows_of(b, ww), st.get(b, h).astype(BF16))
            vnb = vn.astype(BF16)
            a = (_mm_nt(qnb[:, sl], knb[:, sl]) * dm).astype(BF16)
            o = _mm(a, vnb)
            for b in range(mk.nseq):
                s = st.get(b, h)
                o = o + _mm(mk.rows_of(b, qe[:, sl]), s.astype(BF16))
                dec = el_v[b * ls:b * ls + 1, vl]
                st.put(b, h, s * dec + _mm_tn(mk.rows_of(b, kd[:, sl]), vnb))
            y = _rms(o, gain_ref[...]) * _silu(dg[:, vl])
            o_ref[rows, vl] = y.astype(o_ref.dtype)
    if seq_mode:
        _seq_epilogue(s_scr, s1_ref)


def _ssd_kernel(*refs, c, cpb, ls, seq_mode):
    gw = SSD_DI // SSD_G
    hpg = SSD_H // SSD_G
    if seq_mode:
        (z_ref, zs_ref, zst_ref, cw_ref, cb_ref, prow_ref, pcol_ref, dx_ref, gain_ref,
         o_ref, s1_ref, s_scr, prev_scr) = refs
        s0_ref = prev_ref = None
        _seq_prologue(s_scr, prev_scr)
    else:
        (z_ref, zs_ref, zst_ref, cw_ref, cb_ref, prow_ref, pcol_ref, dx_ref, gain_ref, prev_ref, s0_ref,
         o_ref, s1_ref) = refs
        s_scr = prev_scr = None
    mk = _Masks(c, ls)
    shift_m = mk.shift_matrix(1 if seq_mode else 2)
    e_x = _expander(LANE_SDT, SSD_P, SSD_DI)
    a_row = -jnp.exp(prow_ref[0:1, :])
    b_row = prow_ref[1:2, :]
    a_col = -jnp.exp(pcol_ref[:, 0:1])
    b_col = pcol_ref[:, 1:2]
    cw = cw_ref[...]
    lane_g = lax.broadcasted_iota(jnp.int32, (1, gw), 1) >> 6

    def get_state(b, g):
        if seq_mode:
            return s_scr[g]
        return jnp.concatenate([s0_ref[b, g * hpg + j] for j in range(hpg)], axis=-1)

    def put_state(b, g, val):
        if seq_mode:
            s_scr[g] = val
        else:
            for j in range(hpg):
                s1_ref[b, g * hpg + j] = val[:, SSD_P * j:SSD_P * (j + 1)]

    for ci in range(cpb):
        rows = slice(ci * c, (ci + 1) * c)
        z = z_ref[rows, :]
        sz = z[:, 0:SSD_DI].astype(F32)
        u = z[:, SSD_DI:SSD_DI + SSD_CONV_C]
        if seq_mode:
            prev_parts = [prev_scr[...].astype(BF16)]
        else:
            prev_parts = list(_split(prev_ref[...]))
        xbc = _silu(_causal_conv(u, prev_parts, cw, mk, shift_m) + cb_ref[...])
        if seq_mode:
            prev_scr[...] = u.astype(F32)
        sx = xbc[:, 0:SSD_DI]
        s_b = xbc[:, SSD_DI:SSD_DI + SSD_G * SSD_N]
        s_c = xbc[:, SSD_DI + SSD_G * SSD_N:SSD_CONV_C]
        zs = zs_ref[rows, :]
        zst = zst_ref[0, :, ci * c:(ci + 1) * c]
        dt = _softplus(zs + b_row)
        lsd = dt * a_row
        lsd_t = _softplus(zst + b_col) * a_col
        gc = _mm_xr(mk.tri, lsd)
        gt = _mm_xl(lsd_t, mk.tri_t)
        glast = _mm_xr(mk.lastsel, gc)
        dt_x = _mm_xl(dt, e_x)
        eg_x = _mm_xl(jnp.exp(gc), e_x)
        dl_x = _mm_xl(jnp.exp(glast - gc), e_x)
        el_x = _mm_xl(jnp.exp(glast), e_x)
        v = sx * dt_x
        vd = (v * dl_x).astype(BF16)
        for g in range(SSD_G):
            gl = slice(gw * g, gw * (g + 1))
            cg = s_c[:, SSD_N * g:SSD_N * (g + 1)].astype(BF16)
            bg = s_b[:, SSD_N * g:SSD_N * (g + 1)].astype(BF16)
            ag = _mm_nt(cg, bg)
            vg = v[:, gl]
            o = jnp.zeros((c, gw), F32)
            for j in range(hpg):
                lane = LANE_SDT + g * hpg + j
                dm = _pair_decay(gc[:, lane:lane + 1], gt[lane:lane + 1, :], mk.incl)
                vj = jnp.where(lane_g == j, vg, 0.0).astype(BF16)
                o = o + _mm((ag * dm).astype(BF16), vj)
            o_state = jnp.zeros((c, gw), F32)
            for b in range(mk.nseq):
                s = get_state(b, g)
                o_state = o_state + _mm(mk.rows_of(b, cg), s.astype(BF16))
                dec = el_x[b * ls:b * ls + 1, gl]
                put_state(b, g, s * dec + _mm_tn(mk.rows_of(b, bg), vd[:, gl]))
            y = o + o_state * eg_x[:, gl] + sx[:, gl] * dx_ref[:, gl]
            y = y * _silu(sz[:, gl])
            o_ref[rows, gl] = _rms(y, gain_ref[:, gl]).astype(o_ref.dtype)
    if seq_mode:
        @pl.when(pl.program_id(1) == pl.num_programs(1) - 1)
        def _():
            for g in range(SSD_G):
                for j in range(hpg):
                    s1_ref[0, g * hpg + j] = s_scr[g][:, SSD_P * j:SSD_P * (j + 1)]


def _full_spec(a, grid_rank):
    nd = a.ndim
    if grid_rank == 1:
        return pl.BlockSpec(a.shape, lambda i: (0,) * nd)
    return pl.BlockSpec(a.shape, lambda i, j: (0,) * nd)


def _mixer_seq(kernel_fn, name, z, extra_rows, params, nbatch, seqlen, state_shape, scratch):
    c, cpb = CHUNK, SEQ_BLOCK_CHUNKS
    rblk = c * cpb
    nblk = seqlen // rblk
    rowmap = lambda b, j: (b * nblk + j, 0)
    in_specs = [pl.BlockSpec((rblk, z.shape[1]), rowmap)]
    args = [z]
    for a, kind in extra_rows:
        if kind == "rows":
            in_specs.append(pl.BlockSpec((rblk, a.shape[1]), rowmap))
        elif kind == "zst":
            in_specs.append(pl.BlockSpec((1, a.shape[1], rblk), lambda b, j: (b * nblk + j, 0, 0)))
        else:
            in_specs.append(pl.BlockSpec((rblk, a.shape[1]), lambda b, j: (j, 0)))
        args.append(a)
    for a in params:
        in_specs.append(_full_spec(a, 2))
        args.append(a)
    sblk = (1,) + state_shape
    out_shape = [jax.ShapeDtypeStruct((nbatch * seqlen, BRANCH_W), BF16),
                 jax.ShapeDtypeStruct((nbatch,) + state_shape, F32)]
    out_specs = [pl.BlockSpec((rblk, BRANCH_W), rowmap),
                 pl.BlockSpec(sblk, lambda b, j: (b,) + (0,) * len(state_shape))]
    return pl.pallas_call(
        functools.partial(kernel_fn, c=c, cpb=cpb, ls=c, seq_mode=True),
        grid=(nbatch, nblk),
        in_specs=in_specs,
        out_specs=out_specs,
        out_shape=out_shape,
        scratch_shapes=scratch,
        compiler_params=pltpu.CompilerParams(dimension_semantics=("parallel", "arbitrary"),
                                             vmem_limit_bytes=VMEM_LIMIT),
        name=name,
    )(*args)


def _mixer_batch(kernel_fn, name, z, row_off, extra_rows, params, per_seq, nbatch, seqlen, state_shape):
    sb = BATCH_SEQS
    c = sb * seqlen
    nsteps = nbatch // sb
    off = row_off // c
    rowmap = lambda i: (off + i, 0)
    in_specs = [pl.BlockSpec((c, z.shape[1]), rowmap)]
    args = [z]
    for a, kind in extra_rows:
        if kind == "rows":
            in_specs.append(pl.BlockSpec((c, a.shape[1]), rowmap))
        elif kind == "zst":
            in_specs.append(pl.BlockSpec((1, a.shape[1], c), lambda i: (i, 0, 0)))
        else:
            in_specs.append(pl.BlockSpec((c, a.shape[1]), lambda i: (0, 0)))
        args.append(a)
    for a in params:
        in_specs.append(_full_spec(a, 1))
        args.append(a)
    for a in per_seq:
        if a.ndim == 2:
            in_specs.append(pl.BlockSpec((c, a.shape[1]), lambda i: (i, 0)))
        else:
            in_specs.append(pl.BlockSpec((sb,) + a.shape[1:], lambda i: (i,) + (0,) * (a.ndim - 1)))
        args.append(a)
    out_shape = [jax.ShapeDtypeStruct((nbatch * seqlen, BRANCH_W), BF16),
                 jax.ShapeDtypeStruct((nbatch,) + state_shape, F32)]
    out_specs = [pl.BlockSpec((c, BRANCH_W), lambda i: (i, 0)),
                 pl.BlockSpec((sb,) + state_shape, lambda i: (i,) + (0,) * len(state_shape))]
    return pl.pallas_call(
        functools.partial(kernel_fn, c=c, cpb=1, ls=seqlen, seq_mode=False),
        grid=(nsteps,),
        in_specs=in_specs,
        out_specs=out_specs,
        out_shape=out_shape,
        compiler_params=pltpu.CompilerParams(dimension_semantics=("parallel",), vmem_limit_bytes=VMEM_LIMIT),
        name=name,
    )(*args)


def _merge_kernel(oa, ob, oc, od, zg, h_ref, wbr, wout, out_ref):
    acc = jnp.zeros(out_ref.shape, F32)
    for n, o_ref in enumerate((oa, ob, oc, od)):
        gate = _sigmoid(zg[:, n * D_MODEL:(n + 1) * D_MODEL].astype(F32))
        acc = acc + gate * _mm(o_ref[...], wbr[n])
    out_ref[...] = h_ref[...] + _mm(acc.astype(BF16), wout[...])


def _merge(branches, zg, h, wbr, wout):
    n = h.shape[0]
    tm = _row_tile(n)
    row = lambda i: (i, 0)
    in_specs = [pl.BlockSpec((tm, BRANCH_W), row) for _ in range(4)]
    in_specs += [pl.BlockSpec((tm, N_BRANCH * D_MODEL), row), pl.BlockSpec((tm, D_MODEL), row),
                 pl.BlockSpec(wbr.shape, lambda i: (0, 0, 0)), pl.BlockSpec(wout.shape, lambda i: (0, 0))]
    return pl.pallas_call(
        _merge_kernel,
        grid=(n // tm,),
        in_specs=in_specs,
        out_specs=pl.BlockSpec((tm, D_MODEL), row),
        out_shape=jax.ShapeDtypeStruct((n, D_MODEL), F32),
        compiler_params=pltpu.CompilerParams(dimension_semantics=("parallel",), vmem_limit_bytes=VMEM_LIMIT),
        name="merge",
    )(*branches, zg, h, wbr, wout)


def _ffn_kernel(*refs, moe):
    if moe:
        h_ref, g_ref, rt_ref, wg_ref, wu_ref, wd_ref, out_ref, u_scr, acc_scr, comb_scr = refs
    else:
        h_ref, g_ref, wg_ref, wu_ref, wd_ref, out_ref, u_scr, acc_scr = refs
    e = pl.program_id(1)
    f = pl.program_id(2)

    @pl.when(jnp.logical_and(e == 0, f == 0))
    def _():
        u = _rms(h_ref[...], g_ref[...])
        u_scr[...] = u.astype(BF16)
        acc_scr[...] = jnp.zeros_like(acc_scr)
        if moe:
            uh, ul = _split(u)
            rh, rl = _split(rt_ref[...])
            logits = _mm(uh, rh) + (_mm(uh, rl) + _mm(ul, rh))
            lane = lax.broadcasted_iota(jnp.int32, logits.shape, 1).astype(F32)
            neg = -3.0e38
            lg = jnp.where(lane < N_EXPERTS, logits, neg)
            m1 = jnp.max(lg, axis=-1, keepdims=True)
            i1 = jnp.min(jnp.where(lg == m1, lane, float(SMALL_W)), axis=-1, keepdims=True)
            lg2 = jnp.where(lane == i1, neg, lg)
            m2 = jnp.max(lg2, axis=-1, keepdims=True)
            i2 = jnp.min(jnp.where(lg2 == m2, lane, float(SMALL_W)), axis=-1, keepdims=True)
            e2 = jnp.exp(m2 - m1)
            w1 = 1.0 / (1.0 + e2)
            w2 = e2 / (1.0 + e2)
            comb_scr[...] = jnp.where(lane == i1, w1, 0.0) + jnp.where(lane == i2, w2, 0.0)

    u = u_scr[...]
    a = (_silu(_mm(u, wg_ref[0])) * _mm(u, wu_ref[0])).astype(BF16)
    y = _mm(a, wd_ref[0])
    if moe:
        lane = lax.broadcasted_iota(jnp.int32, comb_scr.shape, 1)
        ce = jnp.sum(jnp.where(lane == e, comb_scr[...], 0.0), axis=-1, keepdims=True)
        y = y * ce
    acc_scr[...] += y

    @pl.when(jnp.logical_and(e == pl.num_programs(1) - 1, f == pl.num_programs(2) - 1))
    def _():
        out_ref[...] = h_ref[...] + acc_scr[...]


def _ffn(h, gain, wg, wu, wd, router=None):
    n = h.shape[0]
    tm = _row_tile(n)
    ne = wg.shape[0]
    tf = D_FF // FF_BLOCKS
    moe = router is not None
    row = lambda i, e, f: (i, 0)
    in_specs = [pl.BlockSpec((tm, D_MODEL), row), pl.BlockSpec((1, D_MODEL), lambda i, e, f: (0, 0))]
    args = [h, gain]
    if moe:
        in_specs.append(pl.BlockSpec(router.shape, lambda i, e, f: (0, 0)))
        args.append(router)
    in_specs += [pl.BlockSpec((1, D_MODEL, tf), lambda i, e, f: (e, 0, f)),
                 pl.BlockSpec((1, D_MODEL, tf), lambda i, e, f: (e, 0, f)),
                 pl.BlockSpec((1, tf, D_MODEL), lambda i, e, f: (e, f, 0))]
    args += [wg, wu, wd]
    scratch = [pltpu.VMEM((tm, D_MODEL), BF16), pltpu.VMEM((tm, D_MODEL), F32)]
    if moe:
        scratch.append(pltpu.VMEM((tm, SMALL_W), F32))
    return pl.pallas_call(
        functools.partial(_ffn_kernel, moe=moe),
        grid=(n // tm, ne, FF_BLOCKS),
        in_specs=in_specs,
        out_specs=pl.BlockSpec((tm, D_MODEL), row),
        out_shape=jax.ShapeDtypeStruct((n, D_MODEL), F32),
        scratch_shapes=scratch,
        compiler_params=pltpu.CompilerParams(dimension_semantics=("parallel", "arbitrary", "arbitrary"),
                                             vmem_limit_bytes=VMEM_LIMIT),
        name="moe_ffn" if moe else "dense_ffn",
    )(*args)


def _ple_kernel(*refs, final):
    if final:
        h_ref, p_ref, g_ref, wgate, wproj, gf_ref, out_ref = refs
    else:
        h_ref, p_ref, g_ref, wgate, wproj, out_ref = refs
    h = h_ref[...]
    gate = _sigmoid(_mm(_rms(h, g_ref[...]).astype(BF16), wgate[...]))
    h = h + gate * _mm(p_ref[...].astype(BF16), wproj[...])
    if final:
        h = _rms(h, gf_ref[...])
    out_ref[...] = h


def _ple(h, p, gain, wgate, wproj, gain_final=None):
    n = h.shape[0]
    tm = _row_tile(n)
    final = gain_final is not None
    row = lambda i: (i, 0)
    const = lambda i: (0, 0)
    in_specs = [pl.BlockSpec((tm, D_MODEL), row), pl.BlockSpec((tm, D_PLE), row),
                pl.BlockSpec((1, D_MODEL), const), pl.BlockSpec(wgate.shape, const),
                pl.BlockSpec(wproj.shape, const)]
    args = [h, p, gain, wgate, wproj]
    if final:
        in_specs.append(pl.BlockSpec((1, D_MODEL), const))
        args.append(gain_final)
    return pl.pallas_call(
        functools.partial(_ple_kernel, final=final),
        grid=(n // tm,),
        in_specs=in_specs,
        out_specs=pl.BlockSpec((tm, D_MODEL), row),
        out_shape=jax.ShapeDtypeStruct((n, D_MODEL), F32),
        compiler_params=pltpu.CompilerParams(dimension_semantics=("parallel",), vmem_limit_bytes=VMEM_LIMIT),
        name="ple",
    )(*args)


def _pad_lanes(x, lane0):
    w = x.shape[-1]
    pad = [(0, 0)] * (x.ndim - 1) + [(lane0, SMALL_W - lane0 - w)]
    return jnp.pad(x, pad)


def _in_proj_weights(w):
    cuts = np.cumsum(SPLITS)[:-1].tolist()
    (gq, gk, gv, glr, gg, rq, rk, rv, rg, dqkv, db, da, dg, sz, sxbc, sdt, mg) = jnp.split(w, cuts, axis=-1)
    cat = lambda *xs: jnp.concatenate(xs, axis=-1).astype(BF16)
    small = jnp.concatenate([glr, db, da, sdt], axis=-1)
    small = jnp.pad(small, ((0, 0), (0, SMALL_W - small.shape[1]))).astype(BF16)
    return (cat(gq, gk, gv, gg), cat(rq, rk, rv, rg), cat(dqkv, dg), cat(sz, sxbc), mg.astype(BF16),
            small, small.T)


def _rope_tables(pos):
    half = MIX_DK // 2
    inv = ROPE_BASE ** (-jnp.arange(half, dtype=F32) / half)
    ang = jnp.asarray(pos).astype(F32)[:, None] * inv[None, :]
    cos, sin = jnp.cos(ang), jnp.sin(ang)
    cos_t = jnp.tile(jnp.concatenate([cos, cos], axis=-1), (1, MIX_H))
    sin_t = jnp.tile(jnp.concatenate([-sin, sin], axis=-1), (1, MIX_H))
    return cos_t, sin_t


def _block_rows_t(zst, row0, nrows, rblk):
    x = zst[:, row0:row0 + nrows].reshape(SMALL_W, nrows // rblk, rblk)
    return jnp.transpose(x, (1, 0, 2))


def _conv_prev(state_conv):
    b, _, cdim = state_conv.shape
    return jnp.pad(state_conv, ((0, 0), (1, 0), (0, 0))).reshape(b * CONV_W, cdim)


def kernel(x_prompt, x_sample, state_gla, state_ret, state_gdn, state_gdn_conv, state_ssd, state_ssd_conv, p_prompt, p_sample, norm_mix, w_in, gla_w_gk, gla_b_gk, gla_norm, ret_norm, gdn_conv_w, gdn_a_log, gdn_dt_bias, gdn_norm, ssd_conv_w, ssd_conv_b, ssd_a_log, ssd_dt_bias, ssd_d, ssd_norm, w_branch, w_out, norm_ffn, ffn_w_gate, ffn_w_up, ffn_w_down, moe_router, moe_w_gate, moe_w_up, moe_w_down, norm_ple, ple_w_gate, ple_w_proj, norm_final):
    bp, lp, _ = x_prompt.shape
    bs, lsmp, _ = x_sample.shape
    depth = w_in.shape[0]
    n_p = bp * lp
    n_s = bs * lsmp
    h = jnp.concatenate([x_prompt.reshape(n_p, D_MODEL), x_sample.reshape(n_s, D_MODEL)], axis=0)
    cos_p, sin_p = _rope_tables(np.arange(lp))
    cos_s, sin_s = _rope_tables(PAST_LEN + (np.arange(BATCH_SEQS * lsmp) % lsmp))
    rblk = CHUNK * SEQ_BLOCK_CHUNKS
    cb = BATCH_SEQS * lsmp
    mix_state = (MIX_H, MIX_DK, MIX_DV)
    ssd_state = (SSD_H, SSD_N, SSD_P)

    outs_p = [[] for _ in range(6)]
    outs_s = [[] for _ in range(6)]
    for i in range(depth):
        wts = _in_proj_weights(w_in[i])
        za, zb, zc, zd, zg, zs, zst = _in_proj(h, norm_mix[i][None, :], wts)
        zst_p = _block_rows_t(zst, 0, n_p, rblk)
        zst_s = _block_rows_t(zst, n_p, n_s, cb)

        wgk = jnp.pad(gla_w_gk[i], ((LANE_GLR, SMALL_W - LANE_GLR - GLA_RANK), (0, 0))).astype(BF16)
        bgk = gla_b_gk[i][None, :]
        gla_par = [wgk, bgk, gla_norm[i][None, :]]
        oa_p, gla_p = _mixer_seq(_gla_kernel, "gla_seq", za, [(zs, "rows")], gla_par, bp, lp, mix_state,
                                 [pltpu.VMEM(mix_state, F32)])
        oa_s, gla_s = _mixer_batch(_gla_kernel, "gla_batch", za, n_p, [(zs, "rows")], gla_par,
                                   [state_gla[i]], bs, lsmp, mix_state)
        ret_par = [ret_norm[i][None, :]]
        ob_p, ret_p = _mixer_seq(_ret_kernel, "ret_seq", zb, [(cos_p, "pos"), (sin_p, "pos")], ret_par,
                                 bp, lp, mix_state, [pltpu.VMEM(mix_state, F32)])
        ob_s, ret_s = _mixer_batch(_ret_kernel, "ret_batch", zb, n_p, [(cos_s, "pos"), (sin_s, "pos")],
                                   ret_par, [state_ret[i]], bs, lsmp, mix_state)
        gdn_prow = jnp.concatenate([_pad_lanes(gdn_a_log[i][None, :], LANE_GDA),
                                    _pad_lanes(gdn_dt_bias[i][None, :], LANE_GDA)], axis=0)
        gdn_par = [gdn_conv_w[i], gdn_prow, gdn_prow.T, gdn_norm[i][None, :]]
        oc_p, gdn_p = _mixer_seq(_gdn_kernel, "gdn_seq", zc, [(zs, "rows"), (zst_p, "zst")], gdn_par,
                                 bp, lp, mix_state,
                                 [pltpu.VMEM(mix_state, F32), pltpu.VMEM((CHUNK, GDN_CONV_C), F32)])
        oc_s, gdn_s = _mixer_batch(_gdn_kernel, "gdn_batch", zc, n_p, [(zs, "rows"), (zst_s, "zst")], gdn_par,
                                   [_conv_prev(state_gdn_conv[i]), state_gdn[i]], bs, lsmp, mix_state)
        ssd_prow = jnp.concatenate([_pad_lanes(ssd_a_log[i][None, :], LANE_SDT),
                                    _pad_lanes(ssd_dt_bias[i][None, :], LANE_SDT)], axis=0)
        ssd_par = [ssd_conv_w[i], ssd_conv_b[i][None, :], ssd_prow, ssd_prow.T,
                   jnp.repeat(ssd_d[i], SSD_P)[None, :], ssd_norm[i][None, :]]
        ssd_scr = [pltpu.VMEM((SSD_G, SSD_N, SSD_DI // SSD_G), F32), pltpu.VMEM((CHUNK, SSD_CONV_C), F32)]
        od_p, ssd_p = _mixer_seq(_ssd_kernel, "ssd_seq", zd, [(zs, "rows"), (zst_p, "zst")], ssd_par,
                                 bp, lp, ssd_state, ssd_scr)
        od_s, ssd_s = _mixer_batch(_ssd_kernel, "ssd_batch", zd, n_p, [(zs, "rows"), (zst_s, "zst")], ssd_par,
                                   [_conv_prev(state_ssd_conv[i]), state_ssd[i]], bs, lsmp, ssd_state)

        zc_p = zc[:n_p, :GDN_CONV_C].reshape(bp, lp, GDN_CONV_C)
        zc_s = zc[n_p:, :GDN_CONV_C].reshape(bs, lsmp, GDN_CONV_C)
        zd_p = zd[:n_p, SSD_DI:].reshape(bp, lp, SSD_CONV_C)
        zd_s = zd[n_p:, SSD_DI:].reshape(bs, lsmp, SSD_CONV_C)
        gdnc_p = zc_p[:, lp - (CONV_W - 1):].astype(F32)
        ssdc_p = zd_p[:, lp - (CONV_W - 1):].astype(F32)
        gdnc_s = jnp.concatenate([state_gdn_conv[i], zc_s.astype(F32)], axis=1)[:, lsmp:]
        ssdc_s = jnp.concatenate([state_ssd_conv[i], zd_s.astype(F32)], axis=1)[:, lsmp:]
        for lst, val in zip(outs_p, (gla_p, ret_p, gdn_p, gdnc_p, ssd_p, ssdc_p)):
            lst.append(val)
        for lst, val in zip(outs_s, (gla_s, ret_s, gdn_s, gdnc_s, ssd_s, ssdc_s)):
            lst.append(val)

        branches = [jnp.concatenate([a, b], axis=0)
                    for a, b in ((oa_p, oa_s), (ob_p, ob_s), (oc_p, oc_s), (od_p, od_s))]
        h = _merge(branches, zg, h, w_branch[i].astype(BF16), w_out[i].astype(BF16))

        j = i // 2
        if i % 2 == 0:
            h = _ffn(h, norm_ffn[i][None, :], ffn_w_gate[j][None].astype(BF16), ffn_w_up[j][None].astype(BF16),
                     ffn_w_down[j][None].astype(BF16))
        else:
            router = jnp.pad(moe_router[j], ((0, 0), (0, SMALL_W - N_EXPERTS)))
            h = _ffn(h, norm_ffn[i][None, :], moe_w_gate[j].astype(BF16), moe_w_up[j].astype(BF16),
                     moe_w_down[j].astype(BF16), router=router)
        p_all = jnp.concatenate([p_prompt[i].reshape(n_p, D_PLE), p_sample[i].reshape(n_s, D_PLE)], axis=0)
        h = _ple(h, p_all, norm_ple[i][None, :], ple_w_gate[i].astype(BF16), ple_w_proj[i].astype(BF16),
                 gain_final=norm_final[None, :] if i == depth - 1 else None)

    y_prompt = h[:n_p].reshape(bp, lp, D_MODEL)
    y_sample = h[n_p:].reshape(bs, lsmp, D_MODEL)
    return (y_prompt, y_sample) + tuple(jnp.stack(l) for l in outs_p) + tuple(jnp.stack(l) for l in outs_s)
```

```python
import functools
import math

import numpy as np
import jax
import jax.numpy as jnp
from jax import lax
from jax.experimental import pallas as pl
from jax.experimental.pallas import tpu as pltpu

F32 = jnp.float32
BF16 = jnp.bfloat16
EPS = 1e-6

D_MODEL = 1024
D_PLE = 256
CONV_W = 4
N_BRANCH = 4
BRANCH_W = 512
MIX_H = 4
MIX_DK = 64
MIX_DV = 128
GLA_RANK = 16
GLA_GATE_NORM = 16.0
ROPE_BASE = 10000.0
GDN_CONV_C = 2 * MIX_H * MIX_DK + MIX_H * MIX_DV
SSD_H = 8
SSD_P = 64
SSD_N = 64
SSD_G = 2
SSD_DI = SSD_H * SSD_P
SSD_CONV_C = SSD_DI + 2 * SSD_G * SSD_N
D_FF = 2816
N_EXPERTS = 8
SPLITS = (256, 256, 512, GLA_RANK, 512,
          256, 256, 512, 512,
          GDN_CONV_C, MIX_H, MIX_H, 512,
          SSD_DI, SSD_CONV_C, SSD_H,
          N_BRANCH * D_MODEL)

LANE_GLR = 0
LANE_BETA = 16
LANE_GDA = 20
LANE_SDT = 24
SMALL_W = 128

CHUNK = 64
SEQ_BLOCK_CHUNKS = 4
BATCH_SEQS = 8
ROW_TILE = 512
FF_BLOCKS = 2
VMEM_LIMIT = 56 * 1024 * 1024
PAST_LEN = 16384


def _row_tile(n):
    for tm in (ROW_TILE, 256, 128, 64, 32, 16):
        if n % tm == 0:
            return tm
    raise ValueError(f"token count {n} is not a multiple of 16")


def _mm(a, b):
    return jnp.dot(a, b, preferred_element_type=F32)


def _mm_nt(a, b):
    return lax.dot_general(a, b, (((1,), (1,)), ((), ())), preferred_element_type=F32)


def _mm_tn(a, b):
    return lax.dot_general(a, b, (((0,), (0,)), ((), ())), preferred_element_type=F32)


def _split(x):
    hi = x.astype(BF16)
    lo = (x - hi.astype(F32)).astype(BF16)
    return hi, lo


def _mm_xl(x, m):
    hi, lo = _split(x)
    return _mm(hi, m) + _mm(lo, m)


def _mm_xr(m, x):
    hi, lo = _split(x)
    return _mm(m, hi) + _mm(m, lo)


def _sigmoid(x):
    return 1.0 / (1.0 + jnp.exp(-x))


def _silu(x):
    return x * _sigmoid(x)


def _softplus(x):
    return jnp.maximum(x, 0.0) + jnp.log1p(jnp.exp(-jnp.abs(x)))


def _rms(x, gain):
    ms = jnp.mean(x * x, axis=-1, keepdims=True)
    return x * lax.rsqrt(ms + EPS) * gain


def _log2(n):
    k = int(round(math.log2(n)))
    assert (1 << k) == n, n
    return k


class _Masks:
    def __init__(self, c, ls):
        self.c, self.ls = c, ls
        sh = _log2(ls)
        r = lax.broadcasted_iota(jnp.int32, (c, c), 0)
        q = lax.broadcasted_iota(jnp.int32, (c, c), 1)
        same = (r >> sh) == (q >> sh)
        self.incl = jnp.logical_and(same, q <= r)
        self.strict = jnp.logical_and(same, q < r)
        self.tri = jnp.where(self.incl, 1.0, 0.0).astype(BF16)
        self.tri_t = jnp.where(jnp.logical_and(same, r <= q), 1.0, 0.0).astype(BF16)
        last = ((r >> sh) << sh) + (ls - 1)
        self.lastsel = jnp.where(q == last, 1.0, 0.0).astype(BF16)
        self.eye = jnp.where(r == q, 1.0, 0.0).astype(F32)
        self.r, self.q = r, q
        rc = lax.broadcasted_iota(jnp.int32, (c, 1), 0)
        self.seq_of_row = rc >> sh
        self.t_col = (rc & (ls - 1)).astype(F32)
        self.nseq = c // ls

    def rows_of(self, b, x):
        if self.nseq == 1:
            return x
        return jnp.where(self.seq_of_row == b, x, jnp.zeros_like(x))

    def shift_matrix(self, nprev):
        c, ls = self.c, self.ls
        t = self.r & (ls - 1)
        blocks = []
        for s in (1, 2, 3):
            cur = jnp.logical_and(self.q == self.r - s, t >= s)
            prev = jnp.logical_and(self.q == self.r + (ls - s), t < s)
            cur = jnp.where(cur, 1.0, 0.0).astype(BF16)
            prev = jnp.where(prev, 1.0, 0.0).astype(BF16)
            blocks.append(jnp.concatenate([prev] * nprev + [cur], axis=1))
        return jnp.concatenate(blocks, axis=0)


def _expander(lane0, group, width):
    r = lax.broadcasted_iota(jnp.int32, (SMALL_W, width), 0)
    q = lax.broadcasted_iota(jnp.int32, (SMALL_W, width), 1)
    return jnp.where(r == lane0 + (q >> _log2(group)), 1.0, 0.0).astype(BF16)


def _causal_conv(u, prev_parts, w, masks, shift_m):
    c = masks.c
    x = jnp.concatenate(list(prev_parts) + [u], axis=0)
    y = _mm(shift_m, x)
    out = u.astype(F32) * w[3:4, :]
    for s in (1, 2, 3):
        out = out + y[(s - 1) * c:s * c, :] * w[3 - s:4 - s, :]
    return out


def _tri_inverse(m, masks):
    p = masks.eye - m
    mp = m
    n = 2
    while n < masks.ls:
        mpb = mp.astype(BF16)
        mp = _mm(mpb, mpb)
        p = p + _mm(p.astype(BF16), mp.astype(BF16))
        n *= 2
    return p


def _pair_decay(gc_col, gt_row, incl):
    d = gc_col - gt_row
    return jnp.where(incl, jnp.exp(jnp.where(incl, d, 0.0)), 0.0)


class _States:
    def __init__(self, seq_mode, s_scr, s0_ref, s1_ref):
        self.seq_mode, self.s_scr, self.s0_ref, self.s1_ref = seq_mode, s_scr, s0_ref, s1_ref

    def get(self, b, h):
        if self.seq_mode:
            return self.s_scr[h]
        return self.s0_ref[b, h]

    def put(self, b, h, val):
        if self.seq_mode:
            self.s_scr[h] = val
        else:
            self.s1_ref[b, h] = val


def _seq_prologue(s_scr, prev_scr=None):
    @pl.when(pl.program_id(1) == 0)
    def _():
        s_scr[...] = jnp.zeros_like(s_scr)
        if prev_scr is not None:
            prev_scr[...] = jnp.zeros_like(prev_scr)


def _seq_epilogue(s_scr, s1_ref):
    @pl.when(pl.program_id(1) == pl.num_programs(1) - 1)
    def _():
        s1_ref[0] = s_scr[...]


def _in_proj_kernel(h_ref, g_ref, wa, wb, wc, wd, wg, ws, wst, za, zb, zc, zd, zg, zs, zst):
    xn = _rms(h_ref[...], g_ref[...]).astype(BF16)
    for w_ref, o_ref in ((wa, za), (wb, zb), (wc, zc), (wd, zd), (wg, zg)):
        width = o_ref.shape[1]
        for j in range(0, width, 512):
            jw = min(512, width - j)
            o_ref[:, j:j + jw] = _mm(xn, w_ref[:, j:j + jw]).astype(o_ref.dtype)
    zs[...] = _mm(xn, ws[...])
    zst[...] = _mm_nt(wst[...], xn)


def _in_proj(h, gain, wts):
    n = h.shape[0]
    tm = _row_tile(n)
    wa, wb, wc, wd, wg, ws, wst = wts
    const = lambda i: (0, 0)
    row = lambda i: (i, 0)

    def wspec(w):
        return pl.BlockSpec(w.shape, const, pipeline_mode=pl.Buffered(1))

    outs = [jax.ShapeDtypeStruct((n, w.shape[1]), BF16) for w in (wa, wb, wc, wd, wg)]
    outs += [jax.ShapeDtypeStruct((n, SMALL_W), F32), jax.ShapeDtypeStruct((SMALL_W, n), F32)]
    out_specs = [pl.BlockSpec((tm, w.shape[1]), row) for w in (wa, wb, wc, wd, wg)]
    out_specs += [pl.BlockSpec((tm, SMALL_W), row), pl.BlockSpec((SMALL_W, tm), lambda i: (0, i))]
    return pl.pallas_call(
        _in_proj_kernel,
        grid=(n // tm,),
        in_specs=[pl.BlockSpec((tm, D_MODEL), row), pl.BlockSpec((1, D_MODEL), const)]
        + [wspec(w) for w in wts],
        out_specs=out_specs,
        out_shape=outs,
        compiler_params=pltpu.CompilerParams(dimension_semantics=("parallel",), vmem_limit_bytes=VMEM_LIMIT),
        name="in_proj",
    )(h, gain, *wts)


def _gla_kernel(*refs, c, cpb, ls, seq_mode):
    if seq_mode:
        z_ref, zs_ref, wgk_ref, bgk_ref, gain_ref, o_ref, s1_ref, s_scr = refs
        s0_ref = None
        _seq_prologue(s_scr)
    else:
        z_ref, zs_ref, wgk_ref, bgk_ref, gain_ref, s0_ref, o_ref, s1_ref = refs
        s_scr = None
    st = _States(seq_mode, s_scr, s0_ref, s1_ref)
    mk = _Masks(c, ls)
    r64 = lax.broadcasted_iota(jnp.int32, (MIX_DK, MIX_DK), 0)
    q64 = lax.broadcasted_iota(jnp.int32, (MIX_DK, MIX_DK), 1)
    eye64 = r64 == q64
    ones_dv = jnp.ones((MIX_DK, MIX_DV), BF16)
    for ci in range(cpb):
        rows = slice(ci * c, (ci + 1) * c)
        z = z_ref[rows, :]
        q = z[:, 0:256].astype(F32) * (MIX_DK ** -0.5)
        k = z[:, 256:512].astype(F32)
        v = z[:, 512:1024]
        gg = z[:, 1024:1536].astype(F32)
        pre = _mm(zs_ref[rows, :].astype(BF16), wgk_ref[...]) + bgk_ref[...]
        lg = -_softplus(-pre) * (1.0 / GLA_GATE_NORM)
        g = _mm_xr(mk.tri, lg)
        eg = jnp.exp(g)
        qe = (q * eg).astype(BF16)
        ke = (k * jnp.exp(-g)).astype(BF16)
        glast = _mm_xr(mk.lastsel, g)
        kd = (k * jnp.exp(glast - g)).astype(BF16)
        el = jnp.exp(glast)
        for h in range(MIX_H):
            sl = slice(MIX_DK * h, MIX_DK * (h + 1))
            vl = slice(MIX_DV * h, MIX_DV * (h + 1))
            a = jnp.where(mk.incl, _mm_nt(qe[:, sl], ke[:, sl]), 0.0).astype(BF16)
            o = _mm(a, v[:, vl])
            for b in range(mk.nseq):
                s = st.get(b, h)
                o = o + _mm(mk.rows_of(b, qe[:, sl]), s.astype(BF16))
                el_row = el[b * ls:b * ls + 1, sl]
                diag = jnp.where(eye64, jnp.broadcast_to(el_row, (MIX_DK, MIX_DK)), 0.0)
                el_col = _mm_xl(diag, ones_dv)
                st.put(b, h, s * el_col + _mm_tn(mk.rows_of(b, kd[:, sl]), v[:, vl]))
            y = _rms(o, gain_ref[...]) * _silu(gg[:, vl])
            o_ref[rows, vl] = y.astype(o_ref.dtype)
    if seq_mode:
        _seq_epilogue(s_scr, s1_ref)


def _ret_kernel(*refs, c, cpb, ls, seq_mode):
    if seq_mode:
        z_ref, cos_ref, sin_ref, gain_ref, o_ref, s1_ref, s_scr = refs
        s0_ref = None
        _seq_prologue(s_scr)
    else:
        z_ref, cos_ref, sin_ref, gain_ref, s0_ref, o_ref, s1_ref = refs
        s_scr = None
    st = _States(seq_mode, s_scr, s0_ref, s1_ref)
    mk = _Masks(c, ls)
    lgam = [math.log(1.0 - 2.0 ** (-5.0 - h)) for h in range(MIX_H)]
    lane = lax.broadcasted_iota(jnp.int32, (1, MIX_H * MIX_DK), 1)
    lg_row = jnp.zeros((1, MIX_H * MIX_DK), F32)
    for h in range(MIX_H):
        lg_row = jnp.where((lane >> 6) == h, lgam[h], lg_row)
    first_half = (lane & 63) < 32
    eg = jnp.exp((mk.t_col + 1.0) * lg_row)
    ed = jnp.exp((ls - 1.0 - mk.t_col) * lg_row)
    dt_pos = ((mk.r & (ls - 1)) - (mk.q & (ls - 1))).astype(F32)
    dm = [jnp.where(mk.incl, jnp.exp(jnp.where(mk.incl, dt_pos * lgam[h], 0.0)), 0.0) for h in range(MIX_H)]

    def rope(x, cs, sn):
        sw = jnp.where(first_half, pltpu.roll(x, 256 - 32, 1), pltpu.roll(x, 32, 1))
        return x * cs + sw * sn

    for ci in range(cpb):
        rows = slice(ci * c, (ci + 1) * c)
        z = z_ref[rows, :]
        cs = cos_ref[rows, :]
        sn = sin_ref[rows, :]
        qr = rope(z[:, 0:256].astype(F32), cs, sn)
        kr = rope(z[:, 256:512].astype(F32), cs, sn) * (MIX_DK ** -0.5)
        v = z[:, 512:1024]
        rg = z[:, 1024:1536].astype(F32)
        qb = qr.astype(BF16)
        kb = kr.astype(BF16)
        qe = (qr * eg).astype(BF16)
        kd = (kr * ed).astype(BF16)
        for h in range(MIX_H):
            sl = slice(MIX_DK * h, MIX_DK * (h + 1))
            vl = slice(MIX_DV * h, MIX_DV * (h + 1))
            a = (_mm_nt(qb[:, sl], kb[:, sl]) * dm[h]).astype(BF16)
            o = _mm(a, v[:, vl])
            for b in range(mk.nseq):
                s = st.get(b, h)
                o = o + _mm(mk.rows_of(b, qe[:, sl]), s.astype(BF16))
                st.put(b, h, s * math.exp(lgam[h] * ls) + _mm_tn(mk.rows_of(b, kd[:, sl]), v[:, vl]))
            y = _rms(o, gain_ref[...]) * _silu(rg[:, vl])
            o_ref[rows, vl] = y.astype(o_ref.dtype)
    if seq_mode:
        _seq_epilogue(s_scr, s1_ref)


def _gdn_kernel(*refs, c, cpb, ls, seq_mode):
    if seq_mode:
        z_ref, zs_ref, cw_ref, prow_ref, gain_ref, o_ref, s1_ref, s_scr, prev_scr = refs
        s0_ref = prev_ref = None
        _seq_prologue(s_scr, prev_scr)
    else:
        z_ref, zs_ref, cw_ref, prow_ref, gain_ref, prev_ref, s0_ref, o_ref, s1_ref = refs
        s_scr = prev_scr = None
    hs = MIX_H * c
    hdk = MIX_H * MIX_DK
    mk = _Masks(c, ls)
    mks = _Masks(hs, ls)
    shift_m = mk.shift_matrix(1 if seq_mode else 2)
    r = lax.broadcasted_iota(jnp.int32, (hdk, hdk), 0)
    q = lax.broadcasted_iota(jnp.int32, (hdk, hdk), 1)
    ones_bd = jnp.where((r >> 6) == (q >> 6), 1.0, 0.0).astype(BF16)
    e_beta_k = _expander(LANE_BETA, MIX_DK, hdk)
    e_beta_v = _expander(LANE_BETA, MIX_DV, MIX_H * MIX_DV)
    e_g_k = _expander(LANE_GDA, MIX_DK, hdk)
    e_g_v = _expander(LANE_GDA, MIX_DV, MIX_H * MIX_DV)
    a_row = -jnp.exp(prow_ref[0:1, :])
    b_row = prow_ref[1:2, :]
    cw = cw_ref[...]
    head_of_lane = lax.broadcasted_iota(jnp.int32, (1, hdk), 1) >> 6
    lane0 = jnp.where(lax.broadcasted_iota(jnp.int32, (hs, SMALL_W), 1) == 0, 1.0, 0.0).astype(BF16)
    seq_of_srow = (lax.broadcasted_iota(jnp.int32, (hs, 1), 0) & (c - 1)) >> _log2(ls)

    def stack_k(x):
        return jnp.concatenate([jnp.where(head_of_lane == h, x, jnp.zeros_like(x)) for h in range(MIX_H)], axis=0)

    def stack_v(x):
        return jnp.concatenate([x[:, MIX_DV * h:MIX_DV * (h + 1)] for h in range(MIX_H)], axis=0)

    def srows_of(b, x):
        if mk.nseq == 1:
            return x
        return jnp.where(seq_of_srow == b, x, jnp.zeros_like(x))

    chunks = []
    for ci in range(cpb):
        rows = slice(ci * c, (ci + 1) * c)
        z = z_ref[rows, :]
        u = z[:, 0:GDN_CONV_C]
        if seq_mode:
            prev_parts = [prev_scr[...].astype(BF16)]
        else:
            prev_parts = list(_split(prev_ref[...]))
        cqkv = _silu(_causal_conv(u, prev_parts, cw, mk, shift_m))
        if seq_mode:
            prev_scr[...] = u.astype(F32)
        cq = cqkv[:, 0:256]
        ck = cqkv[:, 256:512]
        cv = cqkv[:, 512:1024]
        qn = cq * lax.rsqrt(_mm_xl(cq * cq, ones_bd) + EPS) * (MIX_DK ** -0.5)
        kn = ck * lax.rsqrt(_mm_xl(ck * ck, ones_bd) + EPS)
        zs = zs_ref[rows, :]
        beta = _sigmoid(zs)
        lgd = a_row * _softplus(zs + b_row)
        gc = _mm_xr(mk.tri, lgd)
        glast = _mm_xr(mk.lastsel, gc)
        beta_k = _mm_xl(beta, e_beta_k)
        beta_v = _mm_xl(beta, e_beta_v)
        eg_k = _mm_xl(jnp.exp(gc), e_g_k)
        dl_k = _mm_xl(jnp.exp(glast - gc), e_g_k)
        el_v = _mm_xl(jnp.exp(glast), e_g_v)
        kbeta = kn * beta_k
        k_st = stack_k(kn.astype(BF16))
        g_col = jnp.concatenate([gc[:, LANE_GDA + h:LANE_GDA + h + 1] for h in range(MIX_H)], axis=0)
        g_hi, g_lo = _split(jnp.broadcast_to(g_col, (hs, SMALL_W)))
        g_row = _mm_nt(lane0, g_hi) + _mm_nt(lane0, g_lo)
        dm = _pair_decay(g_col, g_row, mks.incl)
        m = _mm_nt(stack_k(kbeta.astype(BF16)), k_st) * jnp.where(mks.strict, dm, 0.0)
        chunks.append(dict(
            rows=rows, m=m, p=mks.eye - m, mp=m,
            a=(_mm_nt(stack_k(qn.astype(BF16)), k_st) * dm).astype(BF16),
            vbeta=stack_v((cv * beta_v).astype(BF16)),
            kbe=stack_k((kbeta * eg_k).astype(BF16)),
            qe=stack_k((qn * eg_k).astype(BF16)),
            kd=stack_k((kn * dl_k).astype(BF16)),
            el_v=el_v))
    n = 2
    while n < ls:
        for ch in chunks:
            mpb = ch["mp"].astype(BF16)
            ch["mp"] = _mm(mpb, mpb)
        for ch in chunks:
            ch["p"] = ch["p"] + _mm(ch["p"].astype(BF16), ch["mp"].astype(BF16))
        n *= 2
    for ch in chunks:
        tinv = ch["p"].astype(BF16)
        ch["uu"] = _mm(tinv, ch["vbeta"])
        ch["ww"] = _mm(tinv, ch["kbe"]).astype(BF16)
    for ch in chunks:
        rows = ch["rows"]
        states = []
        vn = ch["uu"]
        for b in range(mk.nseq):
            s = s_scr[...] if seq_mode else s0_ref[b].reshape(hdk, MIX_DV)
            states.append(s)
            vn = vn - _mm(srows_of(b, ch["ww"]), s.astype(BF16))
        vnb = vn.astype(BF16)
        o = _mm(ch["a"], vnb)
        for b in range(mk.nseq):
            s = states[b]
            o = o + _mm(srows_of(b, ch["qe"]), s.astype(BF16))
            dec = jnp.concatenate(
                [jnp.broadcast_to(ch["el_v"][b * ls:b * ls + 1, MIX_DV * h:MIX_DV * (h + 1)], (MIX_DK, MIX_DV))
                 for h in range(MIX_H)], axis=0)
            s_new = s * dec + _mm_tn(srows_of(b, ch["kd"]), vnb)
            if seq_mode:
                s_scr[...] = s_new
            else:
                s1_ref[b] = s_new.reshape(MIX_H, MIX_DK, MIX_DV)
        dg = z_ref[rows, GDN_CONV_C:GDN_CONV_C + 512].astype(F32)
        for h in range(MIX_H):
            vl = slice(MIX_DV * h, MIX_DV * (h + 1))
            y = _rms(o[h * c:(h + 1) * c, :], gain_ref[...]) * _silu(dg[:, vl])
            o_ref[rows, vl] = y.astype(o_ref.dtype)
    if seq_mode:
        @pl.when(pl.program_id(1) == pl.num_programs(1) - 1)
        def _():
            s1_ref[0] = s_scr[...].reshape(MIX_H, MIX_DK, MIX_DV)


def _ssd_kernel(*refs, c, cpb, ls, seq_mode):
    gw = SSD_DI // SSD_G
    hpg = SSD_H // SSD_G
    if seq_mode:
        (z_ref, zs_ref, zst_ref, cw_ref, cb_ref, prow_ref, pcol_ref, dx_ref, gain_ref,
         o_ref, s1_ref, s_scr, prev_scr) = refs
        s0_ref = prev_ref = None
        _seq_prologue(s_scr, prev_scr)
    else:
        (z_ref, zs_ref, zst_ref, cw_ref, cb_ref, prow_ref, pcol_ref, dx_ref, gain_ref, prev_ref, s0_ref,
         o_ref, s1_ref) = refs
        s_scr = prev_scr = None
    mk = _Masks(c, ls)
    shift_m = mk.shift_matrix(1 if seq_mode else 2)
    e_x = _expander(LANE_SDT, SSD_P, SSD_DI)
    a_row = -jnp.exp(prow_ref[0:1, :])
    b_row = prow_ref[1:2, :]
    a_col = -jnp.exp(pcol_ref[:, 0:1])
    b_col = pcol_ref[:, 1:2]
    cw = cw_ref[...]
    lane_g = lax.broadcasted_iota(jnp.int32, (1, gw), 1) >> 6

    def get_state(b, g):
        if seq_mode:
            return s_scr[g]
        return jnp.concatenate([s0_ref[b, g * hpg + j] for j in range(hpg)], axis=-1)

    def put_state(b, g, val):
        if seq_mode:
            s_scr[g] = val
        else:
            for j in range(hpg):
                s1_ref[b, g * hpg + j] = val[:, SSD_P * j:SSD_P * (j + 1)]

    for ci in range(cpb):
        rows = slice(ci * c, (ci + 1) * c)
        z = z_ref[rows, :]
        sz = z[:, 0:SSD_DI].astype(F32)
        u = z[:, SSD_DI:SSD_DI + SSD_CONV_C]
        if seq_mode:
            prev_parts = [prev_scr[...].astype(BF16)]
        else:
            prev_parts = list(_split(prev_ref[...]))
        xbc = _silu(_causal_conv(u, prev_parts, cw, mk, shift_m) + cb_ref[...])
        if seq_mode:
            prev_scr[...] = u.astype(F32)
        sx = xbc[:, 0:SSD_DI]
        s_b = xbc[:, SSD_DI:SSD_DI + SSD_G * SSD_N]
        s_c = xbc[:, SSD_DI + SSD_G * SSD_N:SSD_CONV_C]
        zs = zs_ref[rows, :]
        zst = zst_ref[0, :, ci * c:(ci + 1) * c]
        dt = _softplus(zs + b_row)
        lsd = dt * a_row
        lsd_t = _softplus(zst + b_col) * a_col
        gc = _mm_xr(mk.tri, lsd)
        gt = _mm_xl(lsd_t, mk.tri_t)
        glast = _mm_xr(mk.lastsel, gc)
        dt_x = _mm_xl(dt, e_x)
        eg_x = _mm_xl(jnp.exp(gc), e_x)
        dl_x = _mm_xl(jnp.exp(glast - gc), e_x)
        el_x = _mm_xl(jnp.exp(glast), e_x)
        v = sx * dt_x
        vd = (v * dl_x).astype(BF16)
        for g in range(SSD_G):
            gl = slice(gw * g, gw * (g + 1))
            cg = s_c[:, SSD_N * g:SSD_N * (g + 1)].astype(BF16)
            bg = s_b[:, SSD_N * g:SSD_N * (g + 1)].astype(BF16)
            ag = _mm_nt(cg, bg)
            vg = v[:, gl]
            o = jnp.zeros((c, gw), F32)
            for j in range(hpg):
                lane = LANE_SDT + g * hpg + j
                dm = _pair_decay(gc[:, lane:lane + 1], gt[lane:lane + 1, :], mk.incl)
                vj = jnp.where(lane_g == j, vg, 0.0).astype(BF16)
                o = o + _mm((ag * dm).astype(BF16), vj)
            o_state = jnp.zeros((c, gw), F32)
            for b in range(mk.nseq):
                s = get_state(b, g)
                o_state = o_state + _mm(mk.rows_of(b, cg), s.astype(BF16))
                dec = el_x[b * ls:b * ls + 1, gl]
                put_state(b, g, s * dec + _mm_tn(mk.rows_of(b, bg), vd[:, gl]))
            y = o + o_state * eg_x[:, gl] + sx[:, gl] * dx_ref[:, gl]
            y = y * _silu(sz[:, gl])
            o_ref[rows, gl] = _rms(y, gain_ref[:, gl]).astype(o_ref.dtype)
    if seq_mode:
        @pl.when(pl.program_id(1) == pl.num_programs(1) - 1)
        def _():
            for g in range(SSD_G):
                for j in range(hpg):
                    s1_ref[0, g * hpg + j] = s_scr[g][:, SSD_P * j:SSD_P * (j + 1)]


def _full_spec(a, grid_rank):
    nd = a.ndim
    if grid_rank == 1:
        return pl.BlockSpec(a.shape, lambda i: (0,) * nd)
    return pl.BlockSpec(a.shape, lambda i, j: (0,) * nd)


def _mixer_seq(kernel_fn, name, z, extra_rows, params, nbatch, seqlen, state_shape, scratch):
    c, cpb = CHUNK, SEQ_BLOCK_CHUNKS
    rblk = c * cpb
    nblk = seqlen // rblk
    rowmap = lambda b, j: (b * nblk + j, 0)
    in_specs = [pl.BlockSpec((rblk, z.shape[1]), rowmap)]
    args = [z]
    for a, kind in extra_rows:
        if kind == "rows":
            in_specs.append(pl.BlockSpec((rblk, a.shape[1]), rowmap))
        elif kind == "zst":
            in_specs.append(pl.BlockSpec((1, a.shape[1], rblk), lambda b, j: (b * nblk + j, 0, 0)))
        else:
            in_specs.append(pl.BlockSpec((rblk, a.shape[1]), lambda b, j: (j, 0)))
        args.append(a)
    for a in params:
        in_specs.append(_full_spec(a, 2))
        args.append(a)
    sblk = (1,) + state_shape
    out_shape = [jax.ShapeDtypeStruct((nbatch * seqlen, BRANCH_W), BF16),
                 jax.ShapeDtypeStruct((nbatch,) + state_shape, F32)]
    out_specs = [pl.BlockSpec((rblk, BRANCH_W), rowmap),
                 pl.BlockSpec(sblk, lambda b, j: (b,) + (0,) * len(state_shape))]
    return pl.pallas_call(
        functools.partial(kernel_fn, c=c, cpb=cpb, ls=c, seq_mode=True),
        grid=(nbatch, nblk),
        in_specs=in_specs,
        out_specs=out_specs,
        out_shape=out_shape,
        scratch_shapes=scratch,
        compiler_params=pltpu.CompilerParams(dimension_semantics=("parallel", "arbitrary"),
                                             vmem_limit_bytes=VMEM_LIMIT),
        name=name,
    )(*args)


def _mixer_batch(kernel_fn, name, z, row_off, extra_rows, params, per_seq, nbatch, seqlen, state_shape):
    sb = BATCH_SEQS
    c = sb * seqlen
    nsteps = nbatch // sb
    off = row_off // c
    rowmap = lambda i: (off + i, 0)
    in_specs = [pl.BlockSpec((c, z.shape[1]), rowmap)]
    args = [z]
    for a, kind in extra_rows:
        if kind == "rows":
            in_specs.append(pl.BlockSpec((c, a.shape[1]), rowmap))
        elif kind == "zst":
            in_specs.append(pl.BlockSpec((1, a.shape[1], c), lambda i: (i, 0, 0)))
        else:
            in_specs.append(pl.BlockSpec((c, a.shape[1]), lambda i: (0, 0)))
        args.append(a)
    for a in params:
        in_specs.append(_full_spec(a, 1))
        args.append(a)
    for a in per_seq:
        if a.ndim == 2:
            in_specs.append(pl.BlockSpec((c, a.shape[1]), lambda i: (i, 0)))
        else:
            in_specs.append(pl.BlockSpec((sb,) + a.shape[1:], lambda i: (i,) + (0,) * (a.ndim - 1)))
        args.append(a)
    out_shape = [jax.ShapeDtypeStruct((nbatch * seqlen, BRANCH_W), BF16),
                 jax.ShapeDtypeStruct((nbatch,) + state_shape, F32)]
    out_specs = [pl.BlockSpec((c, BRANCH_W), lambda i: (i, 0)),
                 pl.BlockSpec((sb,) + state_shape, lambda i: (i,) + (0,) * len(state_shape))]
    return pl.pallas_call(
        functools.partial(kernel_fn, c=c, cpb=1, ls=seqlen, seq_mode=False),
        grid=(nsteps,),
        in_specs=in_specs,
        out_specs=out_specs,
        out_shape=out_shape,
        compiler_params=pltpu.CompilerParams(dimension_semantics=("parallel",), vmem_limit_bytes=VMEM_LIMIT),
        name=name,
    )(*args)


def _merge_kernel(oa, ob, oc, od, zg, h_ref, wbr, wout, out_ref):
    acc = jnp.zeros(out_ref.shape, F32)
    for n, o_ref in enumerate((oa, ob, oc, od)):
        gate = _sigmoid(zg[:, n * D_MODEL:(n + 1) * D_MODEL].astype(F32))
        acc = acc + gate * _mm(o_ref[...], wbr[n])
    out_ref[...] = h_ref[...] + _mm(acc.astype(BF16), wout[...])


def _merge(branches, zg, h, wbr, wout):
    n = h.shape[0]
    tm = _row_tile(n)
    row = lambda i: (i, 0)
    in_specs = [pl.BlockSpec((tm, BRANCH_W), row) for _ in range(4)]
    in_specs += [pl.BlockSpec((tm, N_BRANCH * D_MODEL), row), pl.BlockSpec((tm, D_MODEL), row),
                 pl.BlockSpec(wbr.shape, lambda i: (0, 0, 0)), pl.BlockSpec(wout.shape, lambda i: (0, 0))]
    return pl.pallas_call(
        _merge_kernel,
        grid=(n // tm,),
        in_specs=in_specs,
        out_specs=pl.BlockSpec((tm, D_MODEL), row),
        out_shape=jax.ShapeDtypeStruct((n, D_MODEL), F32),
        compiler_params=pltpu.CompilerParams(dimension_semantics=("parallel",), vmem_limit_bytes=VMEM_LIMIT),
        name="merge",
    )(*branches, zg, h, wbr, wout)


def _ffn_kernel(h_ref, g_ref, wg_ref, wu_ref, wd_ref, out_ref, u_scr, acc_scr):
    f = pl.program_id(1)

    @pl.when(f == 0)
    def _():
        u_scr[...] = _rms(h_ref[...], g_ref[...]).astype(BF16)
        acc_scr[...] = jnp.zeros_like(acc_scr)

    u = u_scr[...]
    a = (_silu(_mm(u, wg_ref[...])) * _mm(u, wu_ref[...])).astype(BF16)
    acc_scr[...] += _mm(a, wd_ref[...])

    @pl.when(f == pl.num_programs(1) - 1)
    def _():
        out_ref[...] = h_ref[...] + acc_scr[...]


def _ffn(h, gain, wg, wu, wd):
    n = h.shape[0]
    tm = _row_tile(n)
    tf = D_FF // FF_BLOCKS
    row = lambda i, f: (i, 0)
    in_specs = [pl.BlockSpec((tm, D_MODEL), row), pl.BlockSpec((1, D_MODEL), lambda i, f: (0, 0)),
                pl.BlockSpec((D_MODEL, tf), lambda i, f: (0, f)),
                pl.BlockSpec((D_MODEL, tf), lambda i, f: (0, f)),
                pl.BlockSpec((tf, D_MODEL), lambda i, f: (f, 0))]
    return pl.pallas_call(
        _ffn_kernel,
        grid=(n // tm, FF_BLOCKS),
        in_specs=in_specs,
        out_specs=pl.BlockSpec((tm, D_MODEL), row),
        out_shape=jax.ShapeDtypeStruct((n, D_MODEL), F32),
        scratch_shapes=[pltpu.VMEM((tm, D_MODEL), BF16), pltpu.VMEM((tm, D_MODEL), F32)],
        compiler_params=pltpu.CompilerParams(dimension_semantics=("parallel", "arbitrary"),
                                             vmem_limit_bytes=VMEM_LIMIT),
        name="dense_ffn",
    )(h, gain, wg, wu, wd)


MOE_TOKENS = 768
MOE_CAP = 256


def _moe_kernel(h_ref, g_ref, rt_ref, wg_ref, wu_ref, wd_ref, out_ref,
                u_scr, acc_scr, w_scr, sel_scr, rank_scr, selt_scr, rankt_scr, xc_scr, yc_scr, cnt_scr):
    tt = h_ref.shape[0]
    cap = MOE_CAP
    e = pl.program_id(1)
    f = pl.program_id(2)
    nf = pl.num_programs(2)

    @pl.when(jnp.logical_and(e == 0, f == 0))
    def _():
        u = _rms(h_ref[...], g_ref[...])
        u_scr[...] = u.astype(BF16)
        acc_scr[...] = jnp.zeros_like(acc_scr)
        uh, ul = _split(u)
        rh, rl = _split(rt_ref[...])
        logits = _mm(uh, rh) + (_mm(uh, rl) + _mm(ul, rh))
        lane = lax.broadcasted_iota(jnp.int32, logits.shape, 1).astype(F32)
        neg = -3.0e38
        lg = jnp.where(lane < N_EXPERTS, logits, neg)
        m1 = jnp.max(lg, axis=-1, keepdims=True)
        i1 = jnp.min(jnp.where(lg == m1, lane, float(SMALL_W)), axis=-1, keepdims=True)
        lg2 = jnp.where(lane == i1, neg, lg)
        m2 = jnp.max(lg2, axis=-1, keepdims=True)
        i2 = jnp.min(jnp.where(lg2 == m2, lane, float(SMALL_W)), axis=-1, keepdims=True)
        e2 = jnp.exp(m2 - m1)
        w_scr[...] = jnp.where(lane == i1, 1.0 / (1.0 + e2), 0.0) + jnp.where(lane == i2, e2 / (1.0 + e2), 0.0)
        sel = jnp.where(jnp.logical_or(lane == i1, lane == i2), 1.0, 0.0)
        sel_scr[...] = sel
        selb = sel.astype(BF16)
        r = lax.broadcasted_iota(jnp.int32, (tt, tt), 0)
        q = lax.broadcasted_iota(jnp.int32, (tt, tt), 1)
        before = jnp.where(r < q, 1.0, 0.0).astype(BF16)
        ident = jnp.where(r == q, 1.0, 0.0).astype(BF16)
        rank_scr[...] = _mm_tn(before, selb)
        rankt_scr[...] = _mm_tn(selb, before)
        selt_scr[...] = _mm_tn(selb, ident)
        cnt = jnp.sum(sel, axis=0, keepdims=True)
        for ee in range(N_EXPERTS):
            cnt_scr[ee] = jnp.sum(jnp.where(lane[0:1, :] == float(ee), cnt, 0.0)).astype(jnp.int32)

    nsub = (cnt_scr[e] + (cap - 1)) // cap
    lane_w = lax.broadcasted_iota(jnp.int32, (tt, SMALL_W), 1)

    def col_of(ref):
        return jnp.sum(jnp.where(lane_w == e, ref[...], 0.0), axis=-1, keepdims=True)

    @pl.when(f == 0)
    def _():
        rank_row = rankt_scr[pl.ds(e, 1), :]
        sel_row = selt_scr[pl.ds(e, 1), :]
        jcol = lax.broadcasted_iota(jnp.int32, (cap, 1), 0).astype(F32)

        def gather(s, carry):
            base = (s * cap).astype(F32)
            hit = jnp.logical_and(rank_row == jcol + base, sel_row > 0.5)
            onehot = jnp.where(hit, 1.0, 0.0).astype(BF16)
            off = pl.multiple_of(s * cap, cap)
            xc_scr[pl.ds(off, cap), :] = _mm(onehot, u_scr[...]).astype(BF16)
            return carry

        lax.fori_loop(0, nsub, gather, 0)

    def expert(s, carry):
        off = pl.multiple_of(s * cap, cap)
        x = xc_scr[pl.ds(off, cap), :]
        a = (_silu(_mm(x, wg_ref[0])) * _mm(x, wu_ref[0])).astype(BF16)
        y = _mm(a, wd_ref[0])

        @pl.when(f == 0)
        def _():
            yc_scr[pl.ds(off, cap), :] = y

        @pl.when(f != 0)
        def _():
            yc_scr[pl.ds(off, cap), :] += y

        return carry

    lax.fori_loop(0, nsub, expert, 0)

    @pl.when(f == nf - 1)
    def _():
        rank_col = col_of(rank_scr)
        sel_col = col_of(sel_scr)
        w_col = col_of(w_scr)
        jrow = lax.broadcasted_iota(jnp.int32, (1, cap), 1).astype(F32)

        def scatter(s, carry):
            base = (s * cap).astype(F32)
            hit = jnp.logical_and(rank_col == jrow + base, sel_col > 0.5)
            onehot = jnp.where(hit, 1.0, 0.0).astype(BF16)
            off = pl.multiple_of(s * cap, cap)
            acc_scr[...] += w_col * _mm(onehot, yc_scr[pl.ds(off, cap), :].astype(BF16))
            return carry

        lax.fori_loop(0, nsub, scatter, 0)

    @pl.when(jnp.logical_and(e == pl.num_programs(1) - 1, f == nf - 1))
    def _():
        out_ref[...] = h_ref[...] + acc_scr[...]


def _moe(h, gain, router, wg, wu, wd):
    n = h.shape[0]
    tt = MOE_TOKENS if n % MOE_TOKENS == 0 else _row_tile(n)
    assert tt % MOE_CAP == 0 or tt < MOE_CAP, (tt, MOE_CAP)
    ne = wg.shape[0]
    tf = D_FF // FF_BLOCKS
    row = lambda i, e, f: (i, 0)
    nrows = max(tt, MOE_CAP)
    in_specs = [pl.BlockSpec((tt, D_MODEL), row), pl.BlockSpec((1, D_MODEL), lambda i, e, f: (0, 0)),
                pl.BlockSpec(router.shape, lambda i, e, f: (0, 0)),
                pl.BlockSpec((1, D_MODEL, tf), lambda i, e, f: (e, 0, f)),
                pl.BlockSpec((1, D_MODEL, tf), lambda i, e, f: (e, 0, f)),
                pl.BlockSpec((1, tf, D_MODEL), lambda i, e, f: (e, f, 0))]
    scratch = [pltpu.VMEM((tt, D_MODEL), BF16), pltpu.VMEM((tt, D_MODEL), F32),
               pltpu.VMEM((tt, SMALL_W), F32), pltpu.VMEM((tt, SMALL_W), F32), pltpu.VMEM((tt, SMALL_W), F32),
               pltpu.VMEM((SMALL_W, tt), F32), pltpu.VMEM((SMALL_W, tt), F32),
               pltpu.VMEM((nrows, D_MODEL), BF16), pltpu.VMEM((nrows, D_MODEL), F32),
               pltpu.SMEM((N_EXPERTS,), jnp.int32)]
    return pl.pallas_call(
        _moe_kernel,
        grid=(n // tt, ne, FF_BLOCKS),
        in_specs=in_specs,
        out_specs=pl.BlockSpec((tt, D_MODEL), row),
        out_shape=jax.ShapeDtypeStruct((n, D_MODEL), F32),
        scratch_shapes=scratch,
        compiler_params=pltpu.CompilerParams(dimension_semantics=("parallel", "arbitrary", "arbitrary"),
                                             vmem_limit_bytes=VMEM_LIMIT),
        name="moe",
    )(h, gain, router, wg, wu, wd)


def _ple_kernel(*refs, final):
    if final:
        h_ref, p_ref, g_ref, wgate, wproj, gf_ref, out_ref = refs
    else:
        h_ref, p_ref, g_ref, wgate, wproj, out_ref = refs
    h = h_ref[...]
    gate = _sigmoid(_mm(_rms(h, g_ref[...]).astype(BF16), wgate[...]))
    h = h + gate * _mm(p_ref[...].astype(BF16), wproj[...])
    if final:
        h = _rms(h, gf_ref[...])
    out_ref[...] = h


def _ple(h, p, gain, wgate, wproj, gain_final=None):
    n = h.shape[0]
    tm = _row_tile(n)
    final = gain_final is not None
    row = lambda i: (i, 0)
    const = lambda i: (0, 0)
    in_specs = [pl.BlockSpec((tm, D_MODEL), row), pl.BlockSpec((tm, D_PLE), row),
                pl.BlockSpec((1, D_MODEL), const), pl.BlockSpec(wgate.shape, const),
                pl.BlockSpec(wproj.shape, const)]
    args = [h, p, gain, wgate, wproj]
    if final:
        in_specs.append(pl.BlockSpec((1, D_MODEL), const))
        args.append(gain_final)
    return pl.pallas_call(
        functools.partial(_ple_kernel, final=final),
        grid=(n // tm,),
        in_specs=in_specs,
        out_specs=pl.BlockSpec((tm, D_MODEL), row),
        out_shape=jax.ShapeDtypeStruct((n, D_MODEL), F32),
        compiler_params=pltpu.CompilerParams(dimension_semantics=("parallel",), vmem_limit_bytes=VMEM_LIMIT),
        name="ple",
    )(*args)


def _pad_lanes(x, lane0):
    w = x.shape[-1]
    pad = [(0, 0)] * (x.ndim - 1) + [(lane0, SMALL_W - lane0 - w)]
    return jnp.pad(x, pad)


def _in_proj_weights(w):
    cuts = np.cumsum(SPLITS)[:-1].tolist()
    (gq, gk, gv, glr, gg, rq, rk, rv, rg, dqkv, db, da, dg, sz, sxbc, sdt, mg) = jnp.split(w, cuts, axis=-1)
    cat = lambda *xs: jnp.concatenate(xs, axis=-1).astype(BF16)
    small = jnp.concatenate([glr, db, da, sdt], axis=-1)
    small = jnp.pad(small, ((0, 0), (0, SMALL_W - small.shape[1]))).astype(BF16)
    return (cat(gq, gk, gv, gg), cat(rq, rk, rv, rg), cat(dqkv, dg), cat(sz, sxbc), mg.astype(BF16),
            small, small.T)


def _rope_tables(pos):
    half = MIX_DK // 2
    inv = ROPE_BASE ** (-jnp.arange(half, dtype=F32) / half)
    ang = jnp.asarray(pos).astype(F32)[:, None] * inv[None, :]
    cos, sin = jnp.cos(ang), jnp.sin(ang)
    cos_t = jnp.tile(jnp.concatenate([cos, cos], axis=-1), (1, MIX_H))
    sin_t = jnp.tile(jnp.concatenate([-sin, sin], axis=-1), (1, MIX_H))
    return cos_t, sin_t


def _block_rows_t(zst, row0, nrows, rblk):
    x = zst[:, row0:row0 + nrows].reshape(SMALL_W, nrows // rblk, rblk)
    return jnp.transpose(x, (1, 0, 2))


def _conv_prev(state_conv):
    b, _, cdim = state_conv.shape
    return jnp.pad(state_conv, ((0, 0), (1, 0), (0, 0))).reshape(b * CONV_W, cdim)


def kernel(x_prompt, x_sample, state_gla, state_ret, state_gdn, state_gdn_conv, state_ssd, state_ssd_conv, p_prompt, p_sample, norm_mix, w_in, gla_w_gk, gla_b_gk, gla_norm, ret_norm, gdn_conv_w, gdn_a_log, gdn_dt_bias, gdn_norm, ssd_conv_w, ssd_conv_b, ssd_a_log, ssd_dt_bias, ssd_d, ssd_norm, w_branch, w_out, norm_ffn, ffn_w_gate, ffn_w_up, ffn_w_down, moe_router, moe_w_gate, moe_w_up, moe_w_down, norm_ple, ple_w_gate, ple_w_proj, norm_final):
    bp, lp, _ = x_prompt.shape
    bs, lsmp, _ = x_sample.shape
    depth = w_in.shape[0]
    n_p = bp * lp
    n_s = bs * lsmp
    h = jnp.concatenate([x_prompt.reshape(n_p, D_MODEL), x_sample.reshape(n_s, D_MODEL)], axis=0)
    cos_p, sin_p = _rope_tables(np.arange(lp))
    cos_s, sin_s = _rope_tables(PAST_LEN + (np.arange(BATCH_SEQS * lsmp) % lsmp))
    rblk = CHUNK * SEQ_BLOCK_CHUNKS
    cb = BATCH_SEQS * lsmp
    mix_state = (MIX_H, MIX_DK, MIX_DV)
    ssd_state = (SSD_H, SSD_N, SSD_P)

    outs_p = [[] for _ in range(6)]
    outs_s = [[] for _ in range(6)]
    for i in range(depth):
        wts = _in_proj_weights(w_in[i])
        za, zb, zc, zd, zg, zs, zst = _in_proj(h, norm_mix[i][None, :], wts)
        zst_p = _block_rows_t(zst, 0, n_p, rblk)
        zst_s = _block_rows_t(zst, n_p, n_s, cb)

        wgk = jnp.pad(gla_w_gk[i], ((LANE_GLR, SMALL_W - LANE_GLR - GLA_RANK), (0, 0))).astype(BF16)
        bgk = gla_b_gk[i][None, :]
        gla_par = [wgk, bgk, gla_norm[i][None, :]]
        oa_p, gla_p = _mixer_seq(_gla_kernel, "gla_seq", za, [(zs, "rows")], gla_par, bp, lp, mix_state,
                                 [pltpu.VMEM(mix_state, F32)])
        oa_s, gla_s = _mixer_batch(_gla_kernel, "gla_batch", za, n_p, [(zs, "rows")], gla_par,
                                   [state_gla[i]], bs, lsmp, mix_state)
        ret_par = [ret_norm[i][None, :]]
        ob_p, ret_p = _mixer_seq(_ret_kernel, "ret_seq", zb, [(cos_p, "pos"), (sin_p, "pos")], ret_par,
                                 bp, lp, mix_state, [pltpu.VMEM(mix_state, F32)])
        ob_s, ret_s = _mixer_batch(_ret_kernel, "ret_batch", zb, n_p, [(cos_s, "pos"), (sin_s, "pos")],
                                   ret_par, [state_ret[i]], bs, lsmp, mix_state)
        gdn_prow = jnp.concatenate([_pad_lanes(gdn_a_log[i][None, :], LANE_GDA),
                                    _pad_lanes(gdn_dt_bias[i][None, :], LANE_GDA)], axis=0)
        gdn_par = [gdn_conv_w[i], gdn_prow, gdn_norm[i][None, :]]
        oc_p, gdn_p = _mixer_seq(_gdn_kernel, "gdn_seq", zc, [(zs, "rows")], gdn_par, bp, lp, mix_state,
                                 [pltpu.VMEM((MIX_H * MIX_DK, MIX_DV), F32), pltpu.VMEM((CHUNK, GDN_CONV_C), F32)])
        oc_s, gdn_s = _mixer_batch(_gdn_kernel, "gdn_batch", zc, n_p, [(zs, "rows")], gdn_par,
                                   [_conv_prev(state_gdn_conv[i]), state_gdn[i]], bs, lsmp, mix_state)
        ssd_prow = jnp.concatenate([_pad_lanes(ssd_a_log[i][None, :], LANE_SDT),
                                    _pad_lanes(ssd_dt_bias[i][None, :], LANE_SDT)], axis=0)
        ssd_par = [ssd_conv_w[i], ssd_conv_b[i][None, :], ssd_prow, ssd_prow.T,
                   jnp.repeat(ssd_d[i], SSD_P)[None, :], ssd_norm[i][None, :]]
        ssd_scr = [pltpu.VMEM((SSD_G, SSD_N, SSD_DI // SSD_G), F32), pltpu.VMEM((CHUNK, SSD_CONV_C), F32)]
        od_p, ssd_p = _mixer_seq(_ssd_kernel, "ssd_seq", zd, [(zs, "rows"), (zst_p, "zst")], ssd_par,
                                 bp, lp, ssd_state, ssd_scr)
        od_s, ssd_s = _mixer_batch(_ssd_kernel, "ssd_batch", zd, n_p, [(zs, "rows"), (zst_s, "zst")], ssd_par,
                                   [_conv_prev(state_ssd_conv[i]), state_ssd[i]], bs, lsmp, ssd_state)

        zc_p = zc[:n_p, :GDN_CONV_C].reshape(bp, lp, GDN_CONV_C)
        zc_s = zc[n_p:, :GDN_CONV_C].reshape(bs, lsmp, GDN_CONV_C)
        zd_p = zd[:n_p, SSD_DI:].reshape(bp, lp, SSD_CONV_C)
        zd_s = zd[n_p:, SSD_DI:].reshape(bs, lsmp, SSD_CONV_C)
        gdnc_p = zc_p[:, lp - (CONV_W - 1):].astype(F32)
        ssdc_p = zd_p[:, lp - (CONV_W - 1):].astype(F32)
        gdnc_s = jnp.concatenate([state_gdn_conv[i], zc_s.astype(F32)], axis=1)[:, lsmp:]
        ssdc_s = jnp.concatenate([state_ssd_conv[i], zd_s.astype(F32)], axis=1)[:, lsmp:]
        for lst, val in zip(outs_p, (gla_p, ret_p, gdn_p, gdnc_p, ssd_p, ssdc_p)):
            lst.append(val)
        for lst, val in zip(outs_s, (gla_s, ret_s, gdn_s, gdnc_s, ssd_s, ssdc_s)):
            lst.append(val)

        branches = [jnp.concatenate([a, b], axis=0)
                    for a, b in ((oa_p, oa_s), (ob_p, ob_s), (oc_p, oc_s), (od_p, od_s))]
        h = _merge(branches, zg, h, w_branch[i].astype(BF16), w_out[i].astype(BF16))

        j = i // 2
        if i % 2 == 0:
            h = _ffn(h, norm_ffn[i][None, :], ffn_w_gate[j].astype(BF16), ffn_w_up[j].astype(BF16),
                     ffn_w_down[j].astype(BF16))
        else:
            router = jnp.pad(moe_router[j], ((0, 0), (0, SMALL_W - N_EXPERTS)))
            h = _moe(h, norm_ffn[i][None, :], router, moe_w_gate[j].astype(BF16), moe_w_up[j].astype(BF16),
                     moe_w_down[j].astype(BF16))
        p_all = jnp.concatenate([p_prompt[i].reshape(n_p, D_PLE), p_sample[i].reshape(n_s, D_PLE)], axis=0)
        h = _ple(h, p_all, norm_ple[i][None, :], ple_w_gate[i].astype(BF16), ple_w_proj[i].astype(BF16),
                 gain_final=norm_final[None, :] if i == depth - 1 else None)

    y_prompt = h[:n_p].reshape(bp, lp, D_MODEL)
    y_sample = h[n_p:].reshape(bs, lsmp, D_MODEL)
    return (y_prompt, y_sample) + tuple(jnp.stack(l) for l in outs_p) + tuple(jnp.stack(l) for l in outs_s)
```

```python
import functools
import math

import numpy as np
import jax
import jax.numpy as jnp
from jax import lax
from jax.experimental import pallas as pl
from jax.experimental.pallas import tpu as pltpu

F32 = jnp.float32
BF16 = jnp.bfloat16
EPS = 1e-6

D_MODEL = 1024
D_PLE = 256
CONV_W = 4
N_BRANCH = 4
BRANCH_W = 512
MIX_H = 4
MIX_DK = 64
MIX_DV = 128
GLA_RANK = 16
GLA_GATE_NORM = 16.0
ROPE_BASE = 10000.0
GDN_CONV_C = 2 * MIX_H * MIX_DK + MIX_H * MIX_DV
SSD_H = 8
SSD_P = 64
SSD_N = 64
SSD_G = 2
SSD_DI = SSD_H * SSD_P
SSD_CONV_C = SSD_DI + 2 * SSD_G * SSD_N
D_FF = 2816
N_EXPERTS = 8
SPLITS = (256, 256, 512, GLA_RANK, 512,
          256, 256, 512, 512,
          GDN_CONV_C, MIX_H, MIX_H, 512,
          SSD_DI, SSD_CONV_C, SSD_H,
          N_BRANCH * D_MODEL)

LANE_GLR = 0
LANE_BETA = 16
LANE_GDA = 20
LANE_SDT = 24
SMALL_W = 128

CHUNK = 64
SEQ_BLOCK_CHUNKS = 4
BATCH_SEQS = 8
ROW_TILE = 512
FF_BLOCKS = 2
VMEM_LIMIT = 56 * 1024 * 1024
PAST_LEN = 16384


def _row_tile(*counts):
    for tm in (ROW_TILE, 256, 128, 64, 32, 16):
        if all(n % tm == 0 for n in counts):
            return tm
    raise ValueError(f"row counts {counts} are not all multiples of 16")


def _rows_specs(parts, tm):
    width = parts[0].shape[1]
    if len(parts) == 1:
        return [pl.BlockSpec((tm, width), lambda i: (i, 0))]
    npt = parts[0].shape[0] // tm
    return [pl.BlockSpec((tm, width), lambda i: (jnp.minimum(i, npt - 1), 0)),
            pl.BlockSpec((tm, width), lambda i: (jnp.maximum(i - npt, 0), 0))]


def _rows_read(refs, npt):
    if len(refs) == 1:
        return refs[0][...]
    return jnp.where(pl.program_id(0) < npt, refs[0][...], refs[1][...])


def _mm(a, b):
    return jnp.dot(a, b, preferred_element_type=F32)


def _mm_nt(a, b):
    return lax.dot_general(a, b, (((1,), (1,)), ((), ())), preferred_element_type=F32)


def _mm_tn(a, b):
    return lax.dot_general(a, b, (((0,), (0,)), ((), ())), preferred_element_type=F32)


def _split(x):
    hi = x.astype(BF16)
    lo = (x - hi.astype(F32)).astype(BF16)
    return hi, lo


def _mm_xl(x, m):
    hi, lo = _split(x)
    return _mm(hi, m) + _mm(lo, m)


def _mm_xr(m, x):
    hi, lo = _split(x)
    return _mm(m, hi) + _mm(m, lo)


def _sigmoid(x):
    return 1.0 / (1.0 + jnp.exp(-x))


def _silu(x):
    return x * _sigmoid(x)


def _softplus(x):
    return jnp.maximum(x, 0.0) + jnp.log1p(jnp.exp(-jnp.abs(x)))


def _rms(x, gain):
    ms = jnp.mean(x * x, axis=-1, keepdims=True)
    return x * lax.rsqrt(ms + EPS) * gain


def _log2(n):
    k = int(round(math.log2(n)))
    assert (1 << k) == n, n
    return k


class _Masks:
    def __init__(self, c, ls):
        self.c, self.ls = c, ls
        sh = _log2(ls)
        r = lax.broadcasted_iota(jnp.int32, (c, c), 0)
        q = lax.broadcasted_iota(jnp.int32, (c, c), 1)
        same = (r >> sh) == (q >> sh)
        self.incl = jnp.logical_and(same, q <= r)
        self.strict = jnp.logical_and(same, q < r)
        self.tri = jnp.where(self.incl, 1.0, 0.0).astype(BF16)
        self.tri_t = jnp.where(jnp.logical_and(same, r <= q), 1.0, 0.0).astype(BF16)
        last =((r >> sh) << sh) + (ls - 1)
        self.lastsel = jnp.where(q == last, 1.0, 0.0).astype(BF16)
        self.eye = jnp.where(r == q, 1.0, 0.0).astype(F32)
        self.r, self.q = r, q
        rc = lax.broadcasted_iota(jnp.int32, (c, 1), 0)
        self.seq_of_row = rc >> sh
        self.t_col = (rc & (ls - 1)).astype(F32)
        self.nseq = c // ls

    def rows_of(self, b, x):
        if self.nseq == 1:
            return x
        return jnp.where(self.seq_of_row == b, x, jnp.zeros_like(x))

    def shift_matrix(self, nprev):
        c, ls = self.c, self.ls
        t = self.r & (ls - 1)
        blocks = []
        for s in (1, 2, 3):
            cur = jnp.logical_and(self.q == self.r - s, t >= s)
            prev = jnp.logical_and(self.q == self.r + (ls - s), t < s)
            cur = jnp.where(cur, 1.0, 0.0).astype(BF16)
            prev = jnp.where(prev, 1.0, 0.0).astype(BF16)
            blocks.append(jnp.concatenate([prev] * nprev + [cur], axis=1))
        return jnp.concatenate(blocks, axis=0)


def _expander(lane0, group, width):
    r = lax.broadcasted_iota(jnp.int32, (SMALL_W, width), 0)
    q = lax.broadcasted_iota(jnp.int32, (SMALL_W, width), 1)
    return jnp.where(r == lane0 + (q >> _log2(group)), 1.0, 0.0).astype(BF16)


def _causal_conv(u, prev_parts, w, masks, shift_m):
    c = masks.c
    x = jnp.concatenate(list(prev_parts) + [u], axis=0)
    y = _mm(shift_m, x)
    out = u.astype(F32) * w[3:4, :]
    for s in (1, 2, 3):
        out = out + y[(s - 1) * c:s * c, :] * w[3 - s:4 - s, :]
    return out


def _pair_decay(gc_col, gt_row, incl):
    d = gc_col - gt_row
    return jnp.where(incl, jnp.exp(jnp.where(incl, d, 0.0)), 0.0)


def _seq_prologue(s_scr, prev_scr=None):
    @pl.when(pl.program_id(1) == 0)
    def _():
        s_scr[...] = jnp.zeros_like(s_scr)
        if prev_scr is not None:
            prev_scr[...] = jnp.zeros_like(prev_scr)


def _seq_epilogue(s_scr, s1_ref):
    @pl.when(pl.program_id(1) == pl.num_programs(1) - 1)
    def _():
        s1_ref[0] = s_scr[...].reshape(s1_ref.shape[1:])


def _in_proj_kernel(*refs, nh, npt):
    h_refs, (g_ref, wa, wb, wc, wd, wg, ws, za, zb, zc, zd, zg, zs) = refs[:nh], refs[nh:]
    xn = _rms(_rows_read(h_refs, npt), g_ref[...]).astype(BF16)
    for w_ref, o_ref in ((wa, za), (wb, zb), (wc, zc), (wd, zd), (wg, zg)):
        width = o_ref.shape[1]
        for j in range(0, width, 512):
            jw = min(512, width - j)
            o_ref[:, j:j + jw] = _mm(xn, w_ref[:, j:j + jw]).astype(o_ref.dtype)
    zs[...] = _mm(xn, ws[...])


def _in_proj(h_parts, gain, wts):
    counts = [p.shape[0] for p in h_parts]
    n = sum(counts)
    tm = _row_tile(*counts)
    wa, wb, wc, wd, wg, ws = wts
    const = lambda i: (0, 0)
    row = lambda i: (i, 0)

    def wspec(w):
        return pl.BlockSpec(w.shape, const, pipeline_mode=pl.Buffered(1))

    outs = [jax.ShapeDtypeStruct((n, w.shape[1]), BF16) for w in (wa, wb, wc, wd, wg)]
    outs.append(jax.ShapeDtypeStruct((n, SMALL_W), F32))
    out_specs = [pl.BlockSpec((tm, w.shape[1]), row) for w in (wa, wb, wc, wd, wg)]
    out_specs.append(pl.BlockSpec((tm, SMALL_W), row))
    return pl.pallas_call(
        functools.partial(_in_proj_kernel, nh=len(h_parts), npt=counts[0] // tm),
        grid=(n // tm,),
        in_specs=_rows_specs(h_parts, tm) + [pl.BlockSpec((1, D_MODEL), const)] + [wspec(w) for w in wts],
        out_specs=out_specs,
        out_shape=outs,
        compiler_params=pltpu.CompilerParams(dimension_semantics=("parallel",), vmem_limit_bytes=VMEM_LIMIT),
        name="in_proj",
    )(*h_parts, gain, *wts)


class _Stack:
    def __init__(self, c, ls):
        self.c, self.ls = c, ls
        self.hs = MIX_H * c
        self.nseq = c // ls
        self.masks = _Masks(self.hs, ls)
        self.head_of_lane = lax.broadcasted_iota(jnp.int32, (1, MIX_H * MIX_DK), 1) >> _log2(MIX_DK)
        self.seq_of_row = (lax.broadcasted_iota(jnp.int32, (self.hs, 1), 0) & (c - 1)) >> _log2(ls)

    def keys(self, x):
        return jnp.concatenate(
            [jnp.where(self.head_of_lane == h, x, jnp.zeros_like(x)) for h in range(MIX_H)], axis=0)

    def values(self, x):
        return jnp.concatenate([x[:, MIX_DV * h:MIX_DV * (h + 1)] for h in range(MIX_H)], axis=0)

    def rows_of(self, b, x):
        if self.nseq == 1:
            return x
        return jnp.where(self.seq_of_row == b, x, jnp.zeros_like(x))


def _write_heads(o_ref, rows, o_stacked, c, gain, gate):
    for h in range(MIX_H):
        vl = slice(MIX_DV * h, MIX_DV * (h + 1))
        y = _rms(o_stacked[h * c:(h + 1) * c, :], gain) * _silu(gate[:, vl])
        o_ref[rows, vl] = y.astype(o_ref.dtype)


def _gla_kernel(*refs, c, cpb, ls, seq_mode):
    if seq_mode:
        z_ref, zs_ref, wgk_ref, bgk_ref, gain_ref, o_ref, s1_ref, s_scr = refs
        s0_ref = None
        _seq_prologue(s_scr)
    else:
        z_ref, zs_ref, wgk_ref, bgk_ref, gain_ref, s0_ref, o_ref, s1_ref = refs
        s_scr = None
    hdk = MIX_H * MIX_DK
    mk = _Masks(c, ls)
    sk = _Stack(c, ls)
    r = lax.broadcasted_iota(jnp.int32, (hdk, hdk), 0)
    q = lax.broadcasted_iota(jnp.int32, (hdk, hdk), 1)
    eye_k = r == q
    ones_dv = jnp.ones((hdk, MIX_DV), BF16)
    chunks = []
    for ci in range(cpb):
        rows = slice(ci * c, (ci + 1) * c)
        z = z_ref[rows, :]
        qq = z[:, 0:256].astype(F32) * (MIX_DK ** -0.5)
        k = z[:, 256:512].astype(F32)
        pre = _mm(zs_ref[rows, :].astype(BF16), wgk_ref[...]) + bgk_ref[...]
        lg = -_softplus(-pre) * (1.0 / GLA_GATE_NORM)
        g = _mm_xr(mk.tri, lg)
        glast = _mm_xr(mk.lastsel, g)
        qe = sk.keys((qq * jnp.exp(g)).astype(BF16))
        ke = sk.keys((k * jnp.exp(-g)).astype(BF16))
        kd = sk.keys((k * jnp.exp(glast - g)).astype(BF16))
        v = sk.values(z[:, 512:1024])
        a = jnp.where(sk.masks.incl, _mm_nt(qe, ke), 0.0).astype(BF16)
        el = jnp.exp(glast)
        decays = []
        for b in range(sk.nseq):
            diag = jnp.where(eye_k, jnp.broadcast_to(el[b * ls:b * ls + 1, :], (hdk, hdk)), 0.0)
            decays.append(_mm_xl(diag, ones_dv))
        chunks.append(dict(rows=rows, qe=qe, kd=kd, v=v, o=_mm(a, v), decays=decays))
    for ch in chunks:
        o = ch["o"]
        for b in range(sk.nseq):
            s = s_scr[...] if seq_mode else s0_ref[b].reshape(hdk, MIX_DV)
            o = o + _mm(sk.rows_of(b, ch["qe"]), s.astype(BF16))
            s_new = s * ch["decays"][b] + _mm_tn(sk.rows_of(b, ch["kd"]), ch["v"])
            if seq_mode:
                s_scr[...] = s_new
            else:
                s1_ref[b] = s_new.reshape(MIX_H, MIX_DK, MIX_DV)
        rows = ch["rows"]
        _write_heads(o_ref, rows, o, c, gain_ref[...], z_ref[rows, 1024:1536].astype(F32))
    if seq_mode:
        _seq_epilogue(s_scr, s1_ref)


def _ret_kernel(*refs, c, cpb, ls, seq_mode):
    if seq_mode:
        z_ref, cos_ref, sin_ref, gain_ref, o_ref, s1_ref, s_scr = refs
        s0_ref = None
        _seq_prologue(s_scr)
    else:
        z_ref, cos_ref, sin_ref, gain_ref, s0_ref, o_ref, s1_ref = refs
        s_scr = None
    hdk = MIX_H * MIX_DK
    mk = _Masks(c, ls)
    sk = _Stack(c, ls)
    ms = sk.masks
    lgam = [math.log(1.0 - 2.0 ** (-5.0 - h)) for h in range(MIX_H)]

    def per_head(idx):
        out = jnp.zeros(idx.shape, F32)
        for h in range(MIX_H):
            out = jnp.where(idx == h, lgam[h], out)
        return out

    lane = lax.broadcasted_iota(jnp.int32, (1, hdk), 1)
    lg_lane = per_head(lane >> _log2(MIX_DK))
    lg_srow = per_head(lax.broadcasted_iota(jnp.int32, (sk.hs, 1), 0) >> _log2(c))
    lg_krow = per_head(lax.broadcasted_iota(jnp.int32, (hdk, 1), 0) >> _log2(MIX_DK))
    first_half = (lane & (MIX_DK - 1)) < MIX_DK // 2
    eg = jnp.exp((mk.t_col + 1.0) * lg_lane)
    ed = jnp.exp((ls - 1.0 - mk.t_col) * lg_lane)
    dt_pos = ((ms.r & (ls - 1)) - (ms.q & (ls - 1))).astype(F32)
    dm = jnp.where(ms.incl, jnp.exp(jnp.where(ms.incl, dt_pos * lg_srow, 0.0)), 0.0)
    dec = jnp.broadcast_to(jnp.exp(lg_krow * float(ls)), (hdk, MIX_DV))

    def rope(x, cs, sn):
        sw = jnp.where(first_half, pltpu.roll(x, hdk - MIX_DK // 2, 1), pltpu.roll(x, MIX_DK // 2, 1))
        return x * cs + sw * sn

    chunks = []
    for ci in range(cpb):
        rows = slice(ci * c, (ci + 1) * c)
        z = z_ref[rows, :]
        cs = cos_ref[rows, :]
        sn = sin_ref[rows, :]
        qr = rope(z[:, 0:256].astype(F32), cs, sn)
        kr = rope(z[:, 256:512].astype(F32), cs, sn) * (MIX_DK ** -0.5)
        v = sk.values(z[:, 512:1024])
        a = (_mm_nt(sk.keys(qr.astype(BF16)), sk.keys(kr.astype(BF16))) * dm).astype(BF16)
        chunks.append(dict(rows=rows, qe=sk.keys((qr * eg).astype(BF16)), kd=sk.keys((kr * ed).astype(BF16)),
                           v=v, o=_mm(a, v)))
    for ch in chunks:
        o = ch["o"]
        for b in range(sk.nseq):
            s = s_scr[...] if seq_mode else s0_ref[b].reshape(hdk, MIX_DV)
            o = o + _mm(sk.rows_of(b, ch["qe"]), s.astype(BF16))
            s_new = s * dec + _mm_tn(sk.rows_of(b, ch["kd"]), ch["v"])
            if seq_mode:
                s_scr[...] = s_new
            else:
                s1_ref[b] = s_new.reshape(MIX_H, MIX_DK, MIX_DV)
        rows = ch["rows"]
        _write_heads(o_ref, rows, o, c, gain_ref[...], z_ref[rows, 1024:1536].astype(F32))
    if seq_mode:
        _seq_epilogue(s_scr, s1_ref)


def _gdn_kernel(*refs, c, cpb, ls, seq_mode):
    if seq_mode:
        z_ref, zs_ref, cw_ref, prow_ref, gain_ref, o_ref, s1_ref, s_scr, prev_scr = refs
        s0_ref = prev_ref = None
        _seq_prologue(s_scr, prev_scr)
    else:
        z_ref, zs_ref, cw_ref, prow_ref, gain_ref, prev_ref, s0_ref, o_ref, s1_ref = refs
        s_scr = prev_scr = None
    hdk = MIX_H * MIX_DK
    mk = _Masks(c, ls)
    sk = _Stack(c, ls)
    hs, mks = sk.hs, sk.masks
    shift_m = mk.shift_matrix(1 if seq_mode else 2)
    r = lax.broadcasted_iota(jnp.int32, (hdk, hdk), 0)
    q = lax.broadcasted_iota(jnp.int32, (hdk, hdk), 1)
    ones_bd = jnp.where((r >> 6) == (q >> 6), 1.0, 0.0).astype(BF16)
    e_beta_k = _expander(LANE_BETA, MIX_DK, hdk)
    e_beta_v = _expander(LANE_BETA, MIX_DV, MIX_H * MIX_DV)
    e_g_k = _expander(LANE_GDA, MIX_DK, hdk)
    e_g_v = _expander(LANE_GDA, MIX_DV, MIX_H * MIX_DV)
    a_row = -jnp.exp(prow_ref[0:1, :])
    b_row = prow_ref[1:2, :]
    cw = cw_ref[...]
    lane0 = jnp.where(lax.broadcasted_iota(jnp.int32, (hs, SMALL_W), 1) == 0, 1.0, 0.0).astype(BF16)
    stack_k, stack_v, srows_of = sk.keys, sk.values, sk.rows_of

    chunks = []
    for ci in range(cpb):
        rows = slice(ci * c, (ci + 1) * c)
        z = z_ref[rows, :]
        u = z[:, 0:GDN_CONV_C]
        if seq_mode:
            prev_parts = [prev_scr[...].astype(BF16)]
        else:
            prev_parts = list(_split(prev_ref[...]))
        cqkv = _silu(_causal_conv(u, prev_parts, cw, mk, shift_m))
        if seq_mode:
            prev_scr[...] = u.astype(F32)
        cq = cqkv[:, 0:256]
        ck = cqkv[:, 256:512]
        cv = cqkv[:, 512:1024]
        qn = cq * lax.rsqrt(_mm_xl(cq * cq, ones_bd) + EPS) * (MIX_DK ** -0.5)
        kn = ck * lax.rsqrt(_mm_xl(ck * ck, ones_bd) + EPS)
        zs = zs_ref[rows, :]
        beta = _sigmoid(zs)
        lgd = a_row * _softplus(zs + b_row)
        gc = _mm_xr(mk.tri, lgd)
        glast = _mm_xr(mk.lastsel, gc)
        beta_k = _mm_xl(beta, e_beta_k)
        beta_v = _mm_xl(beta, e_beta_v)
        eg_k = _mm_xl(jnp.exp(gc), e_g_k)
        dl_k = _mm_xl(jnp.exp(glast - gc), e_g_k)
        el_v = _mm_xl(jnp.exp(glast), e_g_v)
        kbeta = kn * beta_k
        k_st = stack_k(kn.astype(BF16))
        g_col = jnp.concatenate([gc[:, LANE_GDA + h:LANE_GDA + h + 1] for h in range(MIX_H)], axis=0)
        g_hi, g_lo = _split(jnp.broadcast_to(g_col, (hs, SMALL_W)))
        g_row = _mm_nt(lane0, g_hi) + _mm_nt(lane0, g_lo)
        dm = _pair_decay(g_col, g_row, mks.incl)
        m = _mm_nt(stack_k(kbeta.astype(BF16)), k_st) * jnp.where(mks.strict, dm, 0.0)
        chunks.append(dict(
            rows=rows, m=m, p=mks.eye - m, mp=m,
            a=(_mm_nt(stack_k(qn.astype(BF16)), k_st) * dm).astype(BF16),
            vbeta=stack_v((cv * beta_v).astype(BF16)),
            kbe=stack_k((kbeta * eg_k).astype(BF16)),
            qe=stack_k((qn * eg_k).astype(BF16)),
            kd=stack_k((kn * dl_k).astype(BF16)),
            el_v=el_v))
    n = 2
    while n < ls:
        for ch in chunks:
            mpb = ch["mp"].astype(BF16)
            ch["mp"] = _mm(mpb, mpb)
        for ch in chunks:
            ch["p"] = ch["p"] + _mm(ch["p"].astype(BF16), ch["mp"].astype(BF16))
        n *= 2
    for ch in chunks:
        tinv = ch["p"].astype(BF16)
        ch["uu"] = _mm(tinv, ch["vbeta"])
        ch["ww"] = _mm(tinv, ch["kbe"]).astype(BF16)
    for ch in chunks:
        rows = ch["rows"]
        states = []
        vn = ch["uu"]
        for b in range(mk.nseq):
            s = s_scr[...] if seq_mode else s0_ref[b].reshape(hdk, MIX_DV)
            states.append(s)
            vn = vn - _mm(srows_of(b, ch["ww"]), s.astype(BF16))
        vnb = vn.astype(BF16)
        o = _mm(ch["a"], vnb)
        for b in range(mk.nseq):
            s = states[b]
            o = o + _mm(srows_of(b, ch["qe"]), s.astype(BF16))
            dec = jnp.concatenate(
                [jnp.broadcast_to(ch["el_v"][b * ls:b * ls + 1, MIX_DV * h:MIX_DV * (h + 1)], (MIX_DK, MIX_DV))
                 for h in range(MIX_H)], axis=0)
            s_new = s * dec + _mm_tn(srows_of(b, ch["kd"]), vnb)
            if seq_mode:
                s_scr[...] = s_new
            else:
                s1_ref[b] = s_new.reshape(MIX_H, MIX_DK, MIX_DV)
        _write_heads(o_ref, rows, o, c, gain_ref[...], z_ref[rows, GDN_CONV_C:GDN_CONV_C + 512].astype(F32))
    if seq_mode:
        _seq_epilogue(s_scr, s1_ref)


def _ssd_kernel(*refs, c, cpb, ls, seq_mode):
    gw = SSD_DI // SSD_G
    hpg = SSD_H // SSD_G
    if seq_mode:
        z_ref, zs_ref, cw_ref, cb_ref, prow_ref, dx_ref, gain_ref, o_ref, s1_ref, s_scr, prev_scr = refs
        s0_ref = prev_ref = None
        _seq_prologue(s_scr, prev_scr)
    else:
        z_ref, zs_ref, cw_ref, cb_ref, prow_ref, dx_ref, gain_ref, prev_ref, s0_ref, o_ref, s1_ref = refs
        s_scr = prev_scr = None
    mk = _Masks(c, ls)
    shift_m = mk.shift_matrix(1 if seq_mode else 2)
    e_x = _expander(LANE_SDT, SSD_P, SSD_DI)
    a_row = -jnp.exp(prow_ref[0:1, :])
    b_row = prow_ref[1:2, :]
    cw = cw_ref[...]
    lane_g = lax.broadcasted_iota(jnp.int32, (1, gw), 1) >> 6

    def get_state(b, g):
        if seq_mode:
            return s_scr[g]
        return jnp.concatenate([s0_ref[b, g * hpg + j] for j in range(hpg)], axis=-1)

    def put_state(b, g, val):
        if seq_mode:
            s_scr[g] = val
        else:
            for j in range(hpg):
                s1_ref[b, g * hpg + j] = val[:, SSD_P * j:SSD_P * (j + 1)]

    for ci in range(cpb):
        rows = slice(ci * c, (ci + 1) * c)
        z = z_ref[rows, :]
        sz = z[:, 0:SSD_DI].astype(F32)
        u = z[:, SSD_DI:SSD_DI + SSD_CONV_C]
        if seq_mode:
            prev_parts = [prev_scr[...].astype(BF16)]
        else:
            prev_parts = list(_split(prev_ref[...]))
        xbc = _silu(_causal_conv(u, prev_parts, cw, mk, shift_m) + cb_ref[...])
        if seq_mode:
            prev_scr[...] = u.astype(F32)
        sx = xbc[:, 0:SSD_DI]
        s_b = xbc[:, SSD_DI:SSD_DI + SSD_G * SSD_N]
        s_c = xbc[:, SSD_DI + SSD_G * SSD_N:SSD_CONV_C]
        zs = zs_ref[rows, :]
        dt = _softplus(zs + b_row)
        lsd = dt * a_row
        gc = _mm_xr(mk.tri, lsd)
        lsd_hi, lsd_lo = _split(lsd)
        gt = _mm_tn(lsd_hi, mk.tri_t) + _mm_tn(lsd_lo, mk.tri_t)
        glast = _mm_xr(mk.lastsel, gc)
        dt_x = _mm_xl(dt, e_x)
        eg_x = _mm_xl(jnp.exp(gc), e_x)
        dl_x = _mm_xl(jnp.exp(glast - gc), e_x)
        el_x = _mm_xl(jnp.exp(glast), e_x)
        v = sx * dt_x
        vd = (v * dl_x).astype(BF16)
        for g in range(SSD_G):
            gl = slice(gw * g, gw * (g + 1))
            cg = s_c[:, SSD_N * g:SSD_N * (g + 1)].astype(BF16)
            bg = s_b[:, SSD_N * g:SSD_N * (g + 1)].astype(BF16)
            ag = _mm_nt(cg, bg)
            vg = v[:, gl]
            o = jnp.zeros((c, gw), F32)
            for j in range(hpg):
                lane = LANE_SDT + g * hpg + j
                dm = _pair_decay(gc[:, lane:lane + 1], gt[lane:lane + 1, :], mk.incl)
                vj = jnp.where(lane_g == j, vg, 0.0).astype(BF16)
                o = o + _mm((ag * dm).astype(BF16), vj)
            o_state = jnp.zeros((c, gw), F32)
            for b in range(mk.nseq):
                s = get_state(b, g)
                o_state = o_state + _mm(mk.rows_of(b, cg), s.astype(BF16))
                dec = el_x[b * ls:b * ls + 1, gl]
                put_state(b, g, s * dec + _mm_tn(mk.rows_of(b, bg), vd[:, gl]))
            y = o + o_state * eg_x[:, gl] + sx[:, gl] * dx_ref[:, gl]
            y = y * _silu(sz[:, gl])
            o_ref[rows, gl] = _rms(y, gain_ref[:, gl]).astype(o_ref.dtype)
    if seq_mode:
        @pl.when(pl.program_id(1) == pl.num_programs(1) - 1)
        def _():
            for g in range(SSD_G):
                for j in range(hpg):
                    s1_ref[0, g * hpg + j] = s_scr[g][:, SSD_P * j:SSD_P * (j + 1)]


def _full_spec(a, grid_rank):
    nd = a.ndim
    if grid_rank == 1:
        return pl.BlockSpec(a.shape, lambda i: (0,) * nd)
    return pl.BlockSpec(a.shape, lambda i, j: (0,) * nd)


def _mixer_seq(kernel_fn, name, z, extra_rows, params, nbatch, seqlen, state_shape, scratch):
    c, cpb = CHUNK, SEQ_BLOCK_CHUNKS
    rblk = c * cpb
    nblk = seqlen // rblk
    rowmap = lambda b, j: (b * nblk + j, 0)
    in_specs = [pl.BlockSpec((rblk, z.shape[1]), rowmap)]
    args = [z]
    for a, kind in extra_rows:
        if kind == "rows":
            in_specs.append(pl.BlockSpec((rblk, a.shape[1]), rowmap))
        else:
            in_specs.append(pl.BlockSpec((rblk, a.shape[1]), lambda b, j: (j, 0)))
        args.append(a)
    for a in params:
        in_specs.append(_full_spec(a, 2))
        args.append(a)
    sblk = (1,) + state_shape
    out_shape = [jax.ShapeDtypeStruct((nbatch * seqlen, BRANCH_W), BF16),
                 jax.ShapeDtypeStruct((nbatch,) + state_shape, F32)]
    out_specs = [pl.BlockSpec((rblk, BRANCH_W), rowmap),
                 pl.BlockSpec(sblk, lambda b, j: (b,) + (0,) * len(state_shape))]
    return pl.pallas_call(
        functools.partial(kernel_fn, c=c, cpb=cpb, ls=c, seq_mode=True),
        grid=(nbatch, nblk),
        in_specs=in_specs,
        out_specs=out_specs,
        out_shape=out_shape,
        scratch_shapes=scratch,
        compiler_params=pltpu.CompilerParams(dimension_semantics=("parallel", "arbitrary"),
                                             vmem_limit_bytes=VMEM_LIMIT),
        name=name,
    )(*args)


def _mixer_batch(kernel_fn, name, z, row_off, extra_rows, params, per_seq, nbatch, seqlen, state_shape):
    sb = BATCH_SEQS
    c = sb * seqlen
    nsteps = nbatch // sb
    off = row_off // c
    rowmap = lambda i: (off + i, 0)
    in_specs = [pl.BlockSpec((c, z.shape[1]), rowmap)]
    args = [z]
    for a, kind in extra_rows:
        if kind == "rows":
            in_specs.append(pl.BlockSpec((c, a.shape[1]), rowmap))
        else:
            in_specs.append(pl.BlockSpec((c, a.shape[1]), lambda i: (0, 0)))
        args.append(a)
    for a in params:
        in_specs.append(_full_spec(a, 1))
        args.append(a)
    for a in per_seq:
        if a.ndim == 2:
            in_specs.append(pl.BlockSpec((c, a.shape[1]), lambda i: (i, 0)))
        else:
            in_specs.append(pl.BlockSpec((sb,) + a.shape[1:], lambda i: (i,) + (0,) * (a.ndim - 1)))
        args.append(a)
    out_shape = [jax.ShapeDtypeStruct((nbatch * seqlen, BRANCH_W), BF16),
                 jax.ShapeDtypeStruct((nbatch,) + state_shape, F32)]
    out_specs = [pl.BlockSpec((c, BRANCH_W), lambda i: (i, 0)),
                 pl.BlockSpec((sb,) + state_shape, lambda i: (i,) + (0,) * len(state_shape))]
    return pl.pallas_call(
        functools.partial(kernel_fn, c=c, cpb=1, ls=seqlen, seq_mode=False),
        grid=(nsteps,),
        in_specs=in_specs,
        out_specs=out_specs,
        out_shape=out_shape,
        compiler_params=pltpu.CompilerParams(dimension_semantics=("parallel",), vmem_limit_bytes=VMEM_LIMIT),
        name=name,
    )(*args)


def _merge_kernel(*refs, nh, npt):
    branch_refs, zg = refs[:2 * N_BRANCH], refs[2 * N_BRANCH]
    h_refs, (wbr, wout, out_ref) = refs[2 * N_BRANCH + 1:2 * N_BRANCH + 1 + nh], refs[2 * N_BRANCH + 1 + nh:]
    acc = jnp.zeros(out_ref.shape, F32)
    for n in range(N_BRANCH):
        gate = _sigmoid(zg[:, n * D_MODEL:(n + 1) * D_MODEL].astype(F32))
        acc = acc + gate * _mm(_rows_read(branch_refs[2 * n:2 * n + 2], npt), wbr[n])
    out_ref[...] = _rows_read(h_refs, npt) + _mm(acc.astype(BF16), wout[...])


def _merge(branches, zg, h_parts, wbr, wout):
    n_p, n_s = branches[0][0].shape[0], branches[0][1].shape[0]
    n = n_p + n_s
    tm = _row_tile(n_p, n_s)
    row = lambda i: (i, 0)
    in_specs, args = [], []
    for pair in branches:
        in_specs += _rows_specs(pair, tm)
        args += list(pair)
    in_specs.append(pl.BlockSpec((tm, N_BRANCH * D_MODEL), row))
    in_specs += _rows_specs(h_parts, tm)
    in_specs += [pl.BlockSpec(wbr.shape, lambda i: (0, 0, 0)), pl.BlockSpec(wout.shape, lambda i: (0, 0))]
    return pl.pallas_call(
        functools.partial(_merge_kernel, nh=len(h_parts), npt=n_p // tm),
        grid=(n // tm,),
        in_specs=in_specs,
        out_specs=pl.BlockSpec((tm, D_MODEL), row),
        out_shape=jax.ShapeDtypeStruct((n, D_MODEL), F32),
        compiler_params=pltpu.CompilerParams(dimension_semantics=("parallel",), vmem_limit_bytes=VMEM_LIMIT),
        name="merge",
    )(*args, zg, *h_parts, wbr, wout)


def _ffn_kernel(h_ref, g_ref, wg_ref, wu_ref, wd_ref, out_ref, u_scr, acc_scr):
    f = pl.program_id(1)

    @pl.when(f == 0)
    def _():
        u_scr[...] = _rms(h_ref[...], g_ref[...]).astype(BF16)
        acc_scr[...] = jnp.zeros_like(acc_scr)

    u = u_scr[...]
    a = (_silu(_mm(u, wg_ref[...])) * _mm(u, wu_ref[...])).astype(BF16)
    acc_scr[...] += _mm(a, wd_ref[...])

    @pl.when(f == pl.num_programs(1) - 1)
    def _():
        out_ref[...] = h_ref[...] + acc_scr[...]


def _ffn(h, gain, wg, wu, wd):
    n = h.shape[0]
    tm = _row_tile(n)
    tf = D_FF // FF_BLOCKS
    row = lambda i, f: (i, 0)
    in_specs = [pl.BlockSpec((tm, D_MODEL), row), pl.BlockSpec((1, D_MODEL), lambda i, f: (0, 0)),
                pl.BlockSpec((D_MODEL, tf), lambda i, f: (0, f)),
                pl.BlockSpec((D_MODEL, tf), lambda i, f: (0, f)),
                pl.BlockSpec((tf, D_MODEL), lambda i, f: (f, 0))]
    return pl.pallas_call(
        _ffn_kernel,
        grid=(n // tm, FF_BLOCKS),
        in_specs=in_specs,
        out_specs=pl.BlockSpec((tm, D_MODEL), row),
        out_shape=jax.ShapeDtypeStruct((n, D_MODEL), F32),
        scratch_shapes=[pltpu.VMEM((tm, D_MODEL), BF16), pltpu.VMEM((tm, D_MODEL), F32)],
        compiler_params=pltpu.CompilerParams(dimension_semantics=("parallel", "arbitrary"),
                                             vmem_limit_bytes=VMEM_LIMIT),
        name="dense_ffn",
    )(h, gain, wg, wu, wd)


MOE_TOKENS = 768
MOE_CAP = 256


def _moe_kernel(h_ref, g_ref, rt_ref, wg_ref, wu_ref, wd_ref, out_ref,
                u_scr, acc_scr, w_scr, sel_scr, rank_scr, selt_scr, rankt_scr, xc_scr, yc_scr, cnt_scr):
    tt = h_ref.shape[0]
    cap = MOE_CAP
    e = pl.program_id(1)
    f = pl.program_id(2)
    nf = pl.num_programs(2)

    @pl.when(jnp.logical_and(e == 0, f == 0))
    def _():
        u = _rms(h_ref[...], g_ref[...])
        u_scr[...] = u.astype(BF16)
        acc_scr[...] = jnp.zeros_like(acc_scr)
        uh, ul = _split(u)
        rh, rl = _split(rt_ref[...])
        logits = _mm(uh, rh) + (_mm(uh, rl) + _mm(ul, rh))
        lane = lax.broadcasted_iota(jnp.int32, logits.shape, 1).astype(F32)
        neg = -3.0e38
        lg = jnp.where(lane < N_EXPERTS, logits, neg)
        m1 = jnp.max(lg, axis=-1, keepdims=True)
        i1 = jnp.min(jnp.where(lg == m1, lane, float(SMALL_W)), axis=-1, keepdims=True)
        lg2 = jnp.where(lane == i1, neg, lg)
        m2 = jnp.max(lg2, axis=-1, keepdims=True)
        i2 = jnp.min(jnp.where(lg2 == m2, lane, float(SMALL_W)), axis=-1, keepdims=True)
        e2 = jnp.exp(m2 - m1)
        w_scr[...] = jnp.where(lane == i1, 1.0 / (1.0 + e2), 0.0) + jnp.where(lane == i2, e2 / (1.0 + e2), 0.0)
        sel = jnp.where(jnp.logical_or(lane == i1, lane == i2), 1.0, 0.0)
        sel_scr[...] = sel
        selb = sel.astype(BF16)
        r = lax.broadcasted_iota(jnp.int32, (tt, tt), 0)
        q = lax.broadcasted_iota(jnp.int32, (tt, tt), 1)
        before = jnp.where(r < q, 1.0, 0.0).astype(BF16)
        ident = jnp.where(r == q, 1.0, 0.0).astype(BF16)
        rank_scr[...] = _mm_tn(before, selb)
        rankt_scr[...] = _mm_tn(selb, before)
        selt_scr[...] = _mm_tn(selb, ident)
        cnt = jnp.sum(sel, axis=0, keepdims=True)
        for ee in range(N_EXPERTS):
            cnt_scr[ee] = jnp.sum(jnp.where(lane[0:1, :] == float(ee), cnt, 0.0)).astype(jnp.int32)

    nsub = (cnt_scr[e] + (cap - 1)) // cap
    lane_w = lax.broadcasted_iota(jnp.int32, (tt, SMALL_W), 1)

    def col_of(ref):
        return jnp.sum(jnp.where(lane_w == e, ref[...], 0.0), axis=-1, keepdims=True)

    @pl.when(f == 0)
    def _():
        rank_row = rankt_scr[pl.ds(e, 1), :]
        sel_row = selt_scr[pl.ds(e, 1), :]
        jcol = lax.broadcasted_iota(jnp.int32, (cap, 1), 0).astype(F32)

        def gather(s, carry):
            base = (s * cap).astype(F32)
            hit = jnp.logical_and(rank_row == jcol + base, sel_row > 0.5)
            onehot = jnp.where(hit, 1.0, 0.0).astype(BF16)
            off = pl.multiple_of(s * cap, cap)
            xc_scr[pl.ds(off, cap), :] = _mm(onehot, u_scr[...]).astype(BF16)
            return carry

        lax.fori_loop(0, nsub, gather, 0)

    def expert(s, carry):
        off = pl.multiple_of(s * cap, cap)
        x = xc_scr[pl.ds(off, cap), :]
        a = (_silu(_mm(x, wg_ref[0])) * _mm(x, wu_ref[0])).astype(BF16)
        y = _mm(a, wd_ref[0])

        @pl.when(f == 0)
        def _():
            yc_scr[pl.ds(off, cap), :] = y

        @pl.when(f != 0)
        def _():
            yc_scr[pl.ds(off, cap), :] += y

        return carry

    lax.fori_loop(0, nsub, expert, 0)

    @pl.when(f == nf - 1)
    def _():
        rank_col = col_of(rank_scr)
        sel_col = col_of(sel_scr)
        w_col = col_of(w_scr)
        jrow = lax.broadcasted_iota(jnp.int32, (1, cap), 1).astype(F32)

        def scatter(s, carry):
            base = (s * cap).astype(F32)
            hit = jnp.logical_and(rank_col == jrow + base, sel_col > 0.5)
            onehot = jnp.where(hit, 1.0, 0.0).astype(BF16)
            off = pl.multiple_of(s * cap, cap)
            acc_scr[...] += w_col * _mm(onehot, yc_scr[pl.ds(off, cap), :].astype(BF16))
            return carry

        lax.fori_loop(0, nsub, scatter, 0)

    @pl.when(jnp.logical_and(e == pl.num_programs(1) - 1, f == nf - 1))
    def _():
        out_ref[...] = h_ref[...] + acc_scr[...]


def _moe(h, gain, router, wg, wu, wd):
    n = h.shape[0]
    tt = MOE_TOKENS if n % MOE_TOKENS == 0 else _row_tile(n)
    assert tt % MOE_CAP == 0 or tt < MOE_CAP, (tt, MOE_CAP)
    ne = wg.shape[0]
    tf = D_FF // FF_BLOCKS
    row = lambda i, e, f: (i, 0)
    nrows = max(tt, MOE_CAP)
    in_specs = [pl.BlockSpec((tt, D_MODEL), row), pl.BlockSpec((1, D_MODEL), lambda i, e, f: (0, 0)),
                pl.BlockSpec(router.shape, lambda i, e, f: (0, 0)),
                pl.BlockSpec((1, D_MODEL, tf), lambda i, e, f: (e, 0, f)),
                pl.BlockSpec((1, D_MODEL, tf), lambda i, e, f: (e, 0, f)),
                pl.BlockSpec((1, tf, D_MODEL), lambda i, e, f: (e, f, 0))]
    scratch = [pltpu.VMEM((tt, D_MODEL), BF16), pltpu.VMEM((tt, D_MODEL), F32),
               pltpu.VMEM((tt, SMALL_W), F32), pltpu.VMEM((tt, SMALL_W), F32), pltpu.VMEM((tt, SMALL_W), F32),
               pltpu.VMEM((SMALL_W, tt), F32), pltpu.VMEM((SMALL_W, tt), F32),
               pltpu.VMEM((nrows, D_MODEL), BF16), pltpu.VMEM((nrows, D_MODEL), F32),
               pltpu.SMEM((N_EXPERTS,), jnp.int32)]
    return pl.pallas_call(
        _moe_kernel,
        grid=(n // tt, ne, FF_BLOCKS),
        in_specs=in_specs,
        out_specs=pl.BlockSpec((tt, D_MODEL), row),
        out_shape=jax.ShapeDtypeStruct((n, D_MODEL), F32),
        scratch_shapes=scratch,
        compiler_params=pltpu.CompilerParams(dimension_semantics=("parallel", "arbitrary", "arbitrary"),
                                             vmem_limit_bytes=VMEM_LIMIT),
        name="moe",
    )(h, gain, router, wg, wu, wd)


def _ple_kernel(*refs, final, npt):
    if final:
        h_ref, pp_ref, ps_ref, g_ref, wgate, wproj, gf_ref, yp_ref, ys_ref = refs
    else:
        h_ref, pp_ref, ps_ref, g_ref, wgate, wproj, out_ref = refs
    h = h_ref[...]
    gate = _sigmoid(_mm(_rms(h, g_ref[...]).astype(BF16), wgate[...]))
    h = h + gate * _mm(_rows_read((pp_ref, ps_ref), npt).astype(BF16), wproj[...])
    if final:
        y = _rms(h, gf_ref[...])

        @pl.when(pl.program_id(0) < npt)
        def _():
            yp_ref[...] = y

        @pl.when(pl.program_id(0) >= npt)
        def _():
            ys_ref[...] = y
    else:
        out_ref[...] = h


def _ple(h, p_parts, gain, wgate, wproj, gain_final=None):
    n_p, n_s = p_parts[0].shape[0], p_parts[1].shape[0]
    n = h.shape[0]
    tm = _row_tile(n_p, n_s)
    final = gain_final is not None
    row = lambda i: (i, 0)
    const = lambda i: (0, 0)
    in_specs = [pl.BlockSpec((tm, D_MODEL), row)] + _rows_specs(p_parts, tm)
    in_specs += [pl.BlockSpec((1, D_MODEL), const), pl.BlockSpec(wgate.shape, const), pl.BlockSpec(wproj.shape, const)]
    args = [h, *p_parts, gain, wgate, wproj]
    if final:
        in_specs.append(pl.BlockSpec((1, D_MODEL), const))
        args.append(gain_final)
        out_shape = [jax.ShapeDtypeStruct((n_p, D_MODEL), F32), jax.ShapeDtypeStruct((n_s, D_MODEL), F32)]
        out_specs = _rows_specs(out_shape, tm)
    else:
        out_shape = jax.ShapeDtypeStruct((n, D_MODEL), F32)
        out_specs = pl.BlockSpec((tm, D_MODEL), row)
    return pl.pallas_call(
        functools.partial(_ple_kernel, final=final, npt=n_p // tm),
        grid=(n // tm,),
        in_specs=in_specs,
        out_specs=out_specs,
        out_shape=out_shape,
        compiler_params=pltpu.CompilerParams(dimension_semantics=("arbitrary",), vmem_limit_bytes=VMEM_LIMIT),
        name="ple",
    )(*args)


def _pad_lanes(x, lane0):
    w = x.shape[-1]
    pad = [(0, 0)] * (x.ndim - 1) + [(lane0, SMALL_W - lane0 - w)]
    return jnp.pad(x, pad)


def _in_proj_weights(w):
    cuts = np.cumsum(SPLITS)[:-1].tolist()
    (gq, gk, gv, glr, gg, rq, rk, rv, rg, dqkv, db, da, dg, sz, sxbc, sdt, mg) = jnp.split(w, cuts, axis=-1)
    cat = lambda *xs: jnp.concatenate(xs, axis=-1).astype(BF16)
    small = jnp.concatenate([glr, db, da, sdt], axis=-1)
    small = jnp.pad(small, ((0, 0), (0, SMALL_W - small.shape[1]))).astype(BF16)
    return (cat(gq, gk, gv, gg), cat(rq, rk, rv, rg), cat(dqkv, dg), cat(sz, sxbc), mg.astype(BF16),
            small)


def _rope_tables(pos):
    half = MIX_DK // 2
    inv = ROPE_BASE ** (-jnp.arange(half, dtype=F32) / half)
    ang = jnp.asarray(pos).astype(F32)[:, None] * inv[None, :]
    cos, sin = jnp.cos(ang), jnp.sin(ang)
    cos_t = jnp.tile(jnp.concatenate([cos, cos], axis=-1), (1, MIX_H))
    sin_t = jnp.tile(jnp.concatenate([-sin, sin], axis=-1), (1, MIX_H))
    return cos_t, sin_t


def _conv_prev(state_conv):
    b, _, cdim = state_conv.shape
    return jnp.pad(state_conv, ((0, 0), (1, 0), (0, 0))).reshape(b * CONV_W, cdim)


def kernel(x_prompt, x_sample, state_gla, state_ret, state_gdn, state_gdn_conv, state_ssd, state_ssd_conv, p_prompt, p_sample, norm_mix, w_in, gla_w_gk, gla_b_gk, gla_norm, ret_norm, gdn_conv_w, gdn_a_log, gdn_dt_bias, gdn_norm, ssd_conv_w, ssd_conv_b, ssd_a_log, ssd_dt_bias, ssd_d, ssd_norm, w_branch, w_out, norm_ffn, ffn_w_gate, ffn_w_up, ffn_w_down, moe_router, moe_w_gate, moe_w_up, moe_w_down, norm_ple, ple_w_gate, ple_w_proj, norm_final):
    bp, lp, _ = x_prompt.shape
    bs, lsmp, _ = x_sample.shape
    depth = w_in.shape[0]
    n_p = bp * lp
    n_s = bs * lsmp
    h_parts = (x_prompt.reshape(n_p, D_MODEL), x_sample.reshape(n_s, D_MODEL))
    cos_p, sin_p = _rope_tables(np.arange(lp))
    cos_s, sin_s = _rope_tables(PAST_LEN + (np.arange(BATCH_SEQS * lsmp) % lsmp))
    mix_state = (MIX_H, MIX_DK, MIX_DV)
    stacked_state = (MIX_H * MIX_DK, MIX_DV)
    ssd_state = (SSD_H, SSD_N, SSD_P)

    outs_p = [[] for _ in range(6)]
    outs_s = [[] for _ in range(6)]
    for i in range(depth):
        wts = _in_proj_weights(w_in[i])
        za, zb, zc, zd, zg, zs = _in_proj(h_parts, norm_mix[i][None, :], wts)

        wgk = jnp.pad(gla_w_gk[i], ((LANE_GLR, SMALL_W - LANE_GLR - GLA_RANK), (0, 0))).astype(BF16)
        bgk = gla_b_gk[i][None, :]
        gla_par = [wgk, bgk, gla_norm[i][None, :]]
        oa_p, gla_p = _mixer_seq(_gla_kernel, "gla_seq", za, [(zs, "rows")], gla_par, bp, lp, mix_state,
                                 [pltpu.VMEM(stacked_state, F32)])
        oa_s, gla_s = _mixer_batch(_gla_kernel, "gla_batch", za, n_p, [(zs, "rows")], gla_par,
                                   [state_gla[i]], bs, lsmp, mix_state)
        ret_par = [ret_norm[i][None, :]]
        ob_p, ret_p = _mixer_seq(_ret_kernel, "ret_seq", zb, [(cos_p, "pos"), (sin_p, "pos")], ret_par,
                                 bp, lp, mix_state, [pltpu.VMEM(stacked_state, F32)])
        ob_s, ret_s = _mixer_batch(_ret_kernel, "ret_batch", zb, n_p, [(cos_s, "pos"), (sin_s, "pos")],
                                   ret_par, [state_ret[i]], bs, lsmp, mix_state)
        gdn_prow = jnp.concatenate([_pad_lanes(gdn_a_log[i][None, :], LANE_GDA),
                                    _pad_lanes(gdn_dt_bias[i][None, :], LANE_GDA)], axis=0)
        gdn_par = [gdn_conv_w[i], gdn_prow, gdn_norm[i][None, :]]
        oc_p, gdn_p = _mixer_seq(_gdn_kernel, "gdn_seq", zc, [(zs, "rows")], gdn_par, bp, lp, mix_state,
                                 [pltpu.VMEM(stacked_state, F32), pltpu.VMEM((CHUNK, GDN_CONV_C), F32)])
        oc_s, gdn_s = _mixer_batch(_gdn_kernel, "gdn_batch", zc, n_p, [(zs, "rows")], gdn_par,
                                   [_conv_prev(state_gdn_conv[i]), state_gdn[i]], bs, lsmp, mix_state)
        ssd_prow = jnp.concatenate([_pad_lanes(ssd_a_log[i][None, :], LANE_SDT),
                                    _pad_lanes(ssd_dt_bias[i][None, :], LANE_SDT)], axis=0)
        ssd_par = [ssd_conv_w[i], ssd_conv_b[i][None, :], ssd_prow,
                   jnp.repeat(ssd_d[i], SSD_P)[None, :], ssd_norm[i][None, :]]
        ssd_scr = [pltpu.VMEM((SSD_G, SSD_N, SSD_DI // SSD_G), F32), pltpu.VMEM((CHUNK, SSD_CONV_C), F32)]
        od_p, ssd_p = _mixer_seq(_ssd_kernel, "ssd_seq", zd, [(zs, "rows")], ssd_par,
                                 bp, lp, ssd_state, ssd_scr)
        od_s, ssd_s = _mixer_batch(_ssd_kernel, "ssd_batch", zd, n_p, [(zs, "rows")], ssd_par,
                                   [_conv_prev(state_ssd_conv[i]), state_ssd[i]], bs, lsmp, ssd_state)

        zc_p = zc[:n_p, :GDN_CONV_C].reshape(bp, lp, GDN_CONV_C)
        zc_s = zc[n_p:, :GDN_CONV_C].reshape(bs, lsmp, GDN_CONV_C)
        zd_p = zd[:n_p, SSD_DI:].reshape(bp, lp, SSD_CONV_C)
        zd_s = zd[n_p:, SSD_DI:].reshape(bs, lsmp, SSD_CONV_C)
        gdnc_p = zc_p[:, lp - (CONV_W - 1):].astype(F32)
        ssdc_p = zd_p[:, lp - (CONV_W - 1):].astype(F32)
        gdnc_s = jnp.concatenate([state_gdn_conv[i], zc_s.astype(F32)], axis=1)[:, lsmp:]
        ssdc_s = jnp.concatenate([state_ssd_conv[i], zd_s.astype(F32)], axis=1)[:, lsmp:]
        for lst, val in zip(outs_p, (gla_p, ret_p, gdn_p, gdnc_p, ssd_p, ssdc_p)):
            lst.append(val)
        for lst, val in zip(outs_s, (gla_s, ret_s, gdn_s, gdnc_s, ssd_s, ssdc_s)):
            lst.append(val)

        branches = ((oa_p, oa_s), (ob_p, ob_s), (oc_p, oc_s), (od_p, od_s))
        h = _merge(branches, zg, h_parts, w_branch[i].astype(BF16), w_out[i].astype(BF16))

        j = i // 2
        if i % 2 == 0:
            h = _ffn(h, norm_ffn[i][None, :], ffn_w_gate[j].astype(BF16), ffn_w_up[j].astype(BF16),
                     ffn_w_down[j].astype(BF16))
        else:
            router = jnp.pad(moe_router[j], ((0, 0), (0, SMALL_W - N_EXPERTS)))
            h = _moe(h, norm_ffn[i][None, :], router, moe_w_gate[j].astype(BF16), moe_w_up[j].astype(BF16),
                     moe_w_down[j].astype(BF16))
        p_parts = (p_prompt[i].reshape(n_p, D_PLE), p_sample[i].reshape(n_s, D_PLE))
        h = _ple(h, p_parts, norm_ple[i][None, :], ple_w_gate[i].astype(BF16), ple_w_proj[i].astype(BF16),
                 gain_final=norm_final[None, :] if i == depth - 1 else None)
        h_parts = (h,)

    y_prompt = h[0].reshape(bp, lp, D_MODEL)
    y_sample = h[1].reshape(bs, lsmp, D_MODEL)
    return (y_prompt, y_sample) + tuple(jnp.stack(l) for l in outs_p) + tuple(jnp.stack(l) for l in outs_s)
```

```python
import functools
import math

import numpy as np
import jax
import jax.numpy as jnp
from jax import lax
from jax.experimental import pallas as pl
from jax.experimental.pallas import tpu as pltpu

F32 = jnp.float32
BF16 = jnp.bfloat16
EPS = 1e-6

D_MODEL = 1024
D_PLE = 256
CONV_W = 4
N_BRANCH = 4
BRANCH_W = 512
MIX_H = 4
MIX_DK = 64
MIX_DV = 128
GLA_RANK = 16
GLA_GATE_NORM = 16.0
ROPE_BASE = 10000.0
GDN_CONV_C = 2 * MIX_H * MIX_DK + MIX_H * MIX_DV
SSD_H = 8
SSD_P = 64
SSD_N = 64
SSD_G = 2
SSD_DI = SSD_H * SSD_P
SSD_CONV_C = SSD_DI + 2 * SSD_G * SSD_N
D_FF = 2816
N_EXPERTS = 8
SPLITS = (256, 256, 512, GLA_RANK, 512,
          256, 256, 512, 512,
          GDN_CONV_C, MIX_H, MIX_H, 512,
          SSD_DI, SSD_CONV_C, SSD_H,
          N_BRANCH * D_MODEL)

LANE_GLR = 0
LANE_BETA = 16
LANE_GDA = 20
LANE_SDT = 24
SMALL_W = 128

CHUNK = 64
SEQ_BLOCK_CHUNKS = 8
BATCH_SEQS = 8
ROW_TILE = 512
FF_BLOCKS = 2
VMEM_LIMIT = 56 * 1024 * 1024
PAST_LEN = 16384


def _row_tile(*counts):
    for tm in (ROW_TILE, 256, 128, 64, 32, 16):
        if all(n % tm == 0 for n in counts):
            return tm
    raise ValueError(f"row counts {counts} are not all multiples of 16")


def _rows_specs(parts, tm):
    width = parts[0].shape[1]
    if len(parts) == 1:
        return [pl.BlockSpec((tm, width), lambda i: (i, 0))]
    npt = parts[0].shape[0] // tm
    return [pl.BlockSpec((tm, width), lambda i: (jnp.minimum(i, npt - 1), 0)),
            pl.BlockSpec((tm, width), lambda i: (jnp.maximum(i - npt, 0), 0))]


def _rows_read(refs, npt):
    if len(refs) == 1:
        return refs[0][...]
    return jnp.where(pl.program_id(0) < npt, refs[0][...], refs[1][...])


def _mm(a, b):
    return jnp.dot(a, b, preferred_element_type=F32)


def _mm_nt(a, b):
    return lax.dot_general(a, b, (((1,), (1,)), ((), ())), preferred_element_type=F32)


def _mm_tn(a, b):
    return lax.dot_general(a, b, (((0,), (0,)), ((), ())), preferred_element_type=F32)


def _split(x):
    hi = x.astype(BF16)
    lo = (x - hi.astype(F32)).astype(BF16)
    return hi, lo


def _mm_xl(x, m):
    hi, lo = _split(x)
    return _mm(hi, m) + _mm(lo, m)


def _mm_xr(m, x):
    hi, lo = _split(x)
    return _mm(m, hi) + _mm(m, lo)


def _sigmoid(x):
    return 1.0 / (1.0 + jnp.exp(-x))


def _silu(x):
    return x * _sigmoid(x)


def _softplus(x):
    return jnp.maximum(x, 0.0) + jnp.log1p(jnp.exp(-jnp.abs(x)))


def _rms(x, gain):
    ms = jnp.mean(x * x, axis=-1, keepdims=True)
    return x * lax.rsqrt(ms + EPS) * gain


def _log2(n):
    k = int(round(math.log2(n)))
    assert (1 << k) == n, n
    return k


class _Masks:
    def __init__(self, c, ls):
        self.c, self.ls = c, ls
        sh = _log2(ls)
        r = lax.broadcasted_iota(jnp.int32, (c, c), 0)
        q = lax.broadcasted_iota(jnp.int32, (c, c), 1)
        same = (r >> sh) == (q >> sh)
        self.incl = jnp.logical_and(same, q <= r)
        self.strict = jnp.logical_and(same, q < r)
        self.tri = jnp.where(self.incl, 1.0, 0.0).astype(BF16)
        self.tri_t = jnp.where(jnp.logical_and(same, r <= q), 1.0, 0.0).astype(BF16)
        last =((r >> sh) << sh) + (ls - 1)
        self.lastsel = jnp.where(q == last, 1.0, 0.0).astype(BF16)
        self.eye = jnp.where(r == q, 1.0, 0.0).astype(F32)
        self.r, self.q = r, q
        rc = lax.broadcasted_iota(jnp.int32, (c, 1), 0)
        self.seq_of_row = rc >> sh
        self.t_col = (rc & (ls - 1)).astype(F32)
        self.nseq = c // ls

    def rows_of(self, b, x):
        if self.nseq == 1:
            return x
        return jnp.where(self.seq_of_row == b, x, jnp.zeros_like(x))

    def shift_matrix(self, nprev):
        c, ls = self.c, self.ls
        t = self.r & (ls - 1)
        blocks = []
        for s in (1, 2, 3):
            cur = jnp.logical_and(self.q == self.r - s, t >= s)
            prev = jnp.logical_and(self.q == self.r + (ls - s), t < s)
            cur = jnp.where(cur, 1.0, 0.0).astype(BF16)
            prev = jnp.where(prev, 1.0, 0.0).astype(BF16)
            blocks.append(jnp.concatenate([prev] * nprev + [cur], axis=1))
        return jnp.concatenate(blocks, axis=0)


def _expander(lane0, group, width):
    r = lax.broadcasted_iota(jnp.int32, (SMALL_W, width), 0)
    q = lax.broadcasted_iota(jnp.int32, (SMALL_W, width), 1)
    return jnp.where(r == lane0 + (q >> _log2(group)), 1.0, 0.0).astype(BF16)


def _causal_conv(u, prev_parts, w, masks, shift_m):
    c = masks.c
    x = jnp.concatenate(list(prev_parts) + [u], axis=0)
    y = _mm(shift_m, x)
    out = u.astype(F32) * w[3:4, :]
    for s in (1, 2, 3):
        out = out + y[(s - 1) * c:s * c, :] * w[3 - s:4 - s, :]
    return out


def _pair_decay(gc_col, gt_row, incl):
    d = gc_col - gt_row
    return jnp.where(incl, jnp.exp(jnp.where(incl, d, 0.0)), 0.0)


def _seq_prologue(s_scr, prev_scr=None):
    @pl.when(pl.program_id(1) == 0)
    def _():
        s_scr[...] = jnp.zeros_like(s_scr)
        if prev_scr is not None:
            prev_scr[...] = jnp.zeros_like(prev_scr)


def _seq_epilogue(s_scr, s1_ref):
    @pl.when(pl.program_id(1) == pl.num_programs(1) - 1)
    def _():
        s1_ref[0] = s_scr[...].reshape(s1_ref.shape[1:])


def _in_proj_kernel(*refs, nh, npt):
    h_refs, (g_ref, wa, wb, wc, wd, wg, ws, za, zb, zc, zd, zg, zs) = refs[:nh], refs[nh:]
    xn = _rms(_rows_read(h_refs, npt), g_ref[...]).astype(BF16)
    for w_ref, o_ref in ((wa, za), (wb, zb), (wc, zc), (wd, zd), (wg, zg)):
        width = o_ref.shape[1]
        for j in range(0, width, 512):
            jw = min(512, width - j)
            o_ref[:, j:j + jw] = _mm(xn, w_ref[:, j:j + jw]).astype(o_ref.dtype)
    zs[...] = _mm(xn, ws[...])


def _in_proj(h_parts, gain, wts):
    counts = [p.shape[0] for p in h_parts]
    n = sum(counts)
    tm = _row_tile(*counts)
    wa, wb, wc, wd, wg, ws = wts
    const = lambda i: (0, 0)
    row = lambda i: (i, 0)

    def wspec(w):
        return pl.BlockSpec(w.shape, const, pipeline_mode=pl.Buffered(1))

    outs = [jax.ShapeDtypeStruct((n, w.shape[1]), BF16) for w in (wa, wb, wc, wd, wg)]
    outs.append(jax.ShapeDtypeStruct((n, SMALL_W), F32))
    out_specs = [pl.BlockSpec((tm, w.shape[1]), row) for w in (wa, wb, wc, wd, wg)]
    out_specs.append(pl.BlockSpec((tm, SMALL_W), row))
    return pl.pallas_call(
        functools.partial(_in_proj_kernel, nh=len(h_parts), npt=counts[0] // tm),
        grid=(n // tm,),
        in_specs=_rows_specs(h_parts, tm) + [pl.BlockSpec((1, D_MODEL), const)] + [wspec(w) for w in wts],
        out_specs=out_specs,
        out_shape=outs,
        compiler_params=pltpu.CompilerParams(dimension_semantics=("parallel",), vmem_limit_bytes=VMEM_LIMIT),
        name="in_proj",
    )(*h_parts, gain, *wts)


class _Stack:
    def __init__(self, c, ls):
        self.c, self.ls = c, ls
        self.hs = MIX_H * c
        self.nseq = c // ls
        self.masks = _Masks(self.hs, ls)
        self.head_of_lane = lax.broadcasted_iota(jnp.int32, (1, MIX_H * MIX_DK), 1) >> _log2(MIX_DK)
        self.seq_of_row = (lax.broadcasted_iota(jnp.int32, (self.hs, 1), 0) & (c - 1)) >> _log2(ls)

    def keys(self, x):
        return jnp.concatenate(
            [jnp.where(self.head_of_lane == h, x, jnp.zeros_like(x)) for h in range(MIX_H)], axis=0)

    def values(self, x):
        return jnp.concatenate([x[:, MIX_DV * h:MIX_DV * (h + 1)] for h in range(MIX_H)], axis=0)

    def rows_of(self, b, x):
        if self.nseq == 1:
            return x
        return jnp.where(self.seq_of_row == b, x, jnp.zeros_like(x))


def _write_heads(o_ref, rows, o_stacked, c, gain, gate):
    for h in range(MIX_H):
        vl = slice(MIX_DV * h, MIX_DV * (h + 1))
        y = _rms(o_stacked[h * c:(h + 1) * c, :], gain) * _silu(gate[:, vl])
        o_ref[rows, vl] = y.astype(o_ref.dtype)


def _gla_kernel(*refs, c, cpb, ls, seq_mode):
    if seq_mode:
        z_ref, zs_ref, wgk_ref, bgk_ref, gain_ref, o_ref, s1_ref, s_scr = refs
        s0_ref = None
        _seq_prologue(s_scr)
    else:
        z_ref, zs_ref, wgk_ref, bgk_ref, gain_ref, s0_ref, o_ref, s1_ref = refs
        s_scr = None
    hdk = MIX_H * MIX_DK
    mkb = _Masks(cpb * c, ls)
    sk = _Stack(c, ls)
    r = lax.broadcasted_iota(jnp.int32, (hdk, hdk), 0)
    q = lax.broadcasted_iota(jnp.int32, (hdk, hdk), 1)
    eye_k = r == q
    ones_dv = jnp.ones((hdk, MIX_DV), BF16)
    qq = z_ref[:, 0:256].astype(F32) * (MIX_DK ** -0.5)
    kk = z_ref[:, 256:512].astype(F32)
    pre = _mm(zs_ref[...].astype(BF16), wgk_ref[...]) + bgk_ref[...]
    lg = -_softplus(-pre) * (1.0 / GLA_GATE_NORM)
    g = _mm_xr(mkb.tri, lg)
    glast = _mm_xr(mkb.lastsel, g)
    qe_all = (qq * jnp.exp(g)).astype(BF16)
    ke_all = (kk * jnp.exp(-g)).astype(BF16)
    kd_all = (kk * jnp.exp(glast - g)).astype(BF16)
    el_all = jnp.exp(glast)
    chunks = []
    for ci in range(cpb):
        rows = slice(ci * c, (ci + 1) * c)
        qe = sk.keys(qe_all[rows, :])
        ke = sk.keys(ke_all[rows, :])
        kd = sk.keys(kd_all[rows, :])
        v = sk.values(z_ref[rows, 512:1024])
        a = jnp.where(sk.masks.incl, _mm_nt(qe, ke), 0.0).astype(BF16)
        el = el_all[rows, :]
        decays = []
        for b in range(sk.nseq):
            diag = jnp.where(eye_k, jnp.broadcast_to(el[b * ls:b * ls + 1, :], (hdk, hdk)), 0.0)
            decays.append(_mm_xl(diag, ones_dv))
        chunks.append(dict(rows=rows, qe=qe, kd=kd, v=v, o=_mm(a, v), decays=decays))
    for ch in chunks:
        o = ch["o"]
        for b in range(sk.nseq):
            s = s_scr[...] if seq_mode else s0_ref[b].reshape(hdk, MIX_DV)
            o = o + _mm(sk.rows_of(b, ch["qe"]), s.astype(BF16))
            s_new = s * ch["decays"][b] + _mm_tn(sk.rows_of(b, ch["kd"]), ch["v"])
            if seq_mode:
                s_scr[...] = s_new
            else:
                s1_ref[b] = s_new.reshape(MIX_H, MIX_DK, MIX_DV)
        rows = ch["rows"]
        _write_heads(o_ref, rows, o, c, gain_ref[...], z_ref[rows, 1024:1536].astype(F32))
    if seq_mode:
        _seq_epilogue(s_scr, s1_ref)


def _ret_kernel(*refs, c, cpb, ls, seq_mode):
    if seq_mode:
        z_ref, cos_ref, sin_ref, gain_ref, o_ref, s1_ref, s_scr = refs
        s0_ref = None
        _seq_prologue(s_scr)
    else:
        z_ref, cos_ref, sin_ref, gain_ref, s0_ref, o_ref, s1_ref = refs
        s_scr = None
    hdk = MIX_H * MIX_DK
    sk = _Stack(c, ls)
    ms = sk.masks
    t_col = (lax.broadcasted_iota(jnp.int32, (cpb * c, 1), 0) & (ls - 1)).astype(F32)
    lgam =[math.log(1.0 - 2.0 ** (-5.0 - h)) for h in range(MIX_H)]

    def per_head(idx):
        out = jnp.zeros(idx.shape, F32)
        for h in range(MIX_H):
            out = jnp.where(idx == h, lgam[h], out)
        return out

    lane = lax.broadcasted_iota(jnp.int32, (1, hdk), 1)
    lg_lane = per_head(lane >> _log2(MIX_DK))
    lg_srow = per_head(lax.broadcasted_iota(jnp.int32, (sk.hs, 1), 0) >> _log2(c))
    lg_krow = per_head(lax.broadcasted_iota(jnp.int32, (hdk, 1), 0) >> _log2(MIX_DK))
    first_half = (lane & (MIX_DK - 1)) < MIX_DK // 2
    eg = jnp.exp((t_col + 1.0) * lg_lane)
    ed = jnp.exp((ls - 1.0 - t_col) * lg_lane)
    dt_pos = ((ms.r & (ls - 1)) - (ms.q & (ls - 1))).astype(F32)
    dm = jnp.where(ms.incl, jnp.exp(jnp.where(ms.incl, dt_pos * lg_srow, 0.0)), 0.0)
    dec = jnp.broadcast_to(jnp.exp(lg_krow * float(ls)), (hdk, MIX_DV))

    def rope(x, cs, sn):
        sw = jnp.where(first_half, pltpu.roll(x, hdk - MIX_DK // 2, 1), pltpu.roll(x, MIX_DK // 2, 1))
        return x * cs + sw * sn

    cs = cos_ref[...]
    sn = sin_ref[...]
    qr = rope(z_ref[:, 0:256].astype(F32), cs, sn)
    kr = rope(z_ref[:, 256:512].astype(F32), cs, sn) * (MIX_DK ** -0.5)
    qb_all = qr.astype(BF16)
    kb_all = kr.astype(BF16)
    qe_all = (qr * eg).astype(BF16)
    kd_all = (kr * ed).astype(BF16)
    chunks = []
    for ci in range(cpb):
        rows = slice(ci * c, (ci + 1) * c)
        v = sk.values(z_ref[rows, 512:1024])
        a = (_mm_nt(sk.keys(qb_all[rows, :]), sk.keys(kb_all[rows, :])) * dm).astype(BF16)
        chunks.append(dict(rows=rows, qe=sk.keys(qe_all[rows, :]), kd=sk.keys(kd_all[rows, :]),
                           v=v, o=_mm(a, v)))
    for ch in chunks:
        o = ch["o"]
        for b in range(sk.nseq):
            s = s_scr[...] if seq_mode else s0_ref[b].reshape(hdk, MIX_DV)
            o = o + _mm(sk.rows_of(b, ch["qe"]), s.astype(BF16))
            s_new = s * dec + _mm_tn(sk.rows_of(b, ch["kd"]), ch["v"])
            if seq_mode:
                s_scr[...] = s_new
            else:
                s1_ref[b] = s_new.reshape(MIX_H, MIX_DK, MIX_DV)
        rows = ch["rows"]
        _write_heads(o_ref, rows, o, c, gain_ref[...], z_ref[rows, 1024:1536].astype(F32))
    if seq_mode:
        _seq_epilogue(s_scr, s1_ref)


def _gdn_kernel(*refs, c, cpb, ls, seq_mode):
    if seq_mode:
        z_ref, zs_ref, cw_ref, prow_ref, gain_ref, o_ref, s1_ref, s_scr, prev_scr = refs
        s0_ref = prev_ref = None
        _seq_prologue(s_scr, prev_scr)
    else:
        z_ref, zs_ref, cw_ref, prow_ref, gain_ref, prev_ref, s0_ref, o_ref, s1_ref = refs
        s_scr = prev_scr = None
    hdk = MIX_H * MIX_DK
    mk = _Masks(c, ls)
    mkb = _Masks(cpb * c, ls)
    sk = _Stack(c, ls)
    hs, mks = sk.hs, sk.masks
    shift_m = mk.shift_matrix(1 if seq_mode else 2)
    r = lax.broadcasted_iota(jnp.int32, (hdk, hdk), 0)
    q = lax.broadcasted_iota(jnp.int32, (hdk, hdk), 1)
    ones_bd = jnp.where((r >> 6) == (q >> 6), 1.0, 0.0).astype(BF16)
    e_beta_k = _expander(LANE_BETA, MIX_DK, hdk)
    e_beta_v = _expander(LANE_BETA, MIX_DV, MIX_H * MIX_DV)
    e_g_k = _expander(LANE_GDA, MIX_DK, hdk)
    e_g_v = _expander(LANE_GDA, MIX_DV, MIX_H * MIX_DV)
    a_row = -jnp.exp(prow_ref[0:1, :])
    b_row = prow_ref[1:2, :]
    cw = cw_ref[...]
    lane0 = jnp.where(lax.broadcasted_iota(jnp.int32, (hs, SMALL_W), 1) == 0, 1.0, 0.0).astype(BF16)
    stack_k, stack_v, srows_of = sk.keys, sk.values, sk.rows_of

    conv = []
    for ci in range(cpb):
        u = z_ref[ci * c:(ci + 1) * c, 0:GDN_CONV_C]
        if seq_mode:
            prev_parts = [prev_scr[...].astype(BF16)]
        else:
            prev_parts = list(_split(prev_ref[...]))
        conv.append(_causal_conv(u, prev_parts, cw, mk, shift_m))
        if seq_mode:
            prev_scr[...] = u.astype(F32)
    cqkv = _silu(conv[0] if cpb == 1 else jnp.concatenate(conv, axis=0))
    cq = cqkv[:, 0:256]
    ck = cqkv[:, 256:512]
    cv = cqkv[:, 512:1024]
    qn = cq * lax.rsqrt(_mm_xl(cq * cq, ones_bd) + EPS) * (MIX_DK ** -0.5)
    kn = ck * lax.rsqrt(_mm_xl(ck * ck, ones_bd) + EPS)
    zs = zs_ref[...]
    beta = _sigmoid(zs)
    lgd = a_row * _softplus(zs + b_row)
    gc_all = _mm_xr(mkb.tri, lgd)
    glast = _mm_xr(mkb.lastsel, gc_all)
    beta_k = _mm_xl(beta, e_beta_k)
    beta_v = _mm_xl(beta, e_beta_v)
    eg_k = _mm_xl(jnp.exp(gc_all), e_g_k)
    dl_k = _mm_xl(jnp.exp(glast - gc_all), e_g_k)
    el_v_all = _mm_xl(jnp.exp(glast), e_g_v)
    kbeta = kn * beta_k
    kn_all = kn.astype(BF16)
    kbeta_all = kbeta.astype(BF16)
    qn_all = qn.astype(BF16)
    vbeta_all = (cv * beta_v).astype(BF16)
    kbe_all = (kbeta * eg_k).astype(BF16)
    qe_all = (qn * eg_k).astype(BF16)
    kd_all = (kn * dl_k).astype(BF16)

    chunks = []
    for ci in range(cpb):
        rows = slice(ci * c, (ci + 1) * c)
        gc = gc_all[rows, :]
        k_st = stack_k(kn_all[rows, :])
        g_col = jnp.concatenate([gc[:, LANE_GDA + h:LANE_GDA + h + 1] for h in range(MIX_H)], axis=0)
        g_hi, g_lo = _split(jnp.broadcast_to(g_col, (hs, SMALL_W)))
        g_row = _mm_nt(lane0, g_hi) + _mm_nt(lane0, g_lo)
        dm = _pair_decay(g_col, g_row, mks.incl)
        m = _mm_nt(stack_k(kbeta_all[rows, :]), k_st) * jnp.where(mks.strict, dm, 0.0)
        chunks.append(dict(
            rows=rows, m=m, p=mks.eye - m, mp=m,
            a=(_mm_nt(stack_k(qn_all[rows, :]), k_st) * dm).astype(BF16),
            vbeta=stack_v(vbeta_all[rows, :]),
            kbe=stack_k(kbe_all[rows, :]),
            qe=stack_k(qe_all[rows, :]),
            kd=stack_k(kd_all[rows, :]),
            el_v=el_v_all[rows, :]))
    n = 2
    while n < ls:
        for ch in chunks:
            mpb = ch["mp"].astype(BF16)
            ch["mp"] = _mm(mpb, mpb)
        for ch in chunks:
            ch["p"] = ch["p"] + _mm(ch["p"].astype(BF16), ch["mp"].astype(BF16))
        n *= 2
    for ch in chunks:
        tinv = ch["p"].astype(BF16)
        ch["uu"] = _mm(tinv, ch["vbeta"])
        ch["ww"] = _mm(tinv, ch["kbe"]).astype(BF16)
    for ch in chunks:
        rows = ch["rows"]
        states = []
        vn = ch["uu"]
        for b in range(mk.nseq):
            s = s_scr[...] if seq_mode else s0_ref[b].reshape(hdk, MIX_DV)
            states.append(s)
            vn = vn - _mm(srows_of(b, ch["ww"]), s.astype(BF16))
        vnb = vn.astype(BF16)
        o = _mm(ch["a"], vnb)
        for b in range(mk.nseq):
            s = states[b]
            o = o + _mm(srows_of(b, ch["qe"]), s.astype(BF16))
            dec = jnp.concatenate(
                [jnp.broadcast_to(ch["el_v"][b * ls:b * ls + 1, MIX_DV * h:MIX_DV * (h + 1)], (MIX_DK, MIX_DV))
                 for h in range(MIX_H)], axis=0)
            s_new = s * dec + _mm_tn(srows_of(b, ch["kd"]), vnb)
            if seq_mode:
                s_scr[...] = s_new
            else:
                s1_ref[b] = s_new.reshape(MIX_H, MIX_DK, MIX_DV)
        _write_heads(o_ref, rows, o, c, gain_ref[...], z_ref[rows, GDN_CONV_C:GDN_CONV_C + 512].astype(F32))
    if seq_mode:
        _seq_epilogue(s_scr, s1_ref)


def _ssd_kernel(*refs, c, cpb, ls, seq_mode):
    gw = SSD_DI // SSD_G
    hpg = SSD_H // SSD_G
    if seq_mode:
        z_ref, zs_ref, cw_ref, cb_ref, prow_ref, dx_ref, gain_ref, o_ref, s1_ref, s_scr, prev_scr = refs
        s0_ref = prev_ref = None
        _seq_prologue(s_scr, prev_scr)
    else:
        z_ref, zs_ref, cw_ref, cb_ref, prow_ref, dx_ref, gain_ref, prev_ref, s0_ref, o_ref, s1_ref = refs
        s_scr = prev_scr = None
    mk = _Masks(c, ls)
    mkb = _Masks(cpb * c, ls)
    shift_m = mk.shift_matrix(1 if seq_mode else 2)
    e_x = _expander(LANE_SDT, SSD_P, SSD_DI)
    a_row = -jnp.exp(prow_ref[0:1, :])
    b_row = prow_ref[1:2, :]
    cw = cw_ref[...]
    lane_g = lax.broadcasted_iota(jnp.int32, (1, gw), 1) >> 6

    def get_state(b, g):
        if seq_mode:
            return s_scr[g]
        return jnp.concatenate([s0_ref[b, g * hpg + j] for j in range(hpg)], axis=-1)

    def put_state(b, g, val):
        if seq_mode:
            s_scr[g] = val
        else:
            for j in range(hpg):
                s1_ref[b, g * hpg + j] = val[:, SSD_P * j:SSD_P * (j + 1)]

    conv = []
    for ci in range(cpb):
        u = z_ref[ci * c:(ci + 1) * c, SSD_DI:SSD_DI + SSD_CONV_C]
        if seq_mode:
            prev_parts = [prev_scr[...].astype(BF16)]
        else:
            prev_parts = list(_split(prev_ref[...]))
        conv.append(_causal_conv(u, prev_parts, cw, mk, shift_m))
        if seq_mode:
            prev_scr[...] = u.astype(F32)
    xbc = _silu((conv[0] if cpb == 1 else jnp.concatenate(conv, axis=0)) + cb_ref[...])
    sx_all = xbc[:, 0:SSD_DI]
    sb_all = xbc[:, SSD_DI:SSD_DI + SSD_G * SSD_N].astype(BF16)
    sc_all = xbc[:, SSD_DI + SSD_G * SSD_N:SSD_CONV_C].astype(BF16)
    dt = _softplus(zs_ref[...] + b_row)
    lsd_all = dt * a_row
    gc_all = _mm_xr(mkb.tri, lsd_all)
    glast = _mm_xr(mkb.lastsel, gc_all)
    eg_all = _mm_xl(jnp.exp(gc_all), e_x)
    el_all = _mm_xl(jnp.exp(glast), e_x)
    v_all = sx_all * _mm_xl(dt, e_x)
    vd_all = (v_all * _mm_xl(jnp.exp(glast - gc_all), e_x)).astype(BF16)
    skip_all = sx_all * dx_ref[...]
    gate_all = _silu(z_ref[:, 0:SSD_DI].astype(F32))

    for ci in range(cpb):
        rows = slice(ci * c, (ci + 1) * c)
        gc = gc_all[rows, :]
        lsd_hi, lsd_lo = _split(lsd_all[rows, :])
        gt = _mm_tn(lsd_hi, mk.tri_t) + _mm_tn(lsd_lo, mk.tri_t)
        for g in range(SSD_G):
            gl = slice(gw * g, gw * (g + 1))
            cg = sc_all[rows, SSD_N * g:SSD_N * (g + 1)]
            bg = sb_all[rows, SSD_N * g:SSD_N * (g + 1)]
            ag = _mm_nt(cg, bg)
            vg = v_all[rows, gl]
            o = jnp.zeros((c, gw), F32)
            for j in range(hpg):
                lane = LANE_SDT + g * hpg + j
                dm = _pair_decay(gc[:, lane:lane + 1], gt[lane:lane + 1, :], mk.incl)
                vj = jnp.where(lane_g == j, vg, 0.0).astype(BF16)
                o = o + _mm((ag * dm).astype(BF16), vj)
            o_state = jnp.zeros((c, gw), F32)
            for b in range(mk.nseq):
                s = get_state(b, g)
                o_state = o_state + _mm(mk.rows_of(b, cg), s.astype(BF16))
                dec = el_all[ci * c + b * ls:ci * c + b * ls + 1, gl]
                put_state(b, g, s * dec + _mm_tn(mk.rows_of(b, bg), vd_all[rows, gl]))
            y = (o + o_state * eg_all[rows, gl] + skip_all[rows, gl]) * gate_all[rows, gl]
            o_ref[rows, gl] = _rms(y, gain_ref[:, gl]).astype(o_ref.dtype)
    if seq_mode:
        @pl.when(pl.program_id(1) == pl.num_programs(1) - 1)
        def _():
            for g in range(SSD_G):
                for j in range(hpg):
                    s1_ref[0, g * hpg + j] = s_scr[g][:, SSD_P * j:SSD_P * (j + 1)]


def _full_spec(a, grid_rank):
    nd = a.ndim
    if grid_rank == 1:
        return pl.BlockSpec(a.shape, lambda i: (0,) * nd)
    return pl.BlockSpec(a.shape, lambda i, j: (0,) * nd)


def _mixer_seq(kernel_fn, name, z, extra_rows, params, nbatch, seqlen, state_shape, scratch):
    c, cpb = CHUNK, SEQ_BLOCK_CHUNKS
    rblk = c * cpb
    nblk = seqlen // rblk
    rowmap = lambda b, j: (b * nblk + j, 0)
    in_specs = [pl.BlockSpec((rblk, z.shape[1]), rowmap)]
    args = [z]
    for a, kind in extra_rows:
        if kind == "rows":
            in_specs.append(pl.BlockSpec((rblk, a.shape[1]), rowmap))
        else:
            in_specs.append(pl.BlockSpec((rblk, a.shape[1]), lambda b, j: (j, 0)))
        args.append(a)
    for a in params:
        in_specs.append(_full_spec(a, 2))
        args.append(a)
    sblk = (1,) + state_shape
    out_shape = [jax.ShapeDtypeStruct((nbatch * seqlen, BRANCH_W), BF16),
                 jax.ShapeDtypeStruct((nbatch,) + state_shape, F32)]
    out_specs = [pl.BlockSpec((rblk, BRANCH_W), rowmap),
                 pl.BlockSpec(sblk, lambda b, j: (b,) + (0,) * len(state_shape))]
    return pl.pallas_call(
        functools.partial(kernel_fn, c=c, cpb=cpb, ls=c, seq_mode=True),
        grid=(nbatch, nblk),
        in_specs=in_specs,
        out_specs=out_specs,
        out_shape=out_shape,
        scratch_shapes=scratch,
        compiler_params=pltpu.CompilerParams(dimension_semantics=("parallel", "arbitrary"),
                                             vmem_limit_bytes=VMEM_LIMIT),
        name=name,
    )(*args)


def _mixer_batch(kernel_fn, name, z, row_off, extra_rows, params, per_seq, nbatch, seqlen, state_shape):
    sb = BATCH_SEQS
    c = sb * seqlen
    nsteps = nbatch // sb
    off = row_off // c
    rowmap = lambda i: (off + i, 0)
    in_specs = [pl.BlockSpec((c, z.shape[1]), rowmap)]
    args = [z]
    for a, kind in extra_rows:
        if kind == "rows":
            in_specs.append(pl.BlockSpec((c, a.shape[1]), rowmap))
        else:
            in_specs.append(pl.BlockSpec((c, a.shape[1]), lambda i: (0, 0)))
        args.append(a)
    for a in params:
        in_specs.append(_full_spec(a, 1))
        args.append(a)
    for a in per_seq:
        if a.ndim == 2:
            in_specs.append(pl.BlockSpec((c, a.shape[1]), lambda i: (i, 0)))
        else:
            in_specs.append(pl.BlockSpec((sb,) + a.shape[1:], lambda i: (i,) + (0,) * (a.ndim - 1)))
        args.append(a)
    out_shape = [jax.ShapeDtypeStruct((nbatch * seqlen, BRANCH_W), BF16),
                 jax.ShapeDtypeStruct((nbatch,) + state_shape, F32)]
    out_specs = [pl.BlockSpec((c, BRANCH_W), lambda i: (i, 0)),
                 pl.BlockSpec((sb,) + state_shape, lambda i: (i,) + (0,) * len(state_shape))]
    return pl.pallas_call(
        functools.partial(kernel_fn, c=c, cpb=1, ls=seqlen, seq_mode=False),
        grid=(nsteps,),
        in_specs=in_specs,
        out_specs=out_specs,
        out_shape=out_shape,
        compiler_params=pltpu.CompilerParams(dimension_semantics=("parallel",), vmem_limit_bytes=VMEM_LIMIT),
        name=name,
    )(*args)


def _merge_kernel(*refs, nh, npt):
    branch_refs, zg = refs[:2 * N_BRANCH], refs[2 * N_BRANCH]
    h_refs, (wbr, wout, out_ref) = refs[2 * N_BRANCH + 1:2 * N_BRANCH + 1 + nh], refs[2 * N_BRANCH + 1 + nh:]
    acc = jnp.zeros(out_ref.shape, F32)
    for n in range(N_BRANCH):
        gate = _sigmoid(zg[:, n * D_MODEL:(n + 1) * D_MODEL].astype(F32))
        acc = acc + gate * _mm(_rows_read(branch_refs[2 * n:2 * n + 2], npt), wbr[n])
    out_ref[...] = _rows_read(h_refs, npt) + _mm(acc.astype(BF16), wout[...])


def _merge(branches, zg, h_parts, wbr, wout):
    n_p, n_s = branches[0][0].shape[0], branches[0][1].shape[0]
    n = n_p + n_s
    tm = _row_tile(n_p, n_s)
    row = lambda i: (i, 0)
    in_specs, args = [], []
    for pair in branches:
        in_specs += _rows_specs(pair, tm)
        args += list(pair)
    in_specs.append(pl.BlockSpec((tm, N_BRANCH * D_MODEL), row))
    in_specs += _rows_specs(h_parts, tm)
    in_specs += [pl.BlockSpec(wbr.shape, lambda i: (0, 0, 0)), pl.BlockSpec(wout.shape, lambda i: (0, 0))]
    return pl.pallas_call(
        functools.partial(_merge_kernel, nh=len(h_parts), npt=n_p // tm),
        grid=(n // tm,),
        in_specs=in_specs,
        out_specs=pl.BlockSpec((tm, D_MODEL), row),
        out_shape=jax.ShapeDtypeStruct((n, D_MODEL), F32),
        compiler_params=pltpu.CompilerParams(dimension_semantics=("parallel",), vmem_limit_bytes=VMEM_LIMIT),
        name="merge",
    )(*args, zg, *h_parts, wbr, wout)


def _ffn_kernel(h_ref, g_ref, wg_ref, wu_ref, wd_ref, out_ref, u_scr, acc_scr):
    f = pl.program_id(1)

    @pl.when(f == 0)
    def _():
        u_scr[...] = _rms(h_ref[...], g_ref[...]).astype(BF16)
        acc_scr[...] = jnp.zeros_like(acc_scr)

    u = u_scr[...]
    a = (_silu(_mm(u, wg_ref[...])) * _mm(u, wu_ref[...])).astype(BF16)
    acc_scr[...] += _mm(a, wd_ref[...])

    @pl.when(f == pl.num_programs(1) - 1)
    def _():
        out_ref[...] = h_ref[...] + acc_scr[...]


def _ffn(h, gain, wg, wu, wd):
    n = h.shape[0]
    tm = _row_tile(n)
    tf = D_FF // FF_BLOCKS
    row = lambda i, f: (i, 0)
    in_specs = [pl.BlockSpec((tm, D_MODEL), row), pl.BlockSpec((1, D_MODEL), lambda i, f: (0, 0)),
                pl.BlockSpec((D_MODEL, tf), lambda i, f: (0, f)),
                pl.BlockSpec((D_MODEL, tf), lambda i, f: (0, f)),
                pl.BlockSpec((tf, D_MODEL), lambda i, f: (f, 0))]
    return pl.pallas_call(
        _ffn_kernel,
        grid=(n // tm, FF_BLOCKS),
        in_specs=in_specs,
        out_specs=pl.BlockSpec((tm, D_MODEL), row),
        out_shape=jax.ShapeDtypeStruct((n, D_MODEL), F32),
        scratch_shapes=[pltpu.VMEM((tm, D_MODEL), BF16), pltpu.VMEM((tm, D_MODEL), F32)],
        compiler_params=pltpu.CompilerParams(dimension_semantics=("parallel", "arbitrary"),
                                             vmem_limit_bytes=VMEM_LIMIT),
        name="dense_ffn",
    )(h, gain, wg, wu, wd)


MOE_TOKENS = 768
MOE_CAP = 224


def _moe_kernel(h_ref, g_ref, rt_ref, wg_ref, wu_ref, wd_ref, out_ref,
                u_scr, acc_scr, w_scr, sel_scr, rank_scr, selt_scr, rankt_scr, xc_scr, yc_scr, cnt_scr):
    tt = h_ref.shape[0]
    cap = MOE_CAP
    e = pl.program_id(1)
    f = pl.program_id(2)
    nf = pl.num_programs(2)

    @pl.when(jnp.logical_and(e == 0, f == 0))
    def _():
        u = _rms(h_ref[...], g_ref[...])
        u_scr[...] = u.astype(BF16)
        acc_scr[...] = jnp.zeros_like(acc_scr)
        uh, ul = _split(u)
        rh, rl = _split(rt_ref[...])
        logits = _mm(uh, rh) + (_mm(uh, rl) + _mm(ul, rh))
        lane = lax.broadcasted_iota(jnp.int32, logits.shape, 1).astype(F32)
        neg = -3.0e38
        lg = jnp.where(lane < N_EXPERTS, logits, neg)
        m1 = jnp.max(lg, axis=-1, keepdims=True)
        i1 = jnp.min(jnp.where(lg == m1, lane, float(SMALL_W)), axis=-1, keepdims=True)
        lg2 = jnp.where(lane == i1, neg, lg)
        m2 = jnp.max(lg2, axis=-1, keepdims=True)
        i2 = jnp.min(jnp.where(lg2 == m2, lane, float(SMALL_W)), axis=-1, keepdims=True)
        e2 = jnp.exp(m2 - m1)
        w_scr[...] = jnp.where(lane == i1, 1.0 / (1.0 + e2), 0.0) + jnp.where(lane == i2, e2 / (1.0 + e2), 0.0)
        sel = jnp.where(jnp.logical_or(lane == i1, lane == i2), 1.0, 0.0)
        sel_scr[...] = sel
        selb = sel.astype(BF16)
        r = lax.broadcasted_iota(jnp.int32, (tt, tt), 0)
        q = lax.broadcasted_iota(jnp.int32, (tt, tt), 1)
        before = jnp.where(r < q, 1.0, 0.0).astype(BF16)
        ident = jnp.where(r == q, 1.0, 0.0).astype(BF16)
        rank_scr[...] = _mm_tn(before, selb)
        rankt_scr[...] = _mm_tn(selb, before)
        selt_scr[...] = _mm_tn(selb, ident)
        cnt = jnp.sum(sel, axis=0, keepdims=True)
        for ee in range(N_EXPERTS):
            cnt_scr[ee] = jnp.sum(jnp.where(lane[0:1, :] == float(ee), cnt, 0.0)).astype(jnp.int32)

    nsub = (cnt_scr[e] + (cap - 1)) // cap
    lane_w = lax.broadcasted_iota(jnp.int32, (tt, SMALL_W), 1)

    def col_of(ref):
        return jnp.sum(jnp.where(lane_w == e, ref[...], 0.0), axis=-1, keepdims=True)

    @pl.when(f == 0)
    def _():
        rank_row = rankt_scr[pl.ds(e, 1), :]
        sel_row = selt_scr[pl.ds(e, 1), :]
        jcol = lax.broadcasted_iota(jnp.int32, (cap, 1), 0).astype(F32)

        def gather(s, carry):
            base = (s * cap).astype(F32)
            hit = jnp.logical_and(rank_row == jcol + base, sel_row > 0.5)
            onehot = jnp.where(hit, 1.0, 0.0).astype(BF16)
            off = pl.multiple_of(s * cap, 16)
            xc_scr[pl.ds(off, cap), :] = _mm(onehot, u_scr[...]).astype(BF16)
            return carry

        lax.fori_loop(0, nsub, gather, 0)

    def expert(s, carry):
        off = pl.multiple_of(s * cap, 16)
        x = xc_scr[pl.ds(off, cap), :]
        a = (_silu(_mm(x, wg_ref[0])) * _mm(x, wu_ref[0])).astype(BF16)
        y = _mm(a, wd_ref[0])

        @pl.when(f == 0)
        def _():
            yc_scr[pl.ds(off, cap), :] = y

        @pl.when(f != 0)
        def _():
            yc_scr[pl.ds(off, cap), :] += y

        return carry

    lax.fori_loop(0, nsub, expert, 0)

    @pl.when(f == nf - 1)
    def _():
        rank_col = col_of(rank_scr)
        sel_col = col_of(sel_scr)
        w_col = col_of(w_scr)
        jrow = lax.broadcasted_iota(jnp.int32, (1, cap), 1).astype(F32)

        def scatter(s, carry):
            base = (s * cap).astype(F32)
            hit = jnp.logical_and(rank_col == jrow + base, sel_col > 0.5)
            onehot = jnp.where(hit, 1.0, 0.0).astype(BF16)
            off = pl.multiple_of(s * cap, 16)
            acc_scr[...] += w_col * _mm(onehot, yc_scr[pl.ds(off, cap), :].astype(BF16))
            return carry

        lax.fori_loop(0, nsub, scatter, 0)

    @pl.when(jnp.logical_and(e == pl.num_programs(1) - 1, f == nf - 1))
    def _():
        out_ref[...] = h_ref[...] + acc_scr[...]


def _moe(h, gain, router, wg, wu, wd):
    n = h.shape[0]
    tt = MOE_TOKENS if n % MOE_TOKENS == 0 else _row_tile(n)
    ne = wg.shape[0]
    tf = D_FF // FF_BLOCKS
    row = lambda i, e, f: (i, 0)
    nrows = pl.cdiv(tt, MOE_CAP) * MOE_CAP
    in_specs = [pl.BlockSpec((tt, D_MODEL), row), pl.BlockSpec((1, D_MODEL), lambda i, e, f: (0, 0)),
                pl.BlockSpec(router.shape, lambda i, e, f: (0, 0)),
                pl.BlockSpec((1, D_MODEL, tf), lambda i, e, f: (e, 0, f)),
                pl.BlockSpec((1, D_MODEL, tf), lambda i, e, f: (e, 0, f)),
                pl.BlockSpec((1, tf, D_MODEL), lambda i, e, f: (e, f, 0))]
    scratch = [pltpu.VMEM((tt, D_MODEL), BF16), pltpu.VMEM((tt, D_MODEL), F32),
               pltpu.VMEM((tt, SMALL_W), F32), pltpu.VMEM((tt, SMALL_W), F32), pltpu.VMEM((tt, SMALL_W), F32),
               pltpu.VMEM((SMALL_W, tt), F32), pltpu.VMEM((SMALL_W, tt), F32),
               pltpu.VMEM((nrows, D_MODEL), BF16), pltpu.VMEM((nrows, D_MODEL), F32),
               pltpu.SMEM((N_EXPERTS,), jnp.int32)]
    return pl.pallas_call(
        _moe_kernel,
        grid=(n // tt, ne, FF_BLOCKS),
        in_specs=in_specs,
        out_specs=pl.BlockSpec((tt, D_MODEL), row),
        out_shape=jax.ShapeDtypeStruct((n, D_MODEL), F32),
        scratch_shapes=scratch,
        compiler_params=pltpu.CompilerParams(dimension_semantics=("parallel", "arbitrary", "arbitrary"),
                                             vmem_limit_bytes=VMEM_LIMIT),
        name="moe",
    )(h, gain, router, wg, wu, wd)


def _ple_kernel(*refs, final, npt):
    if final:
        h_ref, pp_ref, ps_ref, g_ref, wgate, wproj, gf_ref, yp_ref, ys_ref = refs
    else:
        h_ref, pp_ref, ps_ref, g_ref, wgate, wproj, out_ref = refs
    h = h_ref[...]
    gate = _sigmoid(_mm(_rms(h, g_ref[...]).astype(BF16), wgate[...]))
    h = h + gate * _mm(_rows_read((pp_ref, ps_ref), npt).astype(BF16), wproj[...])
    if final:
        y = _rms(h, gf_ref[...])

        @pl.when(pl.program_id(0) < npt)
        def _():
            yp_ref[...] = y

        @pl.when(pl.program_id(0) >= npt)
        def _():
            ys_ref[...] = y
    else:
        out_ref[...] = h


def _ple(h, p_parts, gain, wgate, wproj, gain_final=None):
    n_p, n_s = p_parts[0].shape[0], p_parts[1].shape[0]
    n = h.shape[0]
    tm = _row_tile(n_p, n_s)
    final = gain_final is not None
    row = lambda i: (i, 0)
    const = lambda i: (0, 0)
    in_specs = [pl.BlockSpec((tm, D_MODEL), row)] + _rows_specs(p_parts, tm)
    in_specs += [pl.BlockSpec((1, D_MODEL), const), pl.BlockSpec(wgate.shape, const), pl.BlockSpec(wproj.shape, const)]
    args = [h, *p_parts, gain, wgate, wproj]
    if final:
        in_specs.append(pl.BlockSpec((1, D_MODEL), const))
        args.append(gain_final)
        out_shape = [jax.ShapeDtypeStruct((n_p, D_MODEL), F32), jax.ShapeDtypeStruct((n_s, D_MODEL), F32)]
        out_specs = _rows_specs(out_shape, tm)
    else:
        out_shape = jax.ShapeDtypeStruct((n, D_MODEL), F32)
        out_specs = pl.BlockSpec((tm, D_MODEL), row)
    return pl.pallas_call(
        functools.partial(_ple_kernel, final=final, npt=n_p // tm),
        grid=(n // tm,),
        in_specs=in_specs,
        out_specs=out_specs,
        out_shape=out_shape,
        compiler_params=pltpu.CompilerParams(dimension_semantics=("arbitrary",), vmem_limit_bytes=VMEM_LIMIT),
        name="ple",
    )(*args)


def _pad_lanes(x, lane0):
    w = x.shape[-1]
    pad = [(0, 0)] * (x.ndim - 1) + [(lane0, SMALL_W - lane0 - w)]
    return jnp.pad(x, pad)


def _in_proj_weights(w):
    cuts = np.cumsum(SPLITS)[:-1].tolist()
    (gq, gk, gv, glr, gg, rq, rk, rv, rg, dqkv, db, da, dg, sz, sxbc, sdt, mg) = jnp.split(w, cuts, axis=-1)
    cat = lambda *xs: jnp.concatenate(xs, axis=-1).astype(BF16)
    small = jnp.concatenate([glr, db, da, sdt], axis=-1)
    small = jnp.pad(small, ((0, 0), (0, SMALL_W - small.shape[1]))).astype(BF16)
    return (cat(gq, gk, gv, gg), cat(rq, rk, rv, rg), cat(dqkv, dg), cat(sz, sxbc), mg.astype(BF16),
            small)


def _rope_tables(pos):
    half = MIX_DK // 2
    inv = ROPE_BASE ** (-jnp.arange(half, dtype=F32) / half)
    ang = jnp.asarray(pos).astype(F32)[:, None] * inv[None, :]
    cos, sin = jnp.cos(ang), jnp.sin(ang)
    cos_t = jnp.tile(jnp.concatenate([cos, cos], axis=-1), (1, MIX_H))
    sin_t = jnp.tile(jnp.concatenate([-sin, sin], axis=-1), (1, MIX_H))
    return cos_t, sin_t


def _conv_prev(state_conv):
    b, _, cdim = state_conv.shape
    return jnp.pad(state_conv, ((0, 0), (1, 0), (0, 0))).reshape(b * CONV_W, cdim)


def kernel(x_prompt, x_sample, state_gla, state_ret, state_gdn, state_gdn_conv, state_ssd, state_ssd_conv, p_prompt, p_sample, norm_mix, w_in, gla_w_gk, gla_b_gk, gla_norm, ret_norm, gdn_conv_w, gdn_a_log, gdn_dt_bias, gdn_norm, ssd_conv_w, ssd_conv_b, ssd_a_log, ssd_dt_bias, ssd_d, ssd_norm, w_branch, w_out, norm_ffn, ffn_w_gate, ffn_w_up, ffn_w_down, moe_router, moe_w_gate, moe_w_up, moe_w_down, norm_ple, ple_w_gate, ple_w_proj, norm_final):
    bp, lp, _ = x_prompt.shape
    bs, lsmp, _ = x_sample.shape
    depth = w_in.shape[0]
    n_p = bp * lp
    n_s = bs * lsmp
    h_parts = (x_prompt.reshape(n_p, D_MODEL), x_sample.reshape(n_s, D_MODEL))
    cos_p, sin_p = _rope_tables(np.arange(lp))
    cos_s, sin_s = _rope_tables(PAST_LEN + (np.arange(BATCH_SEQS * lsmp) % lsmp))
    mix_state = (MIX_H, MIX_DK, MIX_DV)
    stacked_state = (MIX_H * MIX_DK, MIX_DV)
    ssd_state = (SSD_H, SSD_N, SSD_P)

    outs_p = [[] for _ in range(6)]
    outs_s = [[] for _ in range(6)]
    for i in range(depth):
        wts = _in_proj_weights(w_in[i])
        za, zb, zc, zd, zg, zs = _in_proj(h_parts, norm_mix[i][None, :], wts)

        wgk = jnp.pad(gla_w_gk[i], ((LANE_GLR, SMALL_W - LANE_GLR - GLA_RANK), (0, 0))).astype(BF16)
        bgk = gla_b_gk[i][None, :]
        gla_par = [wgk, bgk, gla_norm[i][None, :]]
        oa_p, gla_p = _mixer_seq(_gla_kernel, "gla_seq", za, [(zs, "rows")], gla_par, bp, lp, mix_state,
                                 [pltpu.VMEM(stacked_state, F32)])
        oa_s, gla_s = _mixer_batch(_gla_kernel, "gla_batch", za, n_p, [(zs, "rows")], gla_par,
                                   [state_gla[i]], bs, lsmp, mix_state)
        ret_par = [ret_norm[i][None, :]]
        ob_p, ret_p = _mixer_seq(_ret_kernel, "ret_seq", zb, [(cos_p, "pos"), (sin_p, "pos")], ret_par,
                                 bp, lp, mix_state, [pltpu.VMEM(stacked_state, F32)])
        ob_s, ret_s = _mixer_batch(_ret_kernel, "ret_batch", zb, n_p, [(cos_s, "pos"), (sin_s, "pos")],
                                   ret_par, [state_ret[i]], bs, lsmp, mix_state)
        gdn_prow = jnp.concatenate([_pad_lanes(gdn_a_log[i][None, :], LANE_GDA),
                                    _pad_lanes(gdn_dt_bias[i][None, :], LANE_GDA)], axis=0)
        gdn_par = [gdn_conv_w[i], gdn_prow, gdn_norm[i][None, :]]
        oc_p, gdn_p = _mixer_seq(_gdn_kernel, "gdn_seq", zc, [(zs, "rows")], gdn_par, bp, lp, mix_state,
                                 [pltpu.VMEM(stacked_state, F32), pltpu.VMEM((CHUNK, GDN_CONV_C), F32)])
        oc_s, gdn_s = _mixer_batch(_gdn_kernel, "gdn_batch", zc, n_p, [(zs, "rows")], gdn_par,
                                   [_conv_prev(state_gdn_conv[i]), state_gdn[i]], bs, lsmp, mix_state)
        ssd_prow = jnp.concatenate([_pad_lanes(ssd_a_log[i][None, :], LANE_SDT),
                                    _pad_lanes(ssd_dt_bias[i][None, :], LANE_SDT)], axis=0)
        ssd_par = [ssd_conv_w[i], ssd_conv_b[i][None, :], ssd_prow,
                   jnp.repeat(ssd_d[i], SSD_P)[None, :], ssd_norm[i][None, :]]
        ssd_scr = [pltpu.VMEM((SSD_G, SSD_N, SSD_DI // SSD_G), F32), pltpu.VMEM((CHUNK, SSD_CONV_C), F32)]
        od_p, ssd_p = _mixer_seq(_ssd_kernel, "ssd_seq", zd, [(zs, "rows")], ssd_par,
                                 bp, lp, ssd_state, ssd_scr)
        od_s, ssd_s = _mixer_batch(_ssd_kernel, "ssd_batch", zd, n_p, [(zs, "rows")], ssd_par,
                                   [_conv_prev(state_ssd_conv[i]), state_ssd[i]], bs, lsmp, ssd_state)

        tail = lp - (CONV_W - 1)
        gdnc_p = zc[:n_p].reshape(bp, lp, -1)[:, tail:, :GDN_CONV_C].astype(F32)
        ssdc_p = zd[:n_p].reshape(bp, lp, -1)[:, tail:, SSD_DI:].astype(F32)
        zc_s = zc[n_p:].reshape(bs, lsmp, -1)[:, :, :GDN_CONV_C].astype(F32)
        zd_s = zd[n_p:].reshape(bs, lsmp, -1)[:, :, SSD_DI:].astype(F32)
        gdnc_s = jnp.concatenate([state_gdn_conv[i], zc_s], axis=1)[:, lsmp:]
        ssdc_s = jnp.concatenate([state_ssd_conv[i], zd_s], axis=1)[:, lsmp:]
        for lst, val in zip(outs_p, (gla_p, ret_p, gdn_p, gdnc_p, ssd_p, ssdc_p)):
            lst.append(val)
        for lst, val in zip(outs_s, (gla_s, ret_s, gdn_s, gdnc_s, ssd_s, ssdc_s)):
            lst.append(val)

        branches = ((oa_p, oa_s), (ob_p, ob_s), (oc_p, oc_s), (od_p, od_s))
        h = _merge(branches, zg, h_parts, w_branch[i].astype(BF16), w_out[i].astype(BF16))

        j = i // 2
        if i % 2 == 0:
            h = _ffn(h, norm_ffn[i][None, :], ffn_w_gate[j].astype(BF16), ffn_w_up[j].astype(BF16),
                     ffn_w_down[j].astype(BF16))
        else:
            router = jnp.pad(moe_router[j], ((0, 0), (0, SMALL_W - N_EXPERTS)))
            h = _moe(h, norm_ffn[i][None, :], router, moe_w_gate[j].astype(BF16), moe_w_up[j].astype(BF16),
                     moe_w_down[j].astype(BF16))
        p_parts = (p_prompt[i].reshape(n_p, D_PLE), p_sample[i].reshape(n_s, D_PLE))
        h = _ple(h, p_parts, norm_ple[i][None, :], ple_w_gate[i].astype(BF16), ple_w_proj[i].astype(BF16),
                 gain_final=norm_final[None, :] if i == depth - 1 else None)
        h_parts = (h,)

    y_prompt = h[0].reshape(bp, lp, D_MODEL)
    y_sample = h[1].reshape(bs, lsmp, D_MODEL)
    return (y_prompt, y_sample) + tuple(jnp.stack(l) for l in outs_p) + tuple(jnp.stack(l) for l in outs_s)
```

```python
import functools
import math

import numpy as np
import jax
import jax.numpy as jnp
from jax import lax
from jax.experimental import pallas as pl
from jax.experimental.pallas import tpu as pltpu

F32 = jnp.float32
BF16 = jnp.bfloat16
EPS = 1e-6

D_MODEL = 1024
D_PLE = 256
CONV_W = 4
N_BRANCH = 4
BRANCH_W = 512
MIX_H = 4
MIX_DK = 64
MIX_DV = 128
GLA_RANK = 16
GLA_GATE_NORM = 16.0
ROPE_BASE = 10000.0
GDN_CONV_C = 2 * MIX_H * MIX_DK + MIX_H * MIX_DV
SSD_H = 8
SSD_P = 64
SSD_N = 64
SSD_G = 2
SSD_DI = SSD_H * SSD_P
SSD_CONV_C = SSD_DI + 2 * SSD_G * SSD_N
D_FF = 2816
N_EXPERTS = 8
SPLITS = (256, 256, 512, GLA_RANK, 512,
          256, 256, 512, 512,
          GDN_CONV_C, MIX_H, MIX_H, 512,
          SSD_DI, SSD_CONV_C, SSD_H,
          N_BRANCH * D_MODEL)

LANE_GLR = 0
LANE_BETA = 16
LANE_GDA = 20
LANE_SDT = 24
SMALL_W = 128

CHUNK = 64
SEQ_BLOCK_CHUNKS = 8
BATCH_SEQS = 8
ROW_TILE = 512
FF_BLOCKS = 2
VMEM_LIMIT = 56 * 1024 * 1024
PAST_LEN = 16384


def _row_tile(*counts):
    for tm in (ROW_TILE, 256, 128, 64, 32, 16):
        if all(n % tm == 0 for n in counts):
            return tm
    raise ValueError(f"row counts {counts} are not all multiples of 16")


def _rows_specs(parts, tm, layer=None):
    lead = () if layer is None else (layer,)
    blk = (tm, parts[0].shape[-1]) if layer is None else (None, tm, parts[0].shape[-1])
    if len(parts) == 1:
        return [pl.BlockSpec(blk, lambda i: lead + (i, 0))]
    npt = parts[0].shape[-2] // tm
    return [pl.BlockSpec(blk, lambda i: lead + (jnp.minimum(i, npt - 1), 0)),
            pl.BlockSpec(blk, lambda i: lead + (jnp.maximum(i - npt, 0), 0))]


def _rows_read(refs, npt):
    if len(refs) == 1:
        return refs[0][...]
    return jnp.where(pl.program_id(0) < npt, refs[0][...], refs[1][...])


def _mm(a, b):
    return jnp.dot(a, b, preferred_element_type=F32)


def _mm_nt(a, b):
    return lax.dot_general(a, b, (((1,), (1,)), ((), ())), preferred_element_type=F32)


def _mm_tn(a, b):
    return lax.dot_general(a, b, (((0,), (0,)), ((), ())), preferred_element_type=F32)


def _split(x):
    hi = x.astype(BF16)
    lo = (x - hi.astype(F32)).astype(BF16)
    return hi, lo


def _mm_xl(x, m):
    hi, lo = _split(x)
    return _mm(hi, m) + _mm(lo, m)


def _mm_xr(m, x):
    hi, lo = _split(x)
    return _mm(m, hi) + _mm(m, lo)


def _sigmoid(x):
    return 1.0 / (1.0 + jnp.exp(-x))


def _silu(x):
    return x * _sigmoid(x)


def _softplus(x):
    return jnp.maximum(x, 0.0) + jnp.log1p(jnp.exp(-jnp.abs(x)))


def _rms(x, gain):
    ms = jnp.mean(x * x, axis=-1, keepdims=True)
    return x * lax.rsqrt(ms + EPS) * gain


def _log2(n):
    k = int(round(math.log2(n)))
    assert (1 << k) == n, n
    return k


class _Masks:
    def __init__(self, c, ls):
        self.c, self.ls = c, ls
        sh = _log2(ls)
        r = lax.broadcasted_iota(jnp.int32, (c, c), 0)
        q = lax.broadcasted_iota(jnp.int32, (c, c), 1)
        same = (r >> sh) == (q >> sh)
        self.incl = jnp.logical_and(same, q <= r)
        self.strict = jnp.logical_and(same, q < r)
        self.tri = jnp.where(self.incl, 1.0, 0.0).astype(BF16)
        self.tri_t = jnp.where(jnp.logical_and(same, r <= q), 1.0, 0.0).astype(BF16)
        last =((r >> sh) << sh) + (ls - 1)
        self.lastsel = jnp.where(q == last, 1.0, 0.0).astype(BF16)
        self.eye = jnp.where(r == q, 1.0, 0.0).astype(F32)
        self.r, self.q = r, q
        rc = lax.broadcasted_iota(jnp.int32, (c, 1), 0)
        self.seq_of_row = rc >> sh
        self.t_col = (rc & (ls - 1)).astype(F32)
        self.nseq = c // ls

    def rows_of(self, b, x):
        if self.nseq == 1:
            return x
        return jnp.where(self.seq_of_row == b, x, jnp.zeros_like(x))

    def shift_matrix(self, nprev):
        c, ls = self.c, self.ls
        t = self.r & (ls - 1)
        blocks = []
        for s in (1, 2, 3):
            cur = jnp.logical_and(self.q == self.r - s, t >= s)
            prev = jnp.logical_and(self.q == self.r + (ls - s), t < s)
            cur = jnp.where(cur, 1.0, 0.0).astype(BF16)
            prev = jnp.where(prev, 1.0, 0.0).astype(BF16)
            blocks.append(jnp.concatenate([prev] * nprev + [cur], axis=1))
        return jnp.concatenate(blocks, axis=0)


def _expander(lane0, group, width):
    r = lax.broadcasted_iota(jnp.int32, (SMALL_W, width), 0)
    q = lax.broadcasted_iota(jnp.int32, (SMALL_W, width), 1)
    return jnp.where(r == lane0 + (q >> _log2(group)), 1.0, 0.0).astype(BF16)


def _causal_conv(u, prev_parts, w, masks, shift_m):
    c = masks.c
    x = jnp.concatenate(list(prev_parts) + [u], axis=0)
    y = _mm(shift_m, x)
    out = u.astype(F32) * w[3:4, :]
    for s in (1, 2, 3):
        out = out + y[(s - 1) * c:s * c, :] * w[3 - s:4 - s, :]
    return out


def _pair_decay(gc_col, gt_row, incl):
    d = gc_col - gt_row
    return jnp.where(incl, jnp.exp(jnp.where(incl, d, 0.0)), 0.0)


def _seq_prologue(s_scr, prev_scr=None):
    @pl.when(pl.program_id(1) == 0)
    def _():
        s_scr[...] = jnp.zeros_like(s_scr)
        if prev_scr is not None:
            prev_scr[...] = jnp.zeros_like(prev_scr)


def _seq_epilogue(s_scr, s1_ref):
    @pl.when(pl.program_id(1) == pl.num_programs(1) - 1)
    def _():
        s1_ref[0] = s_scr[...].reshape(s1_ref.shape[1:])


def _in_proj_kernel(*refs, nh, npt):
    h_refs, (g_ref, wa, wb, wc, wd, wg, ws, za, zb, zc, zd, zg, zs) = refs[:nh], refs[nh:]
    xn = _rms(_rows_read(h_refs, npt), g_ref[...]).astype(BF16)
    for w_ref, o_ref in ((wa, za), (wb, zb), (wc, zc), (wd, zd), (wg, zg)):
        width = o_ref.shape[1]
        for j in range(0, width, 512):
            jw = min(512, width - j)
            o_ref[:, j:j + jw] = _mm(xn, w_ref[:, j:j + jw]).astype(o_ref.dtype)
    zs[...] = _mm(xn, ws[...])


def _in_proj(h_parts, gain, wts):
    counts = [p.shape[0] for p in h_parts]
    n = sum(counts)
    tm = _row_tile(*counts)
    wa, wb, wc, wd, wg, ws = wts
    const = lambda i: (0, 0)
    row = lambda i: (i, 0)

    def wspec(w):
        return pl.BlockSpec(w.shape, const, pipeline_mode=pl.Buffered(1))

    outs = [jax.ShapeDtypeStruct((n, w.shape[1]), BF16) for w in (wa, wb, wc, wd, wg)]
    outs.append(jax.ShapeDtypeStruct((n, SMALL_W), F32))
    out_specs = [pl.BlockSpec((tm, w.shape[1]), row) for w in (wa, wb, wc, wd, wg)]
    out_specs.append(pl.BlockSpec((tm, SMALL_W), row))
    return pl.pallas_call(
        functools.partial(_in_proj_kernel, nh=len(h_parts), npt=counts[0] // tm),
        grid=(n // tm,),
        in_specs=_rows_specs(h_parts, tm) + [pl.BlockSpec((1, D_MODEL), const)] + [wspec(w) for w in wts],
        out_specs=out_specs,
        out_shape=outs,
        compiler_params=pltpu.CompilerParams(dimension_semantics=("parallel",), vmem_limit_bytes=VMEM_LIMIT),
        name="in_proj",
    )(*h_parts, gain, *wts)


class _Stack:
    def __init__(self, c, ls):
        self.c, self.ls = c, ls
        self.hs = MIX_H * c
        self.nseq = c // ls
        self.masks = _Masks(self.hs, ls)
        self.head_of_lane = lax.broadcasted_iota(jnp.int32, (1, MIX_H * MIX_DK), 1) >> _log2(MIX_DK)
        self.seq_of_row = (lax.broadcasted_iota(jnp.int32, (self.hs, 1), 0) & (c - 1)) >> _log2(ls)

    def keys(self, x):
        return jnp.concatenate(
            [jnp.where(self.head_of_lane == h, x, jnp.zeros_like(x)) for h in range(MIX_H)], axis=0)

    def values(self, x):
        return jnp.concatenate([x[:, MIX_DV * h:MIX_DV * (h + 1)] for h in range(MIX_H)], axis=0)

    def rows_of(self, b, x):
        if self.nseq == 1:
            return x
        return jnp.where(self.seq_of_row == b, x, jnp.zeros_like(x))


def _write_heads(o_ref, rows, o_stacked, c, gain, gate):
    for h in range(MIX_H):
        vl = slice(MIX_DV * h, MIX_DV * (h + 1))
        y = _rms(o_stacked[h * c:(h + 1) * c, :], gain) * _silu(gate[:, vl])
        o_ref[rows, vl] = y.astype(o_ref.dtype)


def _gla_kernel(*refs, c, cpb, ls, seq_mode):
    if seq_mode:
        z_ref, zs_ref, wgk_ref, bgk_ref, gain_ref, o_ref, s1_ref, s_scr = refs
        s0_ref = None
        _seq_prologue(s_scr)
    else:
        z_ref, zs_ref, wgk_ref, bgk_ref, gain_ref, s0_ref, o_ref, s1_ref = refs
        s_scr = None
    hdk = MIX_H * MIX_DK
    mkb = _Masks(cpb * c, ls)
    sk = _Stack(c, ls)
    r = lax.broadcasted_iota(jnp.int32, (hdk, hdk), 0)
    q = lax.broadcasted_iota(jnp.int32, (hdk, hdk), 1)
    eye_k = r == q
    ones_dv = jnp.ones((hdk, MIX_DV), BF16)
    qq = z_ref[:, 0:256].astype(F32) * (MIX_DK ** -0.5)
    kk = z_ref[:, 256:512].astype(F32)
    pre = _mm(zs_ref[...].astype(BF16), wgk_ref[...]) + bgk_ref[...]
    lg = -_softplus(-pre) * (1.0 / GLA_GATE_NORM)
    g = _mm_xr(mkb.tri, lg)
    glast = _mm_xr(mkb.lastsel, g)
    qe_all = (qq * jnp.exp(g)).astype(BF16)
    ke_all = (kk * jnp.exp(-g)).astype(BF16)
    kd_all = (kk * jnp.exp(glast - g)).astype(BF16)
    el_all = jnp.exp(glast)
    chunks = []
    for ci in range(cpb):
        rows = slice(ci * c, (ci + 1) * c)
        qe = sk.keys(qe_all[rows, :])
        ke = sk.keys(ke_all[rows, :])
        kd = sk.keys(kd_all[rows, :])
        v = sk.values(z_ref[rows, 512:1024])
        a = jnp.where(sk.masks.incl, _mm_nt(qe, ke), 0.0).astype(BF16)
        el = el_all[rows, :]
        decays = []
        for b in range(sk.nseq):
            diag = jnp.where(eye_k, jnp.broadcast_to(el[b * ls:b * ls + 1, :], (hdk, hdk)), 0.0)
            decays.append(_mm_xl(diag, ones_dv))
        chunks.append(dict(rows=rows, qe=qe, kd=kd, v=v, o=_mm(a, v), decays=decays))
    for ch in chunks:
        o = ch["o"]
        for b in range(sk.nseq):
            s = s_scr[...] if seq_mode else s0_ref[b].reshape(hdk, MIX_DV)
            o = o + _mm(sk.rows_of(b, ch["qe"]), s.astype(BF16))
            s_new = s * ch["decays"][b] + _mm_tn(sk.rows_of(b, ch["kd"]), ch["v"])
            if seq_mode:
                s_scr[...] = s_new
            else:
                s1_ref[b] = s_new.reshape(MIX_H, MIX_DK, MIX_DV)
        rows = ch["rows"]
        _write_heads(o_ref, rows, o, c, gain_ref[...], z_ref[rows, 1024:1536].astype(F32))
    if seq_mode:
        _seq_epilogue(s_scr, s1_ref)


def _ret_kernel(*refs, c, cpb, ls, seq_mode):
    if seq_mode:
        z_ref, cos_ref, sin_ref, gain_ref, o_ref, s1_ref, s_scr = refs
        s0_ref = None
        _seq_prologue(s_scr)
    else:
        z_ref, cos_ref, sin_ref, gain_ref, s0_ref, o_ref, s1_ref = refs
        s_scr = None
    hdk = MIX_H * MIX_DK
    sk = _Stack(c, ls)
    ms = sk.masks
    t_col = (lax.broadcasted_iota(jnp.int32, (cpb * c, 1), 0) & (ls - 1)).astype(F32)
    lgam =[math.log(1.0 - 2.0 ** (-5.0 - h)) for h in range(MIX_H)]

    def per_head(idx):
        out = jnp.zeros(idx.shape, F32)
        for h in range(MIX_H):
            out = jnp.where(idx == h, lgam[h], out)
        return out

    lane = lax.broadcasted_iota(jnp.int32, (1, hdk), 1)
    lg_lane = per_head(lane >> _log2(MIX_DK))
    lg_srow = per_head(lax.broadcasted_iota(jnp.int32, (sk.hs, 1), 0) >> _log2(c))
    lg_krow = per_head(lax.broadcasted_iota(jnp.int32, (hdk, 1), 0) >> _log2(MIX_DK))
    first_half = (lane & (MIX_DK - 1)) < MIX_DK // 2
    eg = jnp.exp((t_col + 1.0) * lg_lane)
    ed = jnp.exp((ls - 1.0 - t_col) * lg_lane)
    dt_pos = ((ms.r & (ls - 1)) - (ms.q & (ls - 1))).astype(F32)
    dm = jnp.where(ms.incl, jnp.exp(jnp.where(ms.incl, dt_pos * lg_srow, 0.0)), 0.0)
    dec = jnp.broadcast_to(jnp.exp(lg_krow * float(ls)), (hdk, MIX_DV))

    def rope(x, cs, sn):
        sw = jnp.where(first_half, pltpu.roll(x, hdk - MIX_DK // 2, 1), pltpu.roll(x, MIX_DK // 2, 1))
        return x * cs + sw * sn

    cs = cos_ref[...]
    sn = sin_ref[...]
    qr = rope(z_ref[:, 0:256].astype(F32), cs, sn)
    kr = rope(z_ref[:, 256:512].astype(F32), cs, sn) * (MIX_DK ** -0.5)
    qb_all = qr.astype(BF16)
    kb_all = kr.astype(BF16)
    qe_all = (qr * eg).astype(BF16)
    kd_all = (kr * ed).astype(BF16)
    chunks = []
    for ci in range(cpb):
        rows = slice(ci * c, (ci + 1) * c)
        v = sk.values(z_ref[rows, 512:1024])
        a = (_mm_nt(sk.keys(qb_all[rows, :]), sk.keys(kb_all[rows, :])) * dm).astype(BF16)
        chunks.append(dict(rows=rows, qe=sk.keys(qe_all[rows, :]), kd=sk.keys(kd_all[rows, :]),
                           v=v, o=_mm(a, v)))
    for ch in chunks:
        o = ch["o"]
        for b in range(sk.nseq):
            s = s_scr[...] if seq_mode else s0_ref[b].reshape(hdk, MIX_DV)
            o = o + _mm(sk.rows_of(b, ch["qe"]), s.astype(BF16))
            s_new = s * dec + _mm_tn(sk.rows_of(b, ch["kd"]), ch["v"])
            if seq_mode:
                s_scr[...] = s_new
            else:
                s1_ref[b] = s_new.reshape(MIX_H, MIX_DK, MIX_DV)
        rows = ch["rows"]
        _write_heads(o_ref, rows, o, c, gain_ref[...], z_ref[rows, 1024:1536].astype(F32))
    if seq_mode:
        _seq_epilogue(s_scr, s1_ref)


def _gdn_kernel(*refs, c, cpb, ls, seq_mode):
    if seq_mode:
        z_ref, zs_ref, cw_ref, prow_ref, gain_ref, o_ref, s1_ref, s_scr, prev_scr = refs
        s0_ref = prev_ref = None
        _seq_prologue(s_scr, prev_scr)
    else:
        z_ref, zs_ref, cw_ref, prow_ref, gain_ref, prev_ref, s0_ref, o_ref, s1_ref = refs
        s_scr = prev_scr = None
    hdk = MIX_H * MIX_DK
    mk = _Masks(c, ls)
    mkb = _Masks(cpb * c, ls)
    sk = _Stack(c, ls)
    hs, mks = sk.hs, sk.masks
    shift_m = mk.shift_matrix(1 if seq_mode else 2)
    r = lax.broadcasted_iota(jnp.int32, (hdk, hdk), 0)
    q = lax.broadcasted_iota(jnp.int32, (hdk, hdk), 1)
    ones_bd = jnp.where((r >> 6) == (q >> 6), 1.0, 0.0).astype(BF16)
    e_beta_k = _expander(LANE_BETA, MIX_DK, hdk)
    e_beta_v = _expander(LANE_BETA, MIX_DV, MIX_H * MIX_DV)
    e_g_k = _expander(LANE_GDA, MIX_DK, hdk)
    e_g_v = _expander(LANE_GDA, MIX_DV, MIX_H * MIX_DV)
    a_row = -jnp.exp(prow_ref[0:1, :])
    b_row = prow_ref[1:2, :]
    cw = cw_ref[...]
    lane0 = jnp.where(lax.broadcasted_iota(jnp.int32, (hs, SMALL_W), 1) == 0, 1.0, 0.0).astype(BF16)
    stack_k, stack_v, srows_of = sk.keys, sk.values, sk.rows_of

    conv = []
    for ci in range(cpb):
        u = z_ref[ci * c:(ci + 1) * c, 0:GDN_CONV_C]
        if seq_mode:
            prev_parts = [prev_scr[...].astype(BF16)]
        else:
            prev_parts = list(_split(prev_ref[...]))
        conv.append(_causal_conv(u, prev_parts, cw, mk, shift_m))
        if seq_mode:
            prev_scr[...] = u.astype(F32)
    cqkv = _silu(conv[0] if cpb == 1 else jnp.concatenate(conv, axis=0))
    cq = cqkv[:, 0:256]
    ck = cqkv[:, 256:512]
    cv = cqkv[:, 512:1024]
    qn = cq * lax.rsqrt(_mm_xl(cq * cq, ones_bd) + EPS) * (MIX_DK ** -0.5)
    kn = ck * lax.rsqrt(_mm_xl(ck * ck, ones_bd) + EPS)
    zs = zs_ref[...]
    beta = _sigmoid(zs)
    lgd = a_row * _softplus(zs + b_row)
    gc_all = _mm_xr(mkb.tri, lgd)
    glast = _mm_xr(mkb.lastsel, gc_all)
    beta_k = _mm_xl(beta, e_beta_k)
    beta_v = _mm_xl(beta, e_beta_v)
    eg_k = _mm_xl(jnp.exp(gc_all), e_g_k)
    dl_k = _mm_xl(jnp.exp(glast - gc_all), e_g_k)
    el_v_all = _mm_xl(jnp.exp(glast), e_g_v)
    kbeta = kn * beta_k
    kn_all = kn.astype(BF16)
    kbeta_all = kbeta.astype(BF16)
    qn_all = qn.astype(BF16)
    vbeta_all = (cv * beta_v).astype(BF16)
    kbe_all = (kbeta * eg_k).astype(BF16)
    qe_all = (qn * eg_k).astype(BF16)
    kd_all = (kn * dl_k).astype(BF16)

    chunks = []
    for ci in range(cpb):
        rows = slice(ci * c, (ci + 1) * c)
        gc = gc_all[rows, :]
        k_st = stack_k(kn_all[rows, :])
        g_col = jnp.concatenate([gc[:, LANE_GDA + h:LANE_GDA + h + 1] for h in range(MIX_H)], axis=0)
        g_hi, g_lo = _split(jnp.broadcast_to(g_col, (hs, SMALL_W)))
        g_row = _mm_nt(lane0, g_hi) + _mm_nt(lane0, g_lo)
        dm = _pair_decay(g_col, g_row, mks.incl)
        m = _mm_nt(stack_k(kbeta_all[rows, :]), k_st) * jnp.where(mks.strict, dm, 0.0)
        chunks.append(dict(
            rows=rows, m=m, p=mks.eye - m, mp=m,
            a=(_mm_nt(stack_k(qn_all[rows, :]), k_st) * dm).astype(BF16),
            vbeta=stack_v(vbeta_all[rows, :]),
            kbe=stack_k(kbe_all[rows, :]),
            qe=stack_k(qe_all[rows, :]),
            kd=stack_k(kd_all[rows, :]),
            el_v=el_v_all[rows, :]))
    n = 2
    while n < ls:
        for ch in chunks:
            mpb = ch["mp"].astype(BF16)
            ch["mp"] = _mm(mpb, mpb)
        for ch in chunks:
            ch["p"] = ch["p"] + _mm(ch["p"].astype(BF16), ch["mp"].astype(BF16))
        n *= 2
    for ch in chunks:
        tinv = ch["p"].astype(BF16)
        ch["uu"] = _mm(tinv, ch["vbeta"])
        ch["ww"] = _mm(tinv, ch["kbe"]).astype(BF16)
    for ch in chunks:
        rows = ch["rows"]
        states = []
        vn = ch["uu"]
        for b in range(mk.nseq):
            s = s_scr[...] if seq_mode else s0_ref[b].reshape(hdk, MIX_DV)
            states.append(s)
            vn = vn - _mm(srows_of(b, ch["ww"]), s.astype(BF16))
        vnb = vn.astype(BF16)
        o = _mm(ch["a"], vnb)
        for b in range(mk.nseq):
            s = states[b]
            o = o + _mm(srows_of(b, ch["qe"]), s.astype(BF16))
            dec = jnp.concatenate(
                [jnp.broadcast_to(ch["el_v"][b * ls:b * ls + 1, MIX_DV * h:MIX_DV * (h + 1)], (MIX_DK, MIX_DV))
                 for h in range(MIX_H)], axis=0)
            s_new = s * dec + _mm_tn(srows_of(b, ch["kd"]), vnb)
            if seq_mode:
                s_scr[...] = s_new
            else:
                s1_ref[b] = s_new.reshape(MIX_H, MIX_DK, MIX_DV)
        _write_heads(o_ref, rows, o, c, gain_ref[...], z_ref[rows, GDN_CONV_C:GDN_CONV_C + 512].astype(F32))
    if seq_mode:
        _seq_epilogue(s_scr, s1_ref)


def _ssd_kernel(*refs, c, cpb, ls, seq_mode):
    gw = SSD_DI // SSD_G
    hpg = SSD_H // SSD_G
    if seq_mode:
        z_ref, zs_ref, cw_ref, cb_ref, prow_ref, dx_ref, gain_ref, o_ref, s1_ref, s_scr, prev_scr = refs
        s0_ref = prev_ref = None
        _seq_prologue(s_scr, prev_scr)
    else:
        z_ref, zs_ref, cw_ref, cb_ref, prow_ref, dx_ref, gain_ref, prev_ref, s0_ref, o_ref, s1_ref = refs
        s_scr = prev_scr = None
    mk = _Masks(c, ls)
    mkb = _Masks(cpb * c, ls)
    shift_m = mk.shift_matrix(1 if seq_mode else 2)
    e_x = _expander(LANE_SDT, SSD_P, SSD_DI)
    a_row = -jnp.exp(prow_ref[0:1, :])
    b_row = prow_ref[1:2, :]
    cw = cw_ref[...]
    lane_g = lax.broadcasted_iota(jnp.int32, (1, gw), 1) >> 6

    def get_state(b, g):
        if seq_mode:
            return s_scr[g]
        return jnp.concatenate([s0_ref[b, g * hpg + j] for j in range(hpg)], axis=-1)

    def put_state(b, g, val):
        if seq_mode:
            s_scr[g] = val
        else:
            for j in range(hpg):
                s1_ref[b, g * hpg + j] = val[:, SSD_P * j:SSD_P * (j + 1)]

    conv = []
    for ci in range(cpb):
        u = z_ref[ci * c:(ci + 1) * c, SSD_DI:SSD_DI + SSD_CONV_C]
        if seq_mode:
            prev_parts = [prev_scr[...].astype(BF16)]
        else:
            prev_parts = list(_split(prev_ref[...]))
        conv.append(_causal_conv(u, prev_parts, cw, mk, shift_m))
        if seq_mode:
            prev_scr[...] = u.astype(F32)
    xbc = _silu((conv[0] if cpb == 1 else jnp.concatenate(conv, axis=0)) + cb_ref[...])
    sx_all = xbc[:, 0:SSD_DI]
    sb_all = xbc[:, SSD_DI:SSD_DI + SSD_G * SSD_N].astype(BF16)
    sc_all = xbc[:, SSD_DI + SSD_G * SSD_N:SSD_CONV_C].astype(BF16)
    dt = _softplus(zs_ref[...] + b_row)
    lsd_all = dt * a_row
    gc_all = _mm_xr(mkb.tri, lsd_all)
    glast = _mm_xr(mkb.lastsel, gc_all)
    eg_all = _mm_xl(jnp.exp(gc_all), e_x)
    el_all = _mm_xl(jnp.exp(glast), e_x)
    v_all = sx_all * _mm_xl(dt, e_x)
    vd_all = (v_all * _mm_xl(jnp.exp(glast - gc_all), e_x)).astype(BF16)
    skip_all = sx_all * dx_ref[...]
    gate_all = _silu(z_ref[:, 0:SSD_DI].astype(F32))

    for ci in range(cpb):
        rows = slice(ci * c, (ci + 1) * c)
        gc = gc_all[rows, :]
        lsd_hi, lsd_lo = _split(lsd_all[rows, :])
        gt = _mm_tn(lsd_hi, mk.tri_t) + _mm_tn(lsd_lo, mk.tri_t)
        for g in range(SSD_G):
            gl = slice(gw * g, gw * (g + 1))
            cg = sc_all[rows, SSD_N * g:SSD_N * (g + 1)]
            bg = sb_all[rows, SSD_N * g:SSD_N * (g + 1)]
            ag = _mm_nt(cg, bg)
            vg = v_all[rows, gl]
            o = jnp.zeros((c, gw), F32)
            for j in range(hpg):
                lane = LANE_SDT + g * hpg + j
                dm = _pair_decay(gc[:, lane:lane + 1], gt[lane:lane + 1, :], mk.incl)
                vj = jnp.where(lane_g == j, vg, 0.0).astype(BF16)
                o = o + _mm((ag * dm).astype(BF16), vj)
            o_state = jnp.zeros((c, gw), F32)
            for b in range(mk.nseq):
                s = get_state(b, g)
                o_state = o_state + _mm(mk.rows_of(b, cg), s.astype(BF16))
                dec = el_all[ci * c + b * ls:ci * c + b * ls + 1, gl]
                put_state(b, g, s * dec + _mm_tn(mk.rows_of(b, bg), vd_all[rows, gl]))
            y = (o + o_state * eg_all[rows, gl] + skip_all[rows, gl]) * gate_all[rows, gl]
            o_ref[rows, gl] = _rms(y, gain_ref[:, gl]).astype(o_ref.dtype)
    if seq_mode:
        @pl.when(pl.program_id(1) == pl.num_programs(1) - 1)
        def _():
            for g in range(SSD_G):
                for j in range(hpg):
                    s1_ref[0, g * hpg + j] = s_scr[g][:, SSD_P * j:SSD_P * (j + 1)]


def _full_spec(a, grid_rank):
    nd = a.ndim
    if grid_rank == 1:
        return pl.BlockSpec(a.shape, lambda i: (0,) * nd)
    return pl.BlockSpec(a.shape, lambda i, j: (0,) * nd)


def _mixer_seq(kernel_fn, name, z, extra_rows, params, nbatch, seqlen, state_shape, scratch):
    c, cpb = CHUNK, SEQ_BLOCK_CHUNKS
    rblk = c * cpb
    nblk = seqlen // rblk
    rowmap = lambda b, j: (b * nblk + j, 0)
    in_specs = [pl.BlockSpec((rblk, z.shape[1]), rowmap)]
    args = [z]
    for a, kind in extra_rows:
        if kind == "rows":
            in_specs.append(pl.BlockSpec((rblk, a.shape[1]), rowmap))
        else:
            in_specs.append(pl.BlockSpec((rblk, a.shape[1]), lambda b, j: (j, 0)))
        args.append(a)
    for a in params:
        in_specs.append(_full_spec(a, 2))
        args.append(a)
    sblk = (1,) + state_shape
    out_shape = [jax.ShapeDtypeStruct((nbatch * seqlen, BRANCH_W), BF16),
                 jax.ShapeDtypeStruct((nbatch,) + state_shape, F32)]
    out_specs = [pl.BlockSpec((rblk, BRANCH_W), rowmap),
                 pl.BlockSpec(sblk, lambda b, j: (b,) + (0,) * len(state_shape))]
    return pl.pallas_call(
        functools.partial(kernel_fn, c=c, cpb=cpb, ls=c, seq_mode=True),
        grid=(nbatch, nblk),
        in_specs=in_specs,
        out_specs=out_specs,
        out_shape=out_shape,
        scratch_shapes=scratch,
        compiler_params=pltpu.CompilerParams(dimension_semantics=("parallel", "arbitrary"),
                                             vmem_limit_bytes=VMEM_LIMIT),
        name=name,
    )(*args)


def _mixer_batch(kernel_fn, name, z, row_off, extra_rows, params, per_seq, layer, stacked_prev,
                 nbatch, seqlen, state_shape):
    sb = BATCH_SEQS
    c = sb * seqlen
    nsteps = nbatch // sb
    off = row_off // c
    rowmap = lambda i: (off + i, 0)
    in_specs = [pl.BlockSpec((c, z.shape[1]), rowmap)]
    args = [z]
    for a, kind in extra_rows:
        if kind == "rows":
            in_specs.append(pl.BlockSpec((c, a.shape[1]), rowmap))
        else:
            in_specs.append(pl.BlockSpec((c, a.shape[1]), lambda i: (0, 0)))
        args.append(a)
    for a in params:
        in_specs.append(_full_spec(a, 1))
        args.append(a)
    for a in per_seq:
        if a.ndim == 2:
            in_specs.append(pl.BlockSpec((c, a.shape[1]), lambda i: (i, 0)))
        else:
            in_specs.append(pl.BlockSpec((None, sb) + a.shape[2:], lambda i: (layer, i) + (0,) * (a.ndim - 2)))
        args.append(a)
    zeros = (0,) * len(state_shape)
    if layer:
        in_specs.append(pl.BlockSpec((layer, sb) + state_shape, lambda i: (0, i) + zeros))
        args.append(stacked_prev)

    def body(*refs):
        *ins, o_ref, s1_ref = refs
        if layer:
            prev_ref = ins.pop()
            for l in range(layer):
                s1_ref[l] = prev_ref[l]
        kernel_fn(*ins, o_ref, s1_ref.at[layer], c=c, cpb=1, ls=seqlen, seq_mode=False)

    out_shape = [jax.ShapeDtypeStruct((nbatch * seqlen, BRANCH_W), BF16),
                 jax.ShapeDtypeStruct((layer + 1, nbatch) + state_shape, F32)]
    out_specs = [pl.BlockSpec((c, BRANCH_W), lambda i: (i, 0)),
                 pl.BlockSpec((layer + 1, sb) + state_shape, lambda i: (0, i) + zeros)]
    return pl.pallas_call(
        body,
        grid=(nsteps,),
        in_specs=in_specs,
        out_specs=out_specs,
        out_shape=out_shape,
        compiler_params=pltpu.CompilerParams(dimension_semantics=("parallel",), vmem_limit_bytes=VMEM_LIMIT),
        name=name,
    )(*args)


def _merge_kernel(*refs, nh, npt):
    branch_refs, zg = refs[:2 * N_BRANCH], refs[2 * N_BRANCH]
    h_refs, (wbr, wout, out_ref) = refs[2 * N_BRANCH + 1:2 * N_BRANCH + 1 + nh], refs[2 * N_BRANCH + 1 + nh:]
    acc = jnp.zeros(out_ref.shape, F32)
    for n in range(N_BRANCH):
        gate = _sigmoid(zg[:, n * D_MODEL:(n + 1) * D_MODEL].astype(F32))
        acc = acc + gate * _mm(_rows_read(branch_refs[2 * n:2 * n + 2], npt), wbr[n])
    out_ref[...] = _rows_read(h_refs, npt) + _mm(acc.astype(BF16), wout[...])


def _merge(branches, zg, h_parts, wbr, wout):
    n_p, n_s = branches[0][0].shape[0], branches[0][1].shape[0]
    n = n_p + n_s
    tm = _row_tile(n_p, n_s)
    row = lambda i: (i, 0)
    in_specs, args = [], []
    for pair in branches:
        in_specs += _rows_specs(pair, tm)
        args += list(pair)
    in_specs.append(pl.BlockSpec((tm, N_BRANCH * D_MODEL), row))
    in_specs += _rows_specs(h_parts, tm)
    in_specs += [pl.BlockSpec(wbr.shape, lambda i: (0, 0, 0)), pl.BlockSpec(wout.shape, lambda i: (0, 0))]
    return pl.pallas_call(
        functools.partial(_merge_kernel, nh=len(h_parts), npt=n_p // tm),
        grid=(n // tm,),
        in_specs=in_specs,
        out_specs=pl.BlockSpec((tm, D_MODEL), row),
        out_shape=jax.ShapeDtypeStruct((n, D_MODEL), F32),
        compiler_params=pltpu.CompilerParams(dimension_semantics=("parallel",), vmem_limit_bytes=VMEM_LIMIT),
        name="merge",
    )(*args, zg, *h_parts, wbr, wout)


def _ffn_kernel(h_ref, g_ref, wg_ref, wu_ref, wd_ref, out_ref):
    h = h_ref[...]
    u = _rms(h, g_ref[...]).astype(BF16)
    a = (_silu(_mm(u, wg_ref[...])) * _mm(u, wu_ref[...])).astype(BF16)
    out_ref[...] = h + _mm(a, wd_ref[...])


def _ffn(h, gain, wg, wu, wd):
    n = h.shape[0]
    tm = _row_tile(n)
    row = lambda i: (i, 0)
    const = lambda i: (0, 0)
    wspec = lambda w: pl.BlockSpec(w.shape, const, pipeline_mode=pl.Buffered(1))
    return pl.pallas_call(
        _ffn_kernel,
        grid=(n // tm,),
        in_specs=[pl.BlockSpec((tm, D_MODEL), row), pl.BlockSpec((1, D_MODEL), const),
                  wspec(wg), wspec(wu), wspec(wd)],
        out_specs=pl.BlockSpec((tm, D_MODEL), row),
        out_shape=jax.ShapeDtypeStruct((n, D_MODEL), F32),
        compiler_params=pltpu.CompilerParams(dimension_semantics=("parallel",), vmem_limit_bytes=VMEM_LIMIT),
        name="dense_ffn",
    )(h, gain, wg, wu, wd)


MOE_TOKENS = 768
MOE_CAP = 224


def _moe_kernel(h_ref, g_ref, rt_ref, wg_ref, wu_ref, wd_ref, out_ref,
                u_scr, acc_scr, w_scr, sel_scr, rank_scr, selt_scr, rankt_scr, xc_scr, yc_scr, cnt_scr):
    tt = h_ref.shape[0]
    cap = MOE_CAP
    e = pl.program_id(1)
    f = pl.program_id(2)
    nf = pl.num_programs(2)

    @pl.when(jnp.logical_and(e == 0, f == 0))
    def _():
        u = _rms(h_ref[...], g_ref[...])
        u_scr[...] = u.astype(BF16)
        acc_scr[...] = jnp.zeros_like(acc_scr)
        uh, ul = _split(u)
        rh, rl = _split(rt_ref[...])
        logits = _mm(uh, rh) + (_mm(uh, rl) + _mm(ul, rh))
        lane = lax.broadcasted_iota(jnp.int32, logits.shape, 1).astype(F32)
        neg = -3.0e38
        lg = jnp.where(lane < N_EXPERTS, logits, neg)
        m1 = jnp.max(lg, axis=-1, keepdims=True)
        i1 = jnp.min(jnp.where(lg == m1, lane, float(SMALL_W)), axis=-1, keepdims=True)
        lg2 = jnp.where(lane == i1, neg, lg)
        m2 = jnp.max(lg2, axis=-1, keepdims=True)
        i2 = jnp.min(jnp.where(lg2 == m2, lane, float(SMALL_W)), axis=-1, keepdims=True)
        e2 = jnp.exp(m2 - m1)
        w_scr[...] = jnp.where(lane == i1, 1.0 / (1.0 + e2), 0.0) + jnp.where(lane == i2, e2 / (1.0 + e2), 0.0)
        sel = jnp.where(jnp.logical_or(lane == i1, lane == i2), 1.0, 0.0)
        sel_scr[...] = sel
        selb = sel.astype(BF16)
        r = lax.broadcasted_iota(jnp.int32, (tt, tt), 0)
        q = lax.broadcasted_iota(jnp.int32, (tt, tt), 1)
        before = jnp.where(r < q, 1.0, 0.0).astype(BF16)
        ident = jnp.where(r == q, 1.0, 0.0).astype(BF16)
        rank_scr[...] = _mm_tn(before, selb)
        rankt_scr[...] = _mm_tn(selb, before)
        selt_scr[...] = _mm_tn(selb, ident)
        cnt = jnp.sum(sel, axis=0, keepdims=True)
        for ee in range(N_EXPERTS):
            cnt_scr[ee] = jnp.sum(jnp.where(lane[0:1, :] == float(ee), cnt, 0.0)).astype(jnp.int32)

    nsub = (cnt_scr[e] + (cap - 1)) // cap
    lane_w = lax.broadcasted_iota(jnp.int32, (tt, SMALL_W), 1)

    def col_of(ref):
        return jnp.sum(jnp.where(lane_w == e, ref[...], 0.0), axis=-1, keepdims=True)

    @pl.when(f == 0)
    def _():
        rank_row = rankt_scr[pl.ds(e, 1), :]
        sel_row = selt_scr[pl.ds(e, 1), :]
        jcol = lax.broadcasted_iota(jnp.int32, (cap, 1), 0).astype(F32)

        def gather(s, carry):
            base = (s * cap).astype(F32)
            hit = jnp.logical_and(rank_row == jcol + base, sel_row > 0.5)
            onehot = jnp.where(hit, 1.0, 0.0).astype(BF16)
            off = pl.multiple_of(s * cap, 16)
            xc_scr[pl.ds(off, cap), :] = _mm(onehot, u_scr[...]).astype(BF16)
            return carry

        lax.fori_loop(0, nsub, gather, 0)

    def expert(s, carry):
        off = pl.multiple_of(s * cap, 16)
        x = xc_scr[pl.ds(off, cap), :]
        a = (_silu(_mm(x, wg_ref[0])) * _mm(x, wu_ref[0])).astype(BF16)
        y = _mm(a, wd_ref[0])

        @pl.when(f == 0)
        def _():
            yc_scr[pl.ds(off, cap), :] = y

        @pl.when(f != 0)
        def _():
            yc_scr[pl.ds(off, cap), :] += y

        return carry

    lax.fori_loop(0, nsub, expert, 0)

    @pl.when(f == nf - 1)
    def _():
        rank_col = col_of(rank_scr)
        sel_col = col_of(sel_scr)
        w_col = col_of(w_scr)
        jrow = lax.broadcasted_iota(jnp.int32, (1, cap), 1).astype(F32)

        def scatter(s, carry):
            base = (s * cap).astype(F32)
            hit = jnp.logical_and(rank_col == jrow + base, sel_col > 0.5)
            onehot = jnp.where(hit, 1.0, 0.0).astype(BF16)
            off = pl.multiple_of(s * cap, 16)
            acc_scr[...] += w_col * _mm(onehot, yc_scr[pl.ds(off, cap), :].astype(BF16))
            return carry

        lax.fori_loop(0, nsub, scatter, 0)

    @pl.when(jnp.logical_and(e == pl.num_programs(1) - 1, f == nf - 1))
    def _():
        out_ref[...] = h_ref[...] + acc_scr[...]


def _moe(h, gain, router, wg, wu, wd):
    n = h.shape[0]
    tt = MOE_TOKENS if n % MOE_TOKENS == 0 else _row_tile(n)
    ne = wg.shape[0]
    tf = D_FF // FF_BLOCKS
    row = lambda i, e, f: (i, 0)
    nrows = pl.cdiv(tt, MOE_CAP) * MOE_CAP
    in_specs = [pl.BlockSpec((tt, D_MODEL), row), pl.BlockSpec((1, D_MODEL), lambda i, e, f: (0, 0)),
                pl.BlockSpec(router.shape, lambda i, e, f: (0, 0)),
                pl.BlockSpec((1, D_MODEL, tf), lambda i, e, f: (e, 0, f)),
                pl.BlockSpec((1, D_MODEL, tf), lambda i, e, f: (e, 0, f)),
                pl.BlockSpec((1, tf, D_MODEL), lambda i, e, f: (e, f, 0))]
    scratch = [pltpu.VMEM((tt, D_MODEL), BF16), pltpu.VMEM((tt, D_MODEL), F32),
               pltpu.VMEM((tt, SMALL_W), F32), pltpu.VMEM((tt, SMALL_W), F32), pltpu.VMEM((tt, SMALL_W), F32),
               pltpu.VMEM((SMALL_W, tt), F32), pltpu.VMEM((SMALL_W, tt), F32),
               pltpu.VMEM((nrows, D_MODEL), BF16), pltpu.VMEM((nrows, D_MODEL), F32),
               pltpu.SMEM((N_EXPERTS,), jnp.int32)]
    return pl.pallas_call(
        _moe_kernel,
        grid=(n // tt, ne, FF_BLOCKS),
        in_specs=in_specs,
        out_specs=pl.BlockSpec((tt, D_MODEL), row),
        out_shape=jax.ShapeDtypeStruct((n, D_MODEL), F32),
        scratch_shapes=scratch,
        compiler_params=pltpu.CompilerParams(dimension_semantics=("parallel", "arbitrary", "arbitrary"),
                                             vmem_limit_bytes=VMEM_LIMIT),
        name="moe",
    )(h, gain, router, wg, wu, wd)


def _ple_kernel(*refs, final, npt):
    if final:
        h_ref, pp_ref, ps_ref, g_ref, wgate, wproj, gf_ref, yp_ref, ys_ref = refs
    else:
        h_ref, pp_ref, ps_ref, g_ref, wgate, wproj, out_ref = refs
    h = h_ref[...]
    gate = _sigmoid(_mm(_rms(h, g_ref[...]).astype(BF16), wgate[...]))
    h = h + gate * _mm(_rows_read((pp_ref, ps_ref), npt).astype(BF16), wproj[...])
    if final:
        y = _rms(h, gf_ref[...])

        @pl.when(pl.program_id(0) < npt)
        def _():
            yp_ref[...] = y

        @pl.when(pl.program_id(0) >= npt)
        def _():
            ys_ref[...] = y
    else:
        out_ref[...] = h


def _ple(h, p_parts, layer, gain, wgate, wproj, gain_final=None):
    n_p, n_s = p_parts[0].shape[1], p_parts[1].shape[1]
    n = h.shape[0]
    tm = _row_tile(n_p, n_s)
    final = gain_final is not None
    row = lambda i: (i, 0)
    const = lambda i: (0, 0)
    in_specs = [pl.BlockSpec((tm, D_MODEL), row)] + _rows_specs(p_parts, tm, layer)
    in_specs += [pl.BlockSpec((1, D_MODEL), const), pl.BlockSpec(wgate.shape, const), pl.BlockSpec(wproj.shape, const)]
    args = [h, *p_parts, gain, wgate, wproj]
    if final:
        in_specs.append(pl.BlockSpec((1, D_MODEL), const))
        args.append(gain_final)
        out_shape = [jax.ShapeDtypeStruct((n_p, D_MODEL), F32), jax.ShapeDtypeStruct((n_s, D_MODEL), F32)]
        out_specs = _rows_specs(out_shape, tm)
    else:
        out_shape = jax.ShapeDtypeStruct((n, D_MODEL), F32)
        out_specs = pl.BlockSpec((tm, D_MODEL), row)
    return pl.pallas_call(
        functools.partial(_ple_kernel, final=final, npt=n_p // tm),
        grid=(n // tm,),
        in_specs=in_specs,
        out_specs=out_specs,
        out_shape=out_shape,
        compiler_params=pltpu.CompilerParams(dimension_semantics=("arbitrary",), vmem_limit_bytes=VMEM_LIMIT),
        name="ple",
    )(*args)


def _pad_lanes(x, lane0):
    w = x.shape[-1]
    pad = [(0, 0)] * (x.ndim - 1) + [(lane0, SMALL_W - lane0 - w)]
    return jnp.pad(x, pad)


def _in_proj_weights(w):
    cuts = np.cumsum(SPLITS)[:-1].tolist()
    (gq, gk, gv, glr, gg, rq, rk, rv, rg, dqkv, db, da, dg, sz, sxbc, sdt, mg) = jnp.split(w, cuts, axis=-1)
    cat = lambda *xs: jnp.concatenate(xs, axis=-1).astype(BF16)
    small = jnp.concatenate([glr, db, da, sdt], axis=-1)
    small = jnp.pad(small, ((0, 0), (0, SMALL_W - small.shape[1]))).astype(BF16)
    return (cat(gq, gk, gv, gg), cat(rq, rk, rv, rg), cat(dqkv, dg), cat(sz, sxbc), mg.astype(BF16),
            small)


def _rope_tables(pos):
    half = MIX_DK // 2
    inv = ROPE_BASE ** (-jnp.arange(half, dtype=F32) / half)
    ang = jnp.asarray(pos).astype(F32)[:, None] * inv[None, :]
    cos, sin = jnp.cos(ang), jnp.sin(ang)
    cos_t = jnp.tile(jnp.concatenate([cos, cos], axis=-1), (1, MIX_H))
    sin_t = jnp.tile(jnp.concatenate([-sin, sin], axis=-1), (1, MIX_H))
    return cos_t, sin_t


def _last_rows(z, nseq, seqlen, col0, col1):
    rows = [lax.slice(z, (seqlen - (CONV_W - 1) + j, col0), (nseq * seqlen, col1), (seqlen, 1))
            for j in range(CONV_W - 1)]
    return jnp.stack(rows, axis=1).astype(F32)


def _conv_prev(state_conv):
    b, _, cdim = state_conv.shape
    return jnp.pad(state_conv, ((0, 0), (1, 0), (0, 0))).reshape(b * CONV_W, cdim)


def kernel(x_prompt, x_sample, state_gla, state_ret, state_gdn, state_gdn_conv, state_ssd, state_ssd_conv, p_prompt, p_sample, norm_mix, w_in, gla_w_gk, gla_b_gk, gla_norm, ret_norm, gdn_conv_w, gdn_a_log, gdn_dt_bias, gdn_norm, ssd_conv_w, ssd_conv_b, ssd_a_log, ssd_dt_bias, ssd_d, ssd_norm, w_branch, w_out, norm_ffn, ffn_w_gate, ffn_w_up, ffn_w_down, moe_router, moe_w_gate, moe_w_up, moe_w_down, norm_ple, ple_w_gate, ple_w_proj, norm_final):
    bp, lp, _ = x_prompt.shape
    bs, lsmp, _ = x_sample.shape
    depth = w_in.shape[0]
    n_p = bp * lp
    n_s = bs * lsmp
    h_parts = (x_prompt.reshape(n_p, D_MODEL), x_sample.reshape(n_s, D_MODEL))
    p_parts = (p_prompt.reshape(depth, n_p, D_PLE), p_sample.reshape(depth, n_s, D_PLE))
    cos_p, sin_p = _rope_tables(np.arange(lp))
    cos_s, sin_s = _rope_tables(PAST_LEN + (np.arange(BATCH_SEQS * lsmp) % lsmp))
    mix_state = (MIX_H, MIX_DK, MIX_DV)
    stacked_state = (MIX_H * MIX_DK, MIX_DV)
    ssd_state = (SSD_H, SSD_N, SSD_P)

    outs_p = [[] for _ in range(6)]
    outs_s = [[] for _ in range(2)]
    gla_s = ret_s = gdn_s = ssd_s = None
    for i in range(depth):
        wts = _in_proj_weights(w_in[i])
        za, zb, zc, zd, zg, zs = _in_proj(h_parts, norm_mix[i][None, :], wts)

        wgk = jnp.pad(gla_w_gk[i], ((LANE_GLR, SMALL_W - LANE_GLR - GLA_RANK), (0, 0))).astype(BF16)
        bgk = gla_b_gk[i][None, :]
        gla_par = [wgk, bgk, gla_norm[i][None, :]]
        oa_p, gla_p = _mixer_seq(_gla_kernel, "gla_seq", za, [(zs, "rows")], gla_par, bp, lp, mix_state,
                                 [pltpu.VMEM(stacked_state, F32)])
        oa_s, gla_s = _mixer_batch(_gla_kernel, "gla_batch", za, n_p, [(zs, "rows")], gla_par,
                                   [state_gla], i, gla_s, bs, lsmp, mix_state)
        ret_par = [ret_norm[i][None, :]]
        ob_p, ret_p = _mixer_seq(_ret_kernel, "ret_seq", zb, [(cos_p, "pos"), (sin_p, "pos")], ret_par,
                                 bp, lp, mix_state, [pltpu.VMEM(stacked_state, F32)])
        ob_s, ret_s = _mixer_batch(_ret_kernel, "ret_batch", zb, n_p, [(cos_s, "pos"), (sin_s, "pos")],
                                   ret_par, [state_ret], i, ret_s, bs, lsmp, mix_state)
        gdn_prow = jnp.concatenate([_pad_lanes(gdn_a_log[i][None, :], LANE_GDA),
                                    _pad_lanes(gdn_dt_bias[i][None, :], LANE_GDA)], axis=0)
        gdn_par = [gdn_conv_w[i], gdn_prow, gdn_norm[i][None, :]]
        oc_p, gdn_p = _mixer_seq(_gdn_kernel, "gdn_seq", zc, [(zs, "rows")], gdn_par, bp, lp, mix_state,
                                 [pltpu.VMEM(stacked_state, F32), pltpu.VMEM((CHUNK, GDN_CONV_C), F32)])
        oc_s, gdn_s = _mixer_batch(_gdn_kernel, "gdn_batch", zc, n_p, [(zs, "rows")], gdn_par,
                                   [_conv_prev(state_gdn_conv[i]), state_gdn], i, gdn_s, bs, lsmp, mix_state)
        ssd_prow = jnp.concatenate([_pad_lanes(ssd_a_log[i][None, :], LANE_SDT),
                                    _pad_lanes(ssd_dt_bias[i][None, :], LANE_SDT)], axis=0)
        ssd_par = [ssd_conv_w[i], ssd_conv_b[i][None, :], ssd_prow,
                   jnp.repeat(ssd_d[i], SSD_P)[None, :], ssd_norm[i][None, :]]
        ssd_scr = [pltpu.VMEM((SSD_G, SSD_N, SSD_DI // SSD_G), F32), pltpu.VMEM((CHUNK, SSD_CONV_C), F32)]
        od_p, ssd_p = _mixer_seq(_ssd_kernel, "ssd_seq", zd, [(zs, "rows")], ssd_par,
                                 bp, lp, ssd_state, ssd_scr)
        od_s, ssd_s = _mixer_batch(_ssd_kernel, "ssd_batch", zd, n_p, [(zs, "rows")], ssd_par,
                                   [_conv_prev(state_ssd_conv[i]), state_ssd], i, ssd_s, bs, lsmp, ssd_state)

        gdnc_p = _last_rows(zc, bp, lp, 0, GDN_CONV_C)
        ssdc_p = _last_rows(zd, bp, lp, SSD_DI, SSD_DI + SSD_CONV_C)
        zc_s = zc[n_p:].reshape(bs, lsmp, -1)[:, :, :GDN_CONV_C].astype(F32)
        zd_s = zd[n_p:].reshape(bs, lsmp, -1)[:, :, SSD_DI:].astype(F32)
        gdnc_s = jnp.concatenate([state_gdn_conv[i], zc_s], axis=1)[:, lsmp:]
        ssdc_s = jnp.concatenate([state_ssd_conv[i], zd_s], axis=1)[:, lsmp:]
        for lst, val in zip(outs_p, (gla_p, ret_p, gdn_p, gdnc_p, ssd_p, ssdc_p)):
            lst.append(val)
        for lst, val in zip(outs_s, (gdnc_s, ssdc_s)):
            lst.append(val)

        branches = ((oa_p, oa_s), (ob_p, ob_s), (oc_p, oc_s), (od_p, od_s))
        h = _merge(branches, zg, h_parts, w_branch[i].astype(BF16), w_out[i].astype(BF16))

        j = i // 2
        if i % 2 == 0:
            h = _ffn(h, norm_ffn[i][None, :], ffn_w_gate[j].astype(BF16), ffn_w_up[j].astype(BF16),
                     ffn_w_down[j].astype(BF16))
        else:
            router = jnp.pad(moe_router[j], ((0, 0), (0, SMALL_W - N_EXPERTS)))
            h = _moe(h, norm_ffn[i][None, :], router, moe_w_gate[j].astype(BF16), moe_w_up[j].astype(BF16),
                     moe_w_down[j].astype(BF16))
        h = _ple(h, p_parts, i, norm_ple[i][None, :], ple_w_gate[i].astype(BF16), ple_w_proj[i].astype(BF16),
                 gain_final=norm_final[None, :] if i == depth - 1 else None)
        h_parts = (h,)

    y_prompt = h[0].reshape(bp, lp, D_MODEL)
    y_sample = h[1].reshape(bs, lsmp, D_MODEL)
    gdnc_s, ssdc_s = (jnp.stack(l) for l in outs_s)
    return ((y_prompt, y_sample) + tuple(jnp.stack(l) for l in outs_p)
            + (gla_s, ret_s, gdn_s, gdnc_s, ssd_s, ssdc_s))
```

```python
import functools
import math

import numpy as np
import jax
import jax.numpy as jnp
from jax import lax
from jax.experimental import pallas as pl
from jax.experimental.pallas import tpu as pltpu

F32 = jnp.float32
BF16 = jnp.bfloat16
EPS = 1e-6

D_MODEL = 1024
D_PLE = 256
CONV_W = 4
N_BRANCH = 4
BRANCH_W = 512
MIX_H = 4
MIX_DK = 64
MIX_DV = 128
GLA_RANK = 16
GLA_GATE_NORM = 16.0
ROPE_BASE = 10000.0
GDN_CONV_C = 2 * MIX_H * MIX_DK + MIX_H * MIX_DV
SSD_H = 8
SSD_P = 64
SSD_N = 64
SSD_G = 2
SSD_DI = SSD_H * SSD_P
SSD_CONV_C = SSD_DI + 2 * SSD_G * SSD_N
D_FF = 2816
N_EXPERTS = 8
SPLITS = (256, 256, 512, GLA_RANK, 512,
          256, 256, 512, 512,
          GDN_CONV_C, MIX_H, MIX_H, 512,
          SSD_DI, SSD_CONV_C, SSD_H,
          N_BRANCH * D_MODEL)

LANE_GLR = 0
LANE_BETA = 16
LANE_GDA = 20
LANE_SDT = 24
SMALL_W = 128

CHUNK = 64
SEQ_BLOCK_CHUNKS = 16
BATCH_SEQS = 8
CUMSUM_ROWS = 256
ROW_TILE = 512
FF_BLOCKS = 2
VMEM_LIMIT = 56 * 1024 * 1024
PAST_LEN = 16384


def _row_tile(*counts):
    for tm in (ROW_TILE, 256, 128, 64, 32, 16):
        if all(n % tm == 0 for n in counts):
            return tm
    raise ValueError(f"row counts {counts} are not all multiples of 16")


def _rows_specs(parts, tm, layer=None):
    lead = () if layer is None else (layer,)
    blk = (tm, parts[0].shape[-1]) if layer is None else (None, tm, parts[0].shape[-1])
    if len(parts) == 1:
        return [pl.BlockSpec(blk, lambda i: lead + (i, 0))]
    npt = parts[0].shape[-2] // tm
    return [pl.BlockSpec(blk, lambda i: lead + (jnp.minimum(i, npt - 1), 0)),
            pl.BlockSpec(blk, lambda i: lead + (jnp.maximum(i - npt, 0), 0))]


def _rows_read(refs, npt):
    if len(refs) == 1:
        return refs[0][...]
    return jnp.where(pl.program_id(0) < npt, refs[0][...], refs[1][...])


def _mm(a, b):
    return jnp.dot(a, b, preferred_element_type=F32)


def _mm_nt(a, b):
    return lax.dot_general(a, b, (((1,), (1,)), ((), ())), preferred_element_type=F32)


def _mm_tn(a, b):
    return lax.dot_general(a, b, (((0,), (0,)), ((), ())), preferred_element_type=F32)


def _split(x):
    hi = x.astype(BF16)
    lo = (x - hi.astype(F32)).astype(BF16)
    return hi, lo


def _mm_xl(x, m):
    hi, lo = _split(x)
    return _mm(hi, m) + _mm(lo, m)


def _mm_xr(m, x):
    hi, lo = _split(x)
    return _mm(m, hi) + _mm(m, lo)


def _sigmoid(x):
    return 1.0 / (1.0 + jnp.exp(-x))


def _silu(x):
    return x * _sigmoid(x)


def _softplus(x):
    return jnp.maximum(x, 0.0) + jnp.log1p(jnp.exp(-jnp.abs(x)))


def _rms(x, gain):
    ms = jnp.mean(x * x, axis=-1, keepdims=True)
    return x * lax.rsqrt(ms + EPS) * gain


def _log2(n):
    k = int(round(math.log2(n)))
    assert (1 << k) == n, n
    return k


class _Masks:
    def __init__(self, c, ls):
        self.c, self.ls = c, ls
        sh = _log2(ls)
        r = lax.broadcasted_iota(jnp.int32, (c, c), 0)
        q = lax.broadcasted_iota(jnp.int32, (c, c), 1)
        same = (r >> sh) == (q >> sh)
        self.incl = jnp.logical_and(same, q <= r)
        self.strict = jnp.logical_and(same, q < r)
        self.tri = jnp.where(self.incl, 1.0, 0.0).astype(BF16)
        self.tri_t = jnp.where(jnp.logical_and(same, r <= q), 1.0, 0.0).astype(BF16)
        last =((r >> sh) << sh) + (ls - 1)
        self.lastsel = jnp.where(q == last, 1.0, 0.0).astype(BF16)
        self.eye = jnp.where(r == q, 1.0, 0.0).astype(F32)
        self.r, self.q = r, q
        rc = lax.broadcasted_iota(jnp.int32, (c, 1), 0)
        self.seq_of_row = rc >> sh
        self.t_col = (rc & (ls - 1)).astype(F32)
        self.nseq = c // ls

    def rows_of(self, b, x):
        if self.nseq == 1:
            return x
        return jnp.where(self.seq_of_row == b, x, jnp.zeros_like(x))

    def shift_matrix(self, nprev):
        c, ls = self.c, self.ls
        t = self.r & (ls - 1)
        blocks = []
        for s in (1, 2, 3):
            cur = jnp.logical_and(self.q == self.r - s, t >= s)
            prev = jnp.logical_and(self.q == self.r + (ls - s), t < s)
            cur = jnp.where(cur, 1.0, 0.0).astype(BF16)
            prev = jnp.where(prev, 1.0, 0.0).astype(BF16)
            blocks.append(jnp.concatenate([prev] * nprev + [cur], axis=1))
        return jnp.concatenate(blocks, axis=0)


def _expander(lane0, group, width):
    r = lax.broadcasted_iota(jnp.int32, (SMALL_W, width), 0)
    q = lax.broadcasted_iota(jnp.int32, (SMALL_W, width), 1)
    return jnp.where(r == lane0 + (q >> _log2(group)), 1.0, 0.0).astype(BF16)


def _causal_conv(u, prev_parts, w, masks, shift_m):
    c = masks.c
    x = jnp.concatenate(list(prev_parts) + [u], axis=0)
    y = _mm(shift_m, x)
    out = u.astype(F32) * w[3:4, :]
    for s in (1, 2, 3):
        out = out + y[(s - 1) * c:s * c, :] * w[3 - s:4 - s, :]
    return out


def _row_blocks(fn, x):
    n = x.shape[0]
    if n <= CUMSUM_ROWS:
        return fn(x)
    return jnp.concatenate([fn(x[i:i + CUMSUM_ROWS, :]) for i in range(0, n, CUMSUM_ROWS)], axis=0)


def _pair_decay(gc_col, gt_row, incl):
    d = gc_col - gt_row
    return jnp.where(incl, jnp.exp(jnp.where(incl, d, 0.0)), 0.0)


def _seq_prologue(s_scr, prev_scr=None):
    @pl.when(pl.program_id(1) == 0)
    def _():
        s_scr[...] = jnp.zeros_like(s_scr)
        if prev_scr is not None:
            prev_scr[...] = jnp.zeros_like(prev_scr)


def _seq_epilogue(s_scr, s1_ref):
    @pl.when(pl.program_id(1) == pl.num_programs(1) - 1)
    def _():
        s1_ref[0] = s_scr[...].reshape(s1_ref.shape[1:])


def _conv_tail(z_ref, tail_ref, col0, col1):
    @pl.when(pl.program_id(1) == pl.num_programs(1) - 1)
    def _():
        nrows = z_ref.shape[0]
        tail_ref[0] = z_ref[nrows - 8:nrows, col0:col1].astype(F32)


def _in_proj_kernel(*refs, nh, npt):
    h_refs, (g_ref, wa, wb, wc, wd, wg, ws, za, zb, zc, zd, zg, zs) = refs[:nh], refs[nh:]
    xn = _rms(_rows_read(h_refs, npt), g_ref[...]).astype(BF16)
    for w_ref, o_ref in ((wa, za), (wb, zb), (wc, zc), (wd, zd), (wg, zg)):
        width = o_ref.shape[1]
        for j in range(0, width, 512):
            jw = min(512, width - j)
            o_ref[:, j:j + jw] = _mm(xn, w_ref[:, j:j + jw]).astype(o_ref.dtype)
    zs[...] = _mm(xn, ws[...])


def _in_proj(h_parts, gain, wts):
    counts = [p.shape[0] for p in h_parts]
    n = sum(counts)
    tm = _row_tile(*counts)
    wa, wb, wc, wd, wg, ws = wts
    const = lambda i: (0, 0)
    row = lambda i: (i, 0)

    def wspec(w):
        return pl.BlockSpec(w.shape, const, pipeline_mode=pl.Buffered(1))

    outs = [jax.ShapeDtypeStruct((n, w.shape[1]), BF16) for w in (wa, wb, wc, wd, wg)]
    outs.append(jax.ShapeDtypeStruct((n, SMALL_W), F32))
    out_specs = [pl.BlockSpec((tm, w.shape[1]), row) for w in (wa, wb, wc, wd, wg)]
    out_specs.append(pl.BlockSpec((tm, SMALL_W), row))
    return pl.pallas_call(
        functools.partial(_in_proj_kernel, nh=len(h_parts), npt=counts[0] // tm),
        grid=(n // tm,),
        in_specs=_rows_specs(h_parts, tm) + [pl.BlockSpec((1, D_MODEL), const)] + [wspec(w) for w in wts],
        out_specs=out_specs,
        out_shape=outs,
        compiler_params=pltpu.CompilerParams(dimension_semantics=("parallel",), vmem_limit_bytes=VMEM_LIMIT),
        name="in_proj",
    )(*h_parts, gain, *wts)


class _Stack:
    def __init__(self, c, ls):
        self.c, self.ls = c, ls
        self.hs = MIX_H * c
        self.nseq = c // ls
        self.masks = _Masks(self.hs, ls)
        self.head_of_lane = lax.broadcasted_iota(jnp.int32, (1, MIX_H * MIX_DK), 1) >> _log2(MIX_DK)
        self.seq_of_row = (lax.broadcasted_iota(jnp.int32, (self.hs, 1), 0) & (c - 1)) >> _log2(ls)

    def keys(self, x):
        return jnp.concatenate(
            [jnp.where(self.head_of_lane == h, x, jnp.zeros_like(x)) for h in range(MIX_H)], axis=0)

    def values(self, x):
        return jnp.concatenate([x[:, MIX_DV * h:MIX_DV * (h + 1)] for h in range(MIX_H)], axis=0)

    def rows_of(self, b, x):
        if self.nseq == 1:
            return x
        return jnp.where(self.seq_of_row == b, x, jnp.zeros_like(x))


def _write_heads(o_ref, rows, o_stacked, c, gain, gate):
    for h in range(MIX_H):
        vl = slice(MIX_DV * h, MIX_DV * (h + 1))
        y = _rms(o_stacked[h * c:(h + 1) * c, :], gain) * _silu(gate[:, vl])
        o_ref[rows, vl] = y.astype(o_ref.dtype)


def _gla_kernel(*refs, c, cpb, ls, seq_mode):
    if seq_mode:
        z_ref, zs_ref, wgk_ref, bgk_ref, gain_ref, o_ref, s1_ref, s_scr = refs
        s0_ref = None
        _seq_prologue(s_scr)
    else:
        z_ref, zs_ref, wgk_ref, bgk_ref, gain_ref, s0_ref, o_ref, s1_ref = refs
        s_scr = None
    hdk = MIX_H * MIX_DK
    mkb = _Masks(min(cpb * c, CUMSUM_ROWS), ls)
    sk = _Stack(c, ls)
    r = lax.broadcasted_iota(jnp.int32, (hdk, hdk), 0)
    q = lax.broadcasted_iota(jnp.int32, (hdk, hdk), 1)
    eye_k = r == q
    ones_dv = jnp.ones((hdk, MIX_DV), BF16)
    qq = z_ref[:, 0:256].astype(F32) * (MIX_DK ** -0.5)
    kk = z_ref[:, 256:512].astype(F32)
    pre = _mm(zs_ref[...].astype(BF16), wgk_ref[...]) + bgk_ref[...]
    lg = -_softplus(-pre) * (1.0 / GLA_GATE_NORM)
    g = _row_blocks(lambda t: _mm_xr(mkb.tri, t), lg)
    glast = _row_blocks(lambda t: _mm_xr(mkb.lastsel, t), g)
    qe_all = (qq * jnp.exp(g)).astype(BF16)
    ke_all = (kk * jnp.exp(-g)).astype(BF16)
    kd_all = (kk * jnp.exp(glast - g)).astype(BF16)
    el_all = jnp.exp(glast)
    chunks = []
    for ci in range(cpb):
        rows = slice(ci * c, (ci + 1) * c)
        qe = sk.keys(qe_all[rows, :])
        ke = sk.keys(ke_all[rows, :])
        kd = sk.keys(kd_all[rows, :])
        v = sk.values(z_ref[rows, 512:1024])
        a = jnp.where(sk.masks.incl, _mm_nt(qe, ke), 0.0).astype(BF16)
        el = el_all[rows, :]
        decays = []
        for b in range(sk.nseq):
            diag = jnp.where(eye_k, jnp.broadcast_to(el[b * ls:b * ls + 1, :], (hdk, hdk)), 0.0)
            decays.append(_mm_xl(diag, ones_dv))
        chunks.append(dict(rows=rows, qe=qe, kd=kd, v=v, o=_mm(a, v), decays=decays))
    for ch in chunks:
        o = ch["o"]
        for b in range(sk.nseq):
            s = s_scr[...] if seq_mode else s0_ref[b].reshape(hdk, MIX_DV)
            o = o + _mm(sk.rows_of(b, ch["qe"]), s.astype(BF16))
            s_new = s * ch["decays"][b] + _mm_tn(sk.rows_of(b, ch["kd"]), ch["v"])
            if seq_mode:
                s_scr[...] = s_new
            else:
                s1_ref[b] = s_new.reshape(MIX_H, MIX_DK, MIX_DV)
        rows = ch["rows"]
        _write_heads(o_ref, rows, o, c, gain_ref[...], z_ref[rows, 1024:1536].astype(F32))
    if seq_mode:
        _seq_epilogue(s_scr, s1_ref)


def _ret_kernel(*refs, c, cpb, ls, seq_mode):
    if seq_mode:
        z_ref, cos_ref, sin_ref, gain_ref, o_ref, s1_ref, s_scr = refs
        s0_ref = None
        _seq_prologue(s_scr)
    else:
        z_ref, cos_ref, sin_ref, gain_ref, s0_ref, o_ref, s1_ref = refs
        s_scr = None
    hdk = MIX_H * MIX_DK
    sk = _Stack(c, ls)
    ms = sk.masks
    t_col = (lax.broadcasted_iota(jnp.int32, (cpb * c, 1), 0) & (ls - 1)).astype(F32)
    lgam =[math.log(1.0 - 2.0 ** (-5.0 - h)) for h in range(MIX_H)]

    def per_head(idx):
        out = jnp.zeros(idx.shape, F32)
        for h in range(MIX_H):
            out = jnp.where(idx == h, lgam[h], out)
        return out

    lane = lax.broadcasted_iota(jnp.int32, (1, hdk), 1)
    lg_lane = per_head(lane >> _log2(MIX_DK))
    lg_srow = per_head(lax.broadcasted_iota(jnp.int32, (sk.hs, 1), 0) >> _log2(c))
    lg_krow = per_head(lax.broadcasted_iota(jnp.int32, (hdk, 1), 0) >> _log2(MIX_DK))
    first_half = (lane & (MIX_DK - 1)) < MIX_DK // 2
    eg = jnp.exp((t_col + 1.0) * lg_lane)
    ed = jnp.exp((ls - 1.0 - t_col) * lg_lane)
    dt_pos = ((ms.r & (ls - 1)) - (ms.q & (ls - 1))).astype(F32)
    dm = jnp.where(ms.incl, jnp.exp(jnp.where(ms.incl, dt_pos * lg_srow, 0.0)), 0.0)
    dec = jnp.broadcast_to(jnp.exp(lg_krow * float(ls)), (hdk, MIX_DV))

    def rope(x, cs, sn):
        sw = jnp.where(first_half, pltpu.roll(x, hdk - MIX_DK // 2, 1), pltpu.roll(x, MIX_DK // 2, 1))
        return x * cs + sw * sn

    cs = cos_ref[...]
    sn = sin_ref[...]
    qr = rope(z_ref[:, 0:256].astype(F32), cs, sn)
    kr = rope(z_ref[:, 256:512].astype(F32), cs, sn) * (MIX_DK ** -0.5)
    qb_all = qr.astype(BF16)
    kb_all = kr.astype(BF16)
    qe_all = (qr * eg).astype(BF16)
    kd_all = (kr * ed).astype(BF16)
    chunks = []
    for ci in range(cpb):
        rows = slice(ci * c, (ci + 1) * c)
        v = sk.values(z_ref[rows, 512:1024])
        a = (_mm_nt(sk.keys(qb_all[rows, :]), sk.keys(kb_all[rows, :])) * dm).astype(BF16)
        chunks.append(dict(rows=rows, qe=sk.keys(qe_all[rows, :]), kd=sk.keys(kd_all[rows, :]),
                           v=v, o=_mm(a, v)))
    for ch in chunks:
        o = ch["o"]
        for b in range(sk.nseq):
            s = s_scr[...] if seq_mode else s0_ref[b].reshape(hdk, MIX_DV)
            o = o + _mm(sk.rows_of(b, ch["qe"]), s.astype(BF16))
            s_new = s * dec + _mm_tn(sk.rows_of(b, ch["kd"]), ch["v"])
            if seq_mode:
                s_scr[...] = s_new
            else:
                s1_ref[b] = s_new.reshape(MIX_H, MIX_DK, MIX_DV)
        rows = ch["rows"]
        _write_heads(o_ref, rows, o, c, gain_ref[...], z_ref[rows, 1024:1536].astype(F32))
    if seq_mode:
        _seq_epilogue(s_scr, s1_ref)


def _gdn_kernel(*refs, c, cpb, ls, seq_mode):
    if seq_mode:
        z_ref, zs_ref, cw_ref, prow_ref, gain_ref, o_ref, s1_ref, tail_ref, s_scr, prev_scr = refs
        s0_ref = prev_ref = None
        _seq_prologue(s_scr, prev_scr)
        _conv_tail(z_ref, tail_ref, 0, GDN_CONV_C)
    else:
        z_ref, zs_ref, cw_ref, prow_ref, gain_ref, prev_ref, s0_ref, o_ref, s1_ref = refs
        s_scr = prev_scr = None
    hdk = MIX_H * MIX_DK
    mk = _Masks(c, ls)
    mkb = _Masks(min(cpb * c, CUMSUM_ROWS), ls)
    sk = _Stack(c, ls)
    hs, mks = sk.hs, sk.masks
    shift_m = mk.shift_matrix(1 if seq_mode else 2)
    r = lax.broadcasted_iota(jnp.int32, (hdk, hdk), 0)
    q = lax.broadcasted_iota(jnp.int32, (hdk, hdk), 1)
    ones_bd = jnp.where((r >> 6) == (q >> 6), 1.0, 0.0).astype(BF16)
    e_beta_k = _expander(LANE_BETA, MIX_DK, hdk)
    e_beta_v = _expander(LANE_BETA, MIX_DV, MIX_H * MIX_DV)
    e_g_k = _expander(LANE_GDA, MIX_DK, hdk)
    e_g_v = _expander(LANE_GDA, MIX_DV, MIX_H * MIX_DV)
    a_row = -jnp.exp(prow_ref[0:1, :])
    b_row = prow_ref[1:2, :]
    cw = cw_ref[...]
    lane0 = jnp.where(lax.broadcasted_iota(jnp.int32, (hs, SMALL_W), 1) == 0, 1.0, 0.0).astype(BF16)
    stack_k, stack_v, srows_of = sk.keys, sk.values, sk.rows_of

    conv = []
    for ci in range(cpb):
        u = z_ref[ci * c:(ci + 1) * c, 0:GDN_CONV_C]
        if seq_mode:
            prev_parts = [prev_scr[...].astype(BF16)]
        else:
            prev_parts = list(_split(prev_ref[...]))
        conv.append(_causal_conv(u, prev_parts, cw, mk, shift_m))
        if seq_mode:
            prev_scr[...] = u.astype(F32)
    cqkv = _silu(conv[0] if cpb == 1 else jnp.concatenate(conv, axis=0))
    cq = cqkv[:, 0:256]
    ck = cqkv[:, 256:512]
    cv = cqkv[:, 512:1024]
    qn = cq * lax.rsqrt(_mm_xl(cq * cq, ones_bd) + EPS) * (MIX_DK ** -0.5)
    kn = ck * lax.rsqrt(_mm_xl(ck * ck, ones_bd) + EPS)
    zs = zs_ref[...]
    beta = _sigmoid(zs)
    lgd = a_row * _softplus(zs + b_row)
    gc_all = _row_blocks(lambda t: _mm_xr(mkb.tri, t), lgd)
    glast = _row_blocks(lambda t: _mm_xr(mkb.lastsel, t), gc_all)
    beta_k = _mm_xl(beta, e_beta_k)
    beta_v = _mm_xl(beta, e_beta_v)
    eg_k = _mm_xl(jnp.exp(gc_all), e_g_k)
    dl_k = _mm_xl(jnp.exp(glast - gc_all), e_g_k)
    el_v_all = _mm_xl(jnp.exp(glast), e_g_v)
    kbeta = kn * beta_k
    kn_all = kn.astype(BF16)
    kbeta_all = kbeta.astype(BF16)
    qn_all = qn.astype(BF16)
    vbeta_all = (cv * beta_v).astype(BF16)
    kbe_all = (kbeta * eg_k).astype(BF16)
    qe_all = (qn * eg_k).astype(BF16)
    kd_all = (kn * dl_k).astype(BF16)

    chunks = []
    for ci in range(cpb):
        rows = slice(ci * c, (ci + 1) * c)
        gc = gc_all[rows, :]
        k_st = stack_k(kn_all[rows, :])
        g_col = jnp.concatenate([gc[:, LANE_GDA + h:LANE_GDA + h + 1] for h in range(MIX_H)], axis=0)
        g_hi, g_lo = _split(jnp.broadcast_to(g_col, (hs, SMALL_W)))
        g_row = _mm_nt(lane0, g_hi) + _mm_nt(lane0, g_lo)
        dm = _pair_decay(g_col, g_row, mks.incl)
        m = _mm_nt(stack_k(kbeta_all[rows, :]), k_st) * jnp.where(mks.strict, dm, 0.0)
        chunks.append(dict(
            rows=rows, m=m, p=mks.eye - m, mp=m,
            a=(_mm_nt(stack_k(qn_all[rows, :]), k_st) * dm).astype(BF16),
            vbeta=stack_v(vbeta_all[rows, :]),
            kbe=stack_k(kbe_all[rows, :]),
            qe=stack_k(qe_all[rows, :]),
            kd=stack_k(kd_all[rows, :]),
            el_v=el_v_all[rows, :]))
    n = 2
    while n < ls:
        for ch in chunks:
            mpb = ch["mp"].astype(BF16)
            ch["mp"] = _mm(mpb, mpb)
        for ch in chunks:
            ch["p"] = ch["p"] + _mm(ch["p"].astype(BF16), ch["mp"].astype(BF16))
        n *= 2
    for ch in chunks:
        tinv = ch["p"].astype(BF16)
        ch["uu"] = _mm(tinv, ch["vbeta"])
        ch["ww"] = _mm(tinv, ch["kbe"]).astype(BF16)
    for ch in chunks:
        rows = ch["rows"]
        states = []
        vn = ch["uu"]
        for b in range(mk.nseq):
            s = s_scr[...] if seq_mode else s0_ref[b].reshape(hdk, MIX_DV)
            states.append(s)
            vn = vn - _mm(srows_of(b, ch["ww"]), s.astype(BF16))
        vnb = vn.astype(BF16)
        o = _mm(ch["a"], vnb)
        for b in range(mk.nseq):
            s = states[b]
            o = o + _mm(srows_of(b, ch["qe"]), s.astype(BF16))
            dec = jnp.concatenate(
                [jnp.broadcast_to(ch["el_v"][b * ls:b * ls + 1, MIX_DV * h:MIX_DV * (h + 1)], (MIX_DK, MIX_DV))
                 for h in range(MIX_H)], axis=0)
            s_new = s * dec + _mm_tn(srows_of(b, ch["kd"]), vnb)
            if seq_mode:
                s_scr[...] = s_new
            else:
                s1_ref[b] = s_new.reshape(MIX_H, MIX_DK, MIX_DV)
        _write_heads(o_ref, rows, o, c, gain_ref[...], z_ref[rows, GDN_CONV_C:GDN_CONV_C + 512].astype(F32))
    if seq_mode:
        _seq_epilogue(s_scr, s1_ref)


def _ssd_kernel(*refs, c, cpb, ls, seq_mode):
    gw = SSD_DI // SSD_G
    hpg = SSD_H // SSD_G
    if seq_mode:
        z_ref, zs_ref, cw_ref, cb_ref, prow_ref, dx_ref, gain_ref, o_ref, s1_ref, tail_ref, s_scr, prev_scr = refs
        s0_ref = prev_ref = None
        _seq_prologue(s_scr, prev_scr)
        _conv_tail(z_ref, tail_ref, SSD_DI, SSD_DI + SSD_CONV_C)
    else:
        z_ref, zs_ref, cw_ref, cb_ref, prow_ref, dx_ref, gain_ref, prev_ref, s0_ref, o_ref, s1_ref = refs
        s_scr = prev_scr = None
    mk = _Masks(c, ls)
    mkb = _Masks(min(cpb * c, CUMSUM_ROWS), ls)
    shift_m = mk.shift_matrix(1 if seq_mode else 2)
    e_x = _expander(LANE_SDT, SSD_P, SSD_DI)
    a_row = -jnp.exp(prow_ref[0:1, :])
    b_row = prow_ref[1:2, :]
    cw = cw_ref[...]
    lane_g = lax.broadcasted_iota(jnp.int32, (1, gw), 1) >> 6

    def get_state(b, g):
        if seq_mode:
            return s_scr[g]
        return jnp.concatenate([s0_ref[b, g * hpg + j] for j in range(hpg)], axis=-1)

    def put_state(b, g, val):
        if seq_mode:
            s_scr[g] = val
        else:
            for j in range(hpg):
                s1_ref[b, g * hpg + j] = val[:, SSD_P * j:SSD_P * (j + 1)]

    conv = []
    for ci in range(cpb):
        u = z_ref[ci * c:(ci + 1) * c, SSD_DI:SSD_DI + SSD_CONV_C]
        if seq_mode:
            prev_parts = [prev_scr[...].astype(BF16)]
        else:
            prev_parts = list(_split(prev_ref[...]))
        conv.append(_causal_conv(u, prev_parts, cw, mk, shift_m))
        if seq_mode:
            prev_scr[...] = u.astype(F32)
    xbc = _silu((conv[0] if cpb == 1 else jnp.concatenate(conv, axis=0)) + cb_ref[...])
    sx_all = xbc[:, 0:SSD_DI]
    sb_all = xbc[:, SSD_DI:SSD_DI + SSD_G * SSD_N].astype(BF16)
    sc_all = xbc[:, SSD_DI + SSD_G * SSD_N:SSD_CONV_C].astype(BF16)
    dt = _softplus(zs_ref[...] + b_row)
    lsd_all = dt * a_row
    gc_all = _row_blocks(lambda t: _mm_xr(mkb.tri, t), lsd_all)
    glast = _row_blocks(lambda t: _mm_xr(mkb.lastsel, t), gc_all)
    eg_all = _mm_xl(jnp.exp(gc_all), e_x)
    el_all = _mm_xl(jnp.exp(glast), e_x)
    v_all = sx_all * _mm_xl(dt, e_x)
    vd_all = (v_all * _mm_xl(jnp.exp(glast - gc_all), e_x)).astype(BF16)
    skip_all = sx_all * dx_ref[...]
    gate_all = _silu(z_ref[:, 0:SSD_DI].astype(F32))

    for ci in range(cpb):
        rows = slice(ci * c, (ci + 1) * c)
        gc = gc_all[rows, :]
        lsd_hi, lsd_lo = _split(lsd_all[rows, :])
        gt = _mm_tn(lsd_hi, mk.tri_t) + _mm_tn(lsd_lo, mk.tri_t)
        for g in range(SSD_G):
            gl = slice(gw * g, gw * (g + 1))
            cg = sc_all[rows, SSD_N * g:SSD_N * (g + 1)]
            bg = sb_all[rows, SSD_N * g:SSD_N * (g + 1)]
            ag = _mm_nt(cg, bg)
            vg = v_all[rows, gl]
            o = jnp.zeros((c, gw), F32)
            for j in range(hpg):
                lane = LANE_SDT + g * hpg + j
                dm = _pair_decay(gc[:, lane:lane + 1], gt[lane:lane + 1, :], mk.incl)
                vj = jnp.where(lane_g == j, vg, 0.0).astype(BF16)
                o = o + _mm((ag * dm).astype(BF16), vj)
            o_state = jnp.zeros((c, gw), F32)
            for b in range(mk.nseq):
                s = get_state(b, g)
                o_state = o_state + _mm(mk.rows_of(b, cg), s.astype(BF16))
                dec = el_all[ci * c + b * ls:ci * c + b * ls + 1, gl]
                put_state(b, g, s * dec + _mm_tn(mk.rows_of(b, bg), vd_all[rows, gl]))
            y = (o + o_state * eg_all[rows, gl] + skip_all[rows, gl]) * gate_all[rows, gl]
            o_ref[rows, gl] = _rms(y, gain_ref[:, gl]).astype(o_ref.dtype)
    if seq_mode:
        @pl.when(pl.program_id(1) == pl.num_programs(1) - 1)
        def _():
            for g in range(SSD_G):
                for j in range(hpg):
                    s1_ref[0, g * hpg + j] = s_scr[g][:, SSD_P * j:SSD_P * (j + 1)]


def _full_spec(a, grid_rank):
    nd = a.ndim
    if grid_rank == 1:
        return pl.BlockSpec(a.shape, lambda i: (0,) * nd)
    return pl.BlockSpec(a.shape, lambda i, j: (0,) * nd)


def _mixer_seq(kernel_fn, name, z, extra_rows, params, nbatch, seqlen, state_shape, scratch, tail_cols=0):
    c, cpb = CHUNK, SEQ_BLOCK_CHUNKS
    rblk = c * cpb
    nblk = seqlen // rblk
    rowmap = lambda b, j: (b * nblk + j, 0)
    in_specs = [pl.BlockSpec((rblk, z.shape[1]), rowmap)]
    args = [z]
    for a, kind in extra_rows:
        if kind == "rows":
            in_specs.append(pl.BlockSpec((rblk, a.shape[1]), rowmap))
        else:
            in_specs.append(pl.BlockSpec((rblk, a.shape[1]), lambda b, j: (j, 0)))
        args.append(a)
    for a in params:
        in_specs.append(_full_spec(a, 2))
        args.append(a)
    sblk = (1,) + state_shape
    out_shape = [jax.ShapeDtypeStruct((nbatch * seqlen, BRANCH_W), BF16),
                 jax.ShapeDtypeStruct((nbatch,) + state_shape, F32)]
    out_specs = [pl.BlockSpec((rblk, BRANCH_W), rowmap),
                 pl.BlockSpec(sblk, lambda b, j: (b,) + (0,) * len(state_shape))]
    if tail_cols:
        out_shape.append(jax.ShapeDtypeStruct((nbatch, 8, tail_cols), F32))
        out_specs.append(pl.BlockSpec((1, 8, tail_cols), lambda b, j: (b, 0, 0)))
    return pl.pallas_call(
        functools.partial(kernel_fn, c=c, cpb=cpb, ls=c, seq_mode=True),
        grid=(nbatch, nblk),
        in_specs=in_specs,
        out_specs=out_specs,
        out_shape=out_shape,
        scratch_shapes=scratch,
        compiler_params=pltpu.CompilerParams(dimension_semantics=("parallel", "arbitrary"),
                                             vmem_limit_bytes=VMEM_LIMIT),
        name=name,
    )(*args)


def _mixer_batch(kernel_fn, name, z, row_off, extra_rows, params, per_seq, layer, stacked_prev,
                 nbatch, seqlen, state_shape):
    sb = BATCH_SEQS
    c = sb * seqlen
    nsteps = nbatch // sb
    off = row_off // c
    rowmap = lambda i: (off + i, 0)
    in_specs = [pl.BlockSpec((c, z.shape[1]), rowmap)]
    args = [z]
    for a, kind in extra_rows:
        if kind == "rows":
            in_specs.append(pl.BlockSpec((c, a.shape[1]), rowmap))
        else:
            in_specs.append(pl.BlockSpec((c, a.shape[1]), lambda i: (0, 0)))
        args.append(a)
    for a in params:
        in_specs.append(_full_spec(a, 1))
        args.append(a)
    for a in per_seq:
        if a.ndim == 2:
            in_specs.append(pl.BlockSpec((c, a.shape[1]), lambda i: (i, 0)))
        else:
            in_specs.append(pl.BlockSpec((None, sb) + a.shape[2:], lambda i: (layer, i) + (0,) * (a.ndim - 2)))
        args.append(a)
    zeros = (0,) * len(state_shape)
    if layer:
        in_specs.append(pl.BlockSpec((layer, sb) + state_shape, lambda i: (0, i) + zeros))
        args.append(stacked_prev)

    def body(*refs):
        *ins, o_ref, s1_ref = refs
        if layer:
            prev_ref = ins.pop()
            for l in range(layer):
                s1_ref[l] = prev_ref[l]
        kernel_fn(*ins, o_ref, s1_ref.at[layer], c=c, cpb=1, ls=seqlen, seq_mode=False)

    out_shape = [jax.ShapeDtypeStruct((nbatch * seqlen, BRANCH_W), BF16),
                 jax.ShapeDtypeStruct((layer + 1, nbatch) + state_shape, F32)]
    out_specs = [pl.BlockSpec((c, BRANCH_W), lambda i: (i, 0)),
                 pl.BlockSpec((layer + 1, sb) + state_shape, lambda i: (0, i) + zeros)]
    return pl.pallas_call(
        body,
        grid=(nsteps,),
        in_specs=in_specs,
        out_specs=out_specs,
        out_shape=out_shape,
        compiler_params=pltpu.CompilerParams(dimension_semantics=("parallel",), vmem_limit_bytes=VMEM_LIMIT),
        name=name,
    )(*args)


def _merge_kernel(*refs, nh, npt):
    branch_refs, zg = refs[:2 * N_BRANCH], refs[2 * N_BRANCH]
    h_refs, (wbr, wout, out_ref) = refs[2 * N_BRANCH + 1:2 * N_BRANCH + 1 + nh], refs[2 * N_BRANCH + 1 + nh:]
    acc = jnp.zeros(out_ref.shape, F32)
    for n in range(N_BRANCH):
        gate = _sigmoid(zg[:, n * D_MODEL:(n + 1) * D_MODEL].astype(F32))
        acc = acc + gate * _mm(_rows_read(branch_refs[2 * n:2 * n + 2], npt), wbr[n])
    out_ref[...] = _rows_read(h_refs, npt) + _mm(acc.astype(BF16), wout[...])


def _merge(branches, zg, h_parts, wbr, wout):
    n_p, n_s = branches[0][0].shape[0], branches[0][1].shape[0]
    n = n_p + n_s
    tm = _row_tile(n_p, n_s)
    row = lambda i: (i, 0)
    in_specs, args = [], []
    for pair in branches:
        in_specs += _rows_specs(pair, tm)
        args += list(pair)
    in_specs.append(pl.BlockSpec((tm, N_BRANCH * D_MODEL), row))
    in_specs += _rows_specs(h_parts, tm)
    in_specs += [pl.BlockSpec(wbr.shape, lambda i: (0, 0, 0)), pl.BlockSpec(wout.shape, lambda i: (0, 0))]
    return pl.pallas_call(
        functools.partial(_merge_kernel, nh=len(h_parts), npt=n_p // tm),
        grid=(n // tm,),
        in_specs=in_specs,
        out_specs=pl.BlockSpec((tm, D_MODEL), row),
        out_shape=jax.ShapeDtypeStruct((n, D_MODEL), F32),
        compiler_params=pltpu.CompilerParams(dimension_semantics=("parallel",), vmem_limit_bytes=VMEM_LIMIT),
        name="merge",
    )(*args, zg, *h_parts, wbr, wout)


def _ffn_kernel(h_ref, g_ref, wg_ref, wu_ref, wd_ref, out_ref):
    h = h_ref[...]
    u = _rms(h, g_ref[...]).astype(BF16)
    a = (_silu(_mm(u, wg_ref[...])) * _mm(u, wu_ref[...])).astype(BF16)
    out_ref[...] = h + _mm(a, wd_ref[...])


def _ffn(h, gain, wg, wu, wd):
    n = h.shape[0]
    tm = _row_tile(n)
    row = lambda i: (i, 0)
    const = lambda i: (0, 0)
    wspec = lambda w: pl.BlockSpec(w.shape, const, pipeline_mode=pl.Buffered(1))
    return pl.pallas_call(
        _ffn_kernel,
        grid=(n // tm,),
        in_specs=[pl.BlockSpec((tm, D_MODEL), row), pl.BlockSpec((1, D_MODEL), const),
                  wspec(wg), wspec(wu), wspec(wd)],
        out_specs=pl.BlockSpec((tm, D_MODEL), row),
        out_shape=jax.ShapeDtypeStruct((n, D_MODEL), F32),
        compiler_params=pltpu.CompilerParams(dimension_semantics=("parallel",), vmem_limit_bytes=VMEM_LIMIT),
        name="dense_ffn",
    )(h, gain, wg, wu, wd)


MOE_TOKENS = 768
MOE_CAP = 224


def _moe_kernel(h_ref, g_ref, rt_ref, wg_ref, wu_ref, wd_ref, out_ref,
                u_scr, acc_scr, w_scr, sel_scr, rank_scr, selt_scr, rankt_scr, xc_scr, yc_scr, cnt_scr):
    tt = h_ref.shape[0]
    cap = MOE_CAP
    e = pl.program_id(1)
    f = pl.program_id(2)
    nf = pl.num_programs(2)

    @pl.when(jnp.logical_and(e == 0, f == 0))
    def _():
        u = _rms(h_ref[...], g_ref[...])
        u_scr[...] = u.astype(BF16)
        acc_scr[...] = jnp.zeros_like(acc_scr)
        uh, ul = _split(u)
        rh, rl = _split(rt_ref[...])
        logits = _mm(uh, rh) + (_mm(uh, rl) + _mm(ul, rh))
        lane = lax.broadcasted_iota(jnp.int32, logits.shape, 1).astype(F32)
        neg = -3.0e38
        lg = jnp.where(lane < N_EXPERTS, logits, neg)
        m1 = jnp.max(lg, axis=-1, keepdims=True)
        i1 = jnp.min(jnp.where(lg == m1, lane, float(SMALL_W)), axis=-1, keepdims=True)
        lg2 = jnp.where(lane == i1, neg, lg)
        m2 = jnp.max(lg2, axis=-1, keepdims=True)
        i2 = jnp.min(jnp.where(lg2 == m2, lane, float(SMALL_W)), axis=-1, keepdims=True)
        e2 = jnp.exp(m2 - m1)
        w_scr[...] = jnp.where(lane == i1, 1.0 / (1.0 + e2), 0.0) + jnp.where(lane == i2, e2 / (1.0 + e2), 0.0)
        sel = jnp.where(jnp.logical_or(lane == i1, lane == i2), 1.0, 0.0)
        sel_scr[...] = sel
        selb = sel.astype(BF16)
        r = lax.broadcasted_iota(jnp.int32, (tt, tt), 0)
        q = lax.broadcasted_iota(jnp.int32, (tt, tt), 1)
        before = jnp.where(r < q, 1.0, 0.0).astype(BF16)
        ident = jnp.where(r == q, 1.0, 0.0).astype(BF16)
        rank_scr[...] = _mm_tn(before, selb)
        rankt_scr[...] = _mm_tn(selb, before)
        selt_scr[...] = _mm_tn(selb, ident)
        cnt = jnp.sum(sel, axis=0, keepdims=True)
        for ee in range(N_EXPERTS):
            cnt_scr[ee] = jnp.sum(jnp.where(lane[0:1, :] == float(ee), cnt, 0.0)).astype(jnp.int32)

    nsub = (cnt_scr[e] + (cap - 1)) // cap
    lane_w = lax.broadcasted_iota(jnp.int32, (tt, SMALL_W), 1)

    def col_of(ref):
        return jnp.sum(jnp.where(lane_w == e, ref[...], 0.0), axis=-1, keepdims=True)

    @pl.when(f == 0)
    def _():
        rank_row = rankt_scr[pl.ds(e, 1), :]
        sel_row = selt_scr[pl.ds(e, 1), :]
        jcol = lax.broadcasted_iota(jnp.int32, (cap, 1), 0).astype(F32)

        def gather(s, carry):
            base = (s * cap).astype(F32)
            hit = jnp.logical_and(rank_row == jcol + base, sel_row > 0.5)
            onehot = jnp.where(hit, 1.0, 0.0).astype(BF16)
            off = pl.multiple_of(s * cap, 16)
            xc_scr[pl.ds(off, cap), :] = _mm(onehot, u_scr[...]).astype(BF16)
            return carry

        lax.fori_loop(0, nsub, gather, 0)

    def expert(s, carry):
        off = pl.multiple_of(s * cap, 16)
        x = xc_scr[pl.ds(off, cap), :]
        a = (_silu(_mm(x, wg_ref[0])) * _mm(x, wu_ref[0])).astype(BF16)
        y = _mm(a, wd_ref[0])

        @pl.when(f == 0)
        def _():
            yc_scr[pl.ds(off, cap), :] = y

        @pl.when(f != 0)
        def _():
            yc_scr[pl.ds(off, cap), :] += y

        return carry

    lax.fori_loop(0, nsub, expert, 0)

    @pl.when(f == nf - 1)
    def _():
        rank_col = col_of(rank_scr)
        sel_col = col_of(sel_scr)
        w_col = col_of(w_scr)
        jrow = lax.broadcasted_iota(jnp.int32, (1, cap), 1).astype(F32)

        def scatter(s, carry):
            base = (s * cap).astype(F32)
            hit = jnp.logical_and(rank_col == jrow + base, sel_col > 0.5)
            onehot = jnp.where(hit, 1.0, 0.0).astype(BF16)
            off = pl.multiple_of(s * cap, 16)
            acc_scr[...] += w_col * _mm(onehot, yc_scr[pl.ds(off, cap), :].astype(BF16))
            return carry

        lax.fori_loop(0, nsub, scatter, 0)

    @pl.when(jnp.logical_and(e == pl.num_programs(1) - 1, f == nf - 1))
    def _():
        out_ref[...] = h_ref[...] + acc_scr[...]


def _moe(h, gain, router, wg, wu, wd):
    n = h.shape[0]
    tt = MOE_TOKENS if n % MOE_TOKENS == 0 else _row_tile(n)
    ne = wg.shape[0]
    tf = D_FF // FF_BLOCKS
    row = lambda i, e, f: (i, 0)
    nrows = pl.cdiv(tt, MOE_CAP) * MOE_CAP
    in_specs = [pl.BlockSpec((tt, D_MODEL), row), pl.BlockSpec((1, D_MODEL), lambda i, e, f: (0, 0)),
                pl.BlockSpec(router.shape, lambda i, e, f: (0, 0)),
                pl.BlockSpec((1, D_MODEL, tf), lambda i, e, f: (e, 0, f)),
                pl.BlockSpec((1, D_MODEL, tf), lambda i, e, f: (e, 0, f)),
                pl.BlockSpec((1, tf, D_MODEL), lambda i, e, f: (e, f, 0))]
    scratch = [pltpu.VMEM((tt, D_MODEL), BF16), pltpu.VMEM((tt, D_MODEL), F32),
               pltpu.VMEM((tt, SMALL_W), F32), pltpu.VMEM((tt, SMALL_W), F32), pltpu.VMEM((tt, SMALL_W), F32),
               pltpu.VMEM((SMALL_W, tt), F32), pltpu.VMEM((SMALL_W, tt), F32),
               pltpu.VMEM((nrows, D_MODEL), BF16), pltpu.VMEM((nrows, D_MODEL), F32),
               pltpu.SMEM((N_EXPERTS,), jnp.int32)]
    return pl.pallas_call(
        _moe_kernel,
        grid=(n // tt, ne, FF_BLOCKS),
        in_specs=in_specs,
        out_specs=pl.BlockSpec((tt, D_MODEL), row),
        out_shape=jax.ShapeDtypeStruct((n, D_MODEL), F32),
        scratch_shapes=scratch,
        compiler_params=pltpu.CompilerParams(dimension_semantics=("parallel", "arbitrary", "arbitrary"),
                                             vmem_limit_bytes=VMEM_LIMIT),
        name="moe",
    )(h, gain, router, wg, wu, wd)


def _ple_kernel(*refs, final, npt):
    if final:
        h_ref, pp_ref, ps_ref, g_ref, wgate, wproj, gf_ref, yp_ref, ys_ref = refs
    else:
        h_ref, pp_ref, ps_ref, g_ref, wgate, wproj, out_ref = refs
    h = h_ref[...]
    gate = _sigmoid(_mm(_rms(h, g_ref[...]).astype(BF16), wgate[...]))
    h = h + gate * _mm(_rows_read((pp_ref, ps_ref), npt).astype(BF16), wproj[...])
    if final:
        y = _rms(h, gf_ref[...])

        @pl.when(pl.program_id(0) < npt)
        def _():
            yp_ref[...] = y

        @pl.when(pl.program_id(0) >= npt)
        def _():
            ys_ref[...] = y
    else:
        out_ref[...] = h


def _ple(h, p_parts, layer, gain, wgate, wproj, gain_final=None):
    n_p, n_s = p_parts[0].shape[1], p_parts[1].shape[1]
    n = h.shape[0]
    tm = _row_tile(n_p, n_s)
    final = gain_final is not None
    row = lambda i: (i, 0)
    const = lambda i: (0, 0)
    in_specs = [pl.BlockSpec((tm, D_MODEL), row)] + _rows_specs(p_parts, tm, layer)
    in_specs += [pl.BlockSpec((1, D_MODEL), const), pl.BlockSpec(wgate.shape, const), pl.BlockSpec(wproj.shape, const)]
    args = [h, *p_parts, gain, wgate, wproj]
    if final:
        in_specs.append(pl.BlockSpec((1, D_MODEL), const))
        args.append(gain_final)
        out_shape = [jax.ShapeDtypeStruct((n_p, D_MODEL), F32), jax.ShapeDtypeStruct((n_s, D_MODEL), F32)]
        out_specs = _rows_specs(out_shape, tm)
    else:
        out_shape = jax.ShapeDtypeStruct((n, D_MODEL), F32)
        out_specs = pl.BlockSpec((tm, D_MODEL), row)
    return pl.pallas_call(
        functools.partial(_ple_kernel, final=final, npt=n_p // tm),
        grid=(n // tm,),
        in_specs=in_specs,
        out_specs=out_specs,
        out_shape=out_shape,
        compiler_params=pltpu.CompilerParams(dimension_semantics=("arbitrary",), vmem_limit_bytes=VMEM_LIMIT),
        name="ple",
    )(*args)


def _pad_lanes(x, lane0):
    w = x.shape[-1]
    pad = [(0, 0)] * (x.ndim - 1) + [(lane0, SMALL_W - lane0 - w)]
    return jnp.pad(x, pad)


def _in_proj_weights(w):
    cuts = np.cumsum(SPLITS)[:-1].tolist()
    (gq, gk, gv, glr, gg, rq, rk, rv, rg, dqkv, db, da, dg, sz, sxbc, sdt, mg) = jnp.split(w, cuts, axis=-1)
    cat = lambda *xs: jnp.concatenate(xs, axis=-1).astype(BF16)
    small = jnp.concatenate([glr, db, da, sdt], axis=-1)
    small = jnp.pad(small, ((0, 0), (0, SMALL_W - small.shape[1]))).astype(BF16)
    return (cat(gq, gk, gv, gg), cat(rq, rk, rv, rg), cat(dqkv, dg), cat(sz, sxbc), mg.astype(BF16),
            small)


def _rope_tables(pos):
    half = MIX_DK // 2
    inv = ROPE_BASE ** (-jnp.arange(half, dtype=F32) / half)
    ang = jnp.asarray(pos).astype(F32)[:, None] * inv[None, :]
    cos, sin = jnp.cos(ang), jnp.sin(ang)
    cos_t = jnp.tile(jnp.concatenate([cos, cos], axis=-1), (1, MIX_H))
    sin_t = jnp.tile(jnp.concatenate([-sin, sin], axis=-1), (1, MIX_H))
    return cos_t, sin_t


def _conv_prev(state_conv):
    b, _, cdim = state_conv.shape
    return jnp.pad(state_conv, ((0, 0), (1, 0), (0, 0))).reshape(b * CONV_W, cdim)


def kernel(x_prompt, x_sample, state_gla, state_ret, state_gdn, state_gdn_conv, state_ssd, state_ssd_conv, p_prompt, p_sample, norm_mix, w_in, gla_w_gk, gla_b_gk, gla_norm, ret_norm, gdn_conv_w, gdn_a_log, gdn_dt_bias, gdn_norm, ssd_conv_w, ssd_conv_b, ssd_a_log, ssd_dt_bias, ssd_d, ssd_norm, w_branch, w_out, norm_ffn, ffn_w_gate, ffn_w_up, ffn_w_down, moe_router, moe_w_gate, moe_w_up, moe_w_down, norm_ple, ple_w_gate, ple_w_proj, norm_final):
    bp, lp, _ = x_prompt.shape
    bs, lsmp, _ = x_sample.shape
    depth = w_in.shape[0]
    n_p = bp * lp
    n_s = bs * lsmp
    h_parts = (x_prompt.reshape(n_p, D_MODEL), x_sample.reshape(n_s, D_MODEL))
    p_parts = (p_prompt.reshape(depth, n_p, D_PLE), p_sample.reshape(depth, n_s, D_PLE))
    cos_p, sin_p = _rope_tables(np.arange(lp))
    cos_s, sin_s = _rope_tables(PAST_LEN + (np.arange(BATCH_SEQS * lsmp) % lsmp))
    mix_state = (MIX_H, MIX_DK, MIX_DV)
    stacked_state = (MIX_H * MIX_DK, MIX_DV)
    ssd_state = (SSD_H, SSD_N, SSD_P)

    outs_p = [[] for _ in range(6)]
    outs_s = [[] for _ in range(2)]
    gla_s = ret_s = gdn_s = ssd_s = None
    for i in range(depth):
        wts = _in_proj_weights(w_in[i])
        za, zb, zc, zd, zg, zs = _in_proj(h_parts, norm_mix[i][None, :], wts)

        wgk = jnp.pad(gla_w_gk[i], ((LANE_GLR, SMALL_W - LANE_GLR - GLA_RANK), (0, 0))).astype(BF16)
        bgk = gla_b_gk[i][None, :]
        gla_par = [wgk, bgk, gla_norm[i][None, :]]
        oa_p, gla_p = _mixer_seq(_gla_kernel, "gla_seq", za, [(zs, "rows")], gla_par, bp, lp, mix_state,
                                 [pltpu.VMEM(stacked_state, F32)])
        oa_s, gla_s = _mixer_batch(_gla_kernel, "gla_batch", za, n_p, [(zs, "rows")], gla_par,
                                   [state_gla], i, gla_s, bs, lsmp, mix_state)
        ret_par = [ret_norm[i][None, :]]
        ob_p, ret_p = _mixer_seq(_ret_kernel, "ret_seq", zb, [(cos_p, "pos"), (sin_p, "pos")], ret_par,
                                 bp, lp, mix_state, [pltpu.VMEM(stacked_state, F32)])
        ob_s, ret_s = _mixer_batch(_ret_kernel, "ret_batch", zb, n_p, [(cos_s, "pos"), (sin_s, "pos")],
                                   ret_par, [state_ret], i, ret_s, bs, lsmp, mix_state)
        gdn_prow = jnp.concatenate([_pad_lanes(gdn_a_log[i][None, :], LANE_GDA),
                                    _pad_lanes(gdn_dt_bias[i][None, :], LANE_GDA)], axis=0)
        gdn_par = [gdn_conv_w[i], gdn_prow, gdn_norm[i][None, :]]
        oc_p, gdn_p, gdn_tail = _mixer_seq(
            _gdn_kernel, "gdn_seq", zc, [(zs, "rows")], gdn_par, bp, lp, mix_state,
            [pltpu.VMEM(stacked_state, F32), pltpu.VMEM((CHUNK, GDN_CONV_C), F32)], tail_cols=GDN_CONV_C)
        oc_s, gdn_s = _mixer_batch(_gdn_kernel, "gdn_batch", zc, n_p, [(zs, "rows")], gdn_par,
                                   [_conv_prev(state_gdn_conv[i]), state_gdn], i, gdn_s, bs, lsmp, mix_state)
        ssd_prow = jnp.concatenate([_pad_lanes(ssd_a_log[i][None, :], LANE_SDT),
                                    _pad_lanes(ssd_dt_bias[i][None, :], LANE_SDT)], axis=0)
        ssd_par = [ssd_conv_w[i], ssd_conv_b[i][None, :], ssd_prow,
                   jnp.repeat(ssd_d[i], SSD_P)[None, :], ssd_norm[i][None, :]]
        ssd_scr = [pltpu.VMEM((SSD_G, SSD_N, SSD_DI // SSD_G), F32), pltpu.VMEM((CHUNK, SSD_CONV_C), F32)]
        od_p, ssd_p, ssd_tail = _mixer_seq(_ssd_kernel, "ssd_seq", zd, [(zs, "rows")], ssd_par,
                                           bp, lp, ssd_state, ssd_scr, tail_cols=SSD_CONV_C)
        od_s, ssd_s = _mixer_batch(_ssd_kernel, "ssd_batch", zd, n_p, [(zs, "rows")], ssd_par,
                                   [_conv_prev(state_ssd_conv[i]), state_ssd], i, ssd_s, bs, lsmp, ssd_state)

        gdnc_p = gdn_tail[:, 8 - (CONV_W - 1):]
        ssdc_p = ssd_tail[:, 8 - (CONV_W - 1):]
        zc_s = zc[n_p:].reshape(bs, lsmp, -1)[:, :, :GDN_CONV_C].astype(F32)
        zd_s = zd[n_p:].reshape(bs, lsmp, -1)[:, :, SSD_DI:].astype(F32)
        gdnc_s = jnp.concatenate([state_gdn_conv[i], zc_s], axis=1)[:, lsmp:]
        ssdc_s = jnp.concatenate([state_ssd_conv[i], zd_s], axis=1)[:, lsmp:]
        for lst, val in zip(outs_p, (gla_p, ret_p, gdn_p, gdnc_p, ssd_p, ssdc_p)):
            lst.append(val)
        for lst, val in zip(outs_s, (gdnc_s, ssdc_s)):
            lst.append(val)

        branches = ((oa_p, oa_s), (ob_p, ob_s), (oc_p, oc_s), (od_p, od_s))
        h = _merge(branches, zg, h_parts, w_branch[i].astype(BF16), w_out[i].astype(BF16))

        j = i // 2
        if i % 2 == 0:
            h = _ffn(h, norm_ffn[i][None, :], ffn_w_gate[j].astype(BF16), ffn_w_up[j].astype(BF16),
                     ffn_w_down[j].astype(BF16))
        else:
            router = jnp.pad(moe_router[j], ((0, 0), (0, SMALL_W - N_EXPERTS)))
            h = _moe(h, norm_ffn[i][None, :], router, moe_w_gate[j].astype(BF16), moe_w_up[j].astype(BF16),
                     moe_w_down[j].astype(BF16))
        h = _ple(h, p_parts, i, norm_ple[i][None, :], ple_w_gate[i].astype(BF16), ple_w_proj[i].astype(BF16),
                 gain_final=norm_final[None, :] if i == depth - 1 else None)
        h_parts = (h,)

    y_prompt = h[0].reshape(bp, lp, D_MODEL)
    y_sample = h[1].reshape(bs, lsmp, D_MODEL)
    gdnc_s, ssdc_s = (jnp.stack(l) for l in outs_s)
    return ((y_prompt, y_sample) + tuple(jnp.stack(l) for l in outs_p)
            + (gla_s, ret_s, gdn_s, gdnc_s, ssd_s, ssdc_s))
```

```python
import functools
import math

import numpy as np
import jax
import jax.numpy as jnp
from jax import lax
from jax.experimental import pallas as pl
from jax.experimental.pallas import tpu as pltpu

F32 = jnp.float32
BF16 = jnp.bfloat16
EPS = 1e-6

D_MODEL = 1024
D_PLE = 256
CONV_W = 4
N_BRANCH = 4
BRANCH_W = 512
MIX_H = 4
MIX_DK = 64
MIX_DV = 128
GLA_RANK = 16
GLA_GATE_NORM = 16.0
ROPE_BASE = 10000.0
GDN_CONV_C = 2 * MIX_H * MIX_DK + MIX_H * MIX_DV
SSD_H = 8
SSD_P = 64
SSD_N = 64
SSD_G = 2
SSD_DI = SSD_H * SSD_P
SSD_CONV_C = SSD_DI + 2 * SSD_G * SSD_N
D_FF = 2816
N_EXPERTS = 8
SPLITS = (256, 256, 512, GLA_RANK, 512,
          256, 256, 512, 512,
          GDN_CONV_C, MIX_H, MIX_H, 512,
          SSD_DI, SSD_CONV_C, SSD_H,
          N_BRANCH * D_MODEL)

LANE_GLR = 0
LANE_BETA = 16
LANE_GDA = 20
LANE_SDT = 24
SMALL_W = 128

CHUNK = 64
SEQ_BLOCK_CHUNKS = 16
BATCH_SEQS = 8
CUMSUM_ROWS = 256
ROW_TILE = 512
FF_BLOCKS = 2
VMEM_LIMIT = 56 * 1024 * 1024
PAST_LEN = 16384


def _row_tile(*counts):
    for tm in (ROW_TILE, 256, 128, 64, 32, 16):
        if all(n % tm == 0 for n in counts):
            return tm
    raise ValueError(f"row counts {counts} are not all multiples of 16")


def _rows_specs(parts, tm, layer=None):
    lead = () if layer is None else (layer,)
    blk = (tm, parts[0].shape[-1]) if layer is None else (None, tm, parts[0].shape[-1])
    if len(parts) == 1:
        return [pl.BlockSpec(blk, lambda i: lead + (i, 0))]
    npt = parts[0].shape[-2] // tm
    return [pl.BlockSpec(blk, lambda i: lead + (jnp.minimum(i, npt - 1), 0)),
            pl.BlockSpec(blk, lambda i: lead + (jnp.maximum(i - npt, 0), 0))]


def _rows_read(refs, npt):
    if len(refs) == 1:
        return refs[0][...]
    return jnp.where(pl.program_id(0) < npt, refs[0][...], refs[1][...])


def _mm(a, b):
    return jnp.dot(a, b, preferred_element_type=F32)


def _mm_nt(a, b):
    return lax.dot_general(a, b, (((1,), (1,)), ((), ())), preferred_element_type=F32)


def _mm_tn(a, b):
    return lax.dot_general(a, b, (((0,), (0,)), ((), ())), preferred_element_type=F32)


def _split(x):
    hi = x.astype(BF16)
    lo = (x - hi.astype(F32)).astype(BF16)
    return hi, lo


def _mm_xl(x, m):
    hi, lo = _split(x)
    return _mm(hi, m) + _mm(lo, m)


def _mm_xr(m, x):
    hi, lo = _split(x)
    return _mm(m, hi) + _mm(m, lo)


def _sigmoid(x):
    return 1.0 / (1.0 + jnp.exp(-x))


def _silu(x):
    return x * _sigmoid(x)


def _softplus(x):
    return jnp.maximum(x, 0.0) + jnp.log1p(jnp.exp(-jnp.abs(x)))


def _rms(x, gain):
    ms = jnp.mean(x * x, axis=-1, keepdims=True)
    return x * lax.rsqrt(ms + EPS) * gain


def _log2(n):
    k = int(round(math.log2(n)))
    assert (1 << k) == n, n
    return k


class _Masks:
    def __init__(self, c, ls):
        self.c, self.ls = c, ls
        sh = _log2(ls)
        r = lax.broadcasted_iota(jnp.int32, (c, c), 0)
        q = lax.broadcasted_iota(jnp.int32, (c, c), 1)
        same = (r >> sh) == (q >> sh)
        self.incl = jnp.logical_and(same, q <= r)
        self.strict = jnp.logical_and(same, q < r)
        self.tri = jnp.where(self.incl, 1.0, 0.0).astype(BF16)
        self.tri_t = jnp.where(jnp.logical_and(same, r <= q), 1.0, 0.0).astype(BF16)
        last =((r >> sh) << sh) + (ls - 1)
        self.lastsel = jnp.where(q == last, 1.0, 0.0).astype(BF16)
        self.midsel = jnp.where(q == last - ls // 2, 1.0, 0.0).astype(BF16)
        self.eye = jnp.where(r == q, 1.0, 0.0).astype(F32)
        self.r, self.q = r, q
        rc = lax.broadcasted_iota(jnp.int32, (c, 1), 0)
        self.seq_of_row = rc >> sh
        self.t_col = (rc & (ls - 1)).astype(F32)
        self.nseq = c // ls

    def rows_of(self, b, x):
        if self.nseq == 1:
            return x
        return jnp.where(self.seq_of_row == b, x, jnp.zeros_like(x))

    def shift_matrix(self, nprev):
        c, ls = self.c, self.ls
        t = self.r & (ls - 1)
        blocks = []
        for s in (1, 2, 3):
            cur = jnp.logical_and(self.q == self.r - s, t >= s)
            prev = jnp.logical_and(self.q == self.r + (ls - s), t < s)
            cur = jnp.where(cur, 1.0, 0.0).astype(BF16)
            prev = jnp.where(prev, 1.0, 0.0).astype(BF16)
            blocks.append(jnp.concatenate([prev] * nprev + [cur], axis=1))
        return jnp.concatenate(blocks, axis=0)


def _expander(lane0, group, width):
    r = lax.broadcasted_iota(jnp.int32, (SMALL_W, width), 0)
    q = lax.broadcasted_iota(jnp.int32, (SMALL_W, width), 1)
    return jnp.where(r == lane0 + (q >> _log2(group)), 1.0, 0.0).astype(BF16)


def _causal_conv(u, prev_parts, w, masks, shift_m):
    c = masks.c
    x = jnp.concatenate(list(prev_parts) + [u], axis=0)
    y = _mm(shift_m, x)
    out = u.astype(F32) * w[3:4, :]
    for s in (1, 2, 3):
        out = out + y[(s - 1) * c:s * c, :] * w[3 - s:4 - s, :]
    return out


def _row_blocks(fn, x):
    n = x.shape[0]
    if n <= CUMSUM_ROWS:
        return fn(x)
    return jnp.concatenate([fn(x[i:i + CUMSUM_ROWS, :]) for i in range(0, n, CUMSUM_ROWS)], axis=0)


def _pair_decay(gc_col, gt_row, incl):
    d = gc_col - gt_row
    return jnp.where(incl, jnp.exp(jnp.where(incl, d, 0.0)), 0.0)


def _seq_prologue(s_scr, prev_scr=None):
    @pl.when(pl.program_id(1) == 0)
    def _():
        s_scr[...] = jnp.zeros_like(s_scr)
        if prev_scr is not None:
            prev_scr[...] = jnp.zeros_like(prev_scr)


def _seq_epilogue(s_scr, s1_ref):
    @pl.when(pl.program_id(1) == pl.num_programs(1) - 1)
    def _():
        s1_ref[0] = s_scr[...].reshape(s1_ref.shape[1:])


def _conv_tail(z_ref, tail_ref, col0, col1):
    @pl.when(pl.program_id(1) == pl.num_programs(1) - 1)
    def _():
        nrows = z_ref.shape[0]
        tail_ref[0] = z_ref[nrows - 8:nrows, col0:col1].astype(F32)


def _in_proj_kernel(*refs, nh, npt):
    h_refs, (g_ref, wa, wb, wc, wd, wg, ws, za, zb, zc, zd, zg, zs) = refs[:nh], refs[nh:]
    xn = _rms(_rows_read(h_refs, npt), g_ref[...]).astype(BF16)
    for w_ref, o_ref in ((wa, za), (wb, zb), (wc, zc), (wd, zd), (wg, zg)):
        width = o_ref.shape[1]
        for j in range(0, width, 512):
            jw = min(512, width - j)
            o_ref[:, j:j + jw] = _mm(xn, w_ref[:, j:j + jw]).astype(o_ref.dtype)
    zs[...] = _mm(xn, ws[...])


def _in_proj(h_parts, gain, wts):
    counts = [p.shape[0] for p in h_parts]
    n = sum(counts)
    tm = _row_tile(*counts)
    wa, wb, wc, wd, wg, ws = wts
    const = lambda i: (0, 0)
    row = lambda i: (i, 0)

    def wspec(w):
        return pl.BlockSpec(w.shape, const, pipeline_mode=pl.Buffered(1))

    outs = [jax.ShapeDtypeStruct((n, w.shape[1]), BF16) for w in (wa, wb, wc, wd, wg)]
    outs.append(jax.ShapeDtypeStruct((n, SMALL_W), F32))
    out_specs = [pl.BlockSpec((tm, w.shape[1]), row) for w in (wa, wb, wc, wd, wg)]
    out_specs.append(pl.BlockSpec((tm, SMALL_W), row))
    return pl.pallas_call(
        functools.partial(_in_proj_kernel, nh=len(h_parts), npt=counts[0] // tm),
        grid=(n // tm,),
        in_specs=_rows_specs(h_parts, tm) + [pl.BlockSpec((1, D_MODEL), const)] + [wspec(w) for w in wts],
        out_specs=out_specs,
        out_shape=outs,
        compiler_params=pltpu.CompilerParams(dimension_semantics=("parallel",), vmem_limit_bytes=VMEM_LIMIT),
        name="in_proj",
    )(*h_parts, gain, *wts)


class _Stack:
    def __init__(self, c, ls):
        self.c, self.ls = c, ls
        self.hs = MIX_H * c
        self.nseq = c // ls
        self.masks = _Masks(self.hs, ls)
        self.head_of_lane = lax.broadcasted_iota(jnp.int32, (1, MIX_H * MIX_DK), 1) >> _log2(MIX_DK)
        self.seq_of_row = (lax.broadcasted_iota(jnp.int32, (self.hs, 1), 0) & (c - 1)) >> _log2(ls)

    def keys(self, x):
        return jnp.concatenate(
            [jnp.where(self.head_of_lane == h, x, jnp.zeros_like(x)) for h in range(MIX_H)], axis=0)

    def values(self, x):
        return jnp.concatenate([x[:, MIX_DV * h:MIX_DV * (h + 1)] for h in range(MIX_H)], axis=0)

    def rows_of(self, b, x):
        if self.nseq == 1:
            return x
        return jnp.where(self.seq_of_row == b, x, jnp.zeros_like(x))


def _write_heads(o_ref, rows, o_stacked, c, gain, gate):
    for h in range(MIX_H):
        vl = slice(MIX_DV * h, MIX_DV * (h + 1))
        y = _rms(o_stacked[h * c:(h + 1) * c, :], gain) * _silu(gate[:, vl])
        o_ref[rows, vl] = y.astype(o_ref.dtype)


def _gla_kernel(*refs, c, cpb, ls, seq_mode):
    if seq_mode:
        z_ref, zs_ref, wgk_ref, bgk_ref, gain_ref, o_ref, s1_ref, s_scr = refs
        s0_ref = None
        _seq_prologue(s_scr)
    else:
        z_ref, zs_ref, wgk_ref, bgk_ref, gain_ref, s0_ref, o_ref, s1_ref = refs
        s_scr = None
    hdk = MIX_H * MIX_DK
    mkb = _Masks(min(cpb * c, CUMSUM_ROWS), ls)
    sk = _Stack(c, ls)
    r = lax.broadcasted_iota(jnp.int32, (hdk, hdk), 0)
    q = lax.broadcasted_iota(jnp.int32, (hdk, hdk), 1)
    eye_k = r == q
    ones_dv = jnp.ones((hdk, MIX_DV), BF16)
    qq = z_ref[:, 0:256].astype(F32) * (MIX_DK ** -0.5)
    kk = z_ref[:, 256:512].astype(F32)
    pre = _mm(zs_ref[...].astype(BF16), wgk_ref[...]) + bgk_ref[...]
    lg = -_softplus(-pre) * (1.0 / GLA_GATE_NORM)
    g = _row_blocks(lambda t: _mm_xr(mkb.tri, t), lg)
    glast = _row_blocks(lambda t: _mm_xr(mkb.lastsel, t), g)
    gmid = _row_blocks(lambda t: _mm_xr(mkb.midsel, t), g)
    qa_all = (qq * jnp.exp(g - gmid)).astype(BF16)
    ka_all = (kk * jnp.exp(gmid - g)).astype(BF16)
    qe_all = (qq * jnp.exp(g)).astype(BF16)
    kd_all = (kk * jnp.exp(glast - g)).astype(BF16)
    el_all = jnp.exp(glast)
    chunks = []
    for ci in range(cpb):
        rows = slice(ci * c, (ci + 1) * c)
        qe = sk.keys(qe_all[rows, :])
        kd = sk.keys(kd_all[rows, :])
        v = sk.values(z_ref[rows, 512:1024])
        a = jnp.where(sk.masks.incl, _mm_nt(sk.keys(qa_all[rows, :]), sk.keys(ka_all[rows, :])), 0.0).astype(BF16)
        el = el_all[rows, :]
        decays = []
        for b in range(sk.nseq):
            diag = jnp.where(eye_k, jnp.broadcast_to(el[b * ls:b * ls + 1, :], (hdk, hdk)), 0.0)
            decays.append(_mm_xl(diag, ones_dv))
        chunks.append(dict(rows=rows, qe=qe, kd=kd, v=v, o=_mm(a, v), decays=decays))
    for ch in chunks:
        o = ch["o"]
        for b in range(sk.nseq):
            s = s_scr[...] if seq_mode else s0_ref[b].reshape(hdk, MIX_DV)
            o = o + _mm(sk.rows_of(b, ch["qe"]), s.astype(BF16))
            s_new = s * ch["decays"][b] + _mm_tn(sk.rows_of(b, ch["kd"]), ch["v"])
            if seq_mode:
                s_scr[...] = s_new
            else:
                s1_ref[b] = s_new.reshape(MIX_H, MIX_DK, MIX_DV)
        rows = ch["rows"]
        _write_heads(o_ref, rows, o, c, gain_ref[...], z_ref[rows, 1024:1536].astype(F32))
    if seq_mode:
        _seq_epilogue(s_scr, s1_ref)


def _ret_kernel(*refs, c, cpb, ls, seq_mode):
    if seq_mode:
        z_ref, cos_ref, sin_ref, gain_ref, o_ref, s1_ref, s_scr = refs
        s0_ref = None
        _seq_prologue(s_scr)
    else:
        z_ref, cos_ref, sin_ref, gain_ref, s0_ref, o_ref, s1_ref = refs
        s_scr = None
    hdk = MIX_H * MIX_DK
    sk = _Stack(c, ls)
    ms = sk.masks
    t_col = (lax.broadcasted_iota(jnp.int32, (cpb * c, 1), 0) & (ls - 1)).astype(F32)
    lgam =[math.log(1.0 - 2.0 ** (-5.0 - h)) for h in range(MIX_H)]

    def per_head(idx):
        out = jnp.zeros(idx.shape, F32)
        for h in range(MIX_H):
            out = jnp.where(idx == h, lgam[h], out)
        return out

    lane = lax.broadcasted_iota(jnp.int32, (1, hdk), 1)
    lg_lane = per_head(lane >> _log2(MIX_DK))
    lg_srow = per_head(lax.broadcasted_iota(jnp.int32, (sk.hs, 1), 0) >> _log2(c))
    lg_krow = per_head(lax.broadcasted_iota(jnp.int32, (hdk, 1), 0) >> _log2(MIX_DK))
    first_half = (lane & (MIX_DK - 1)) < MIX_DK // 2
    eg = jnp.exp((t_col + 1.0) * lg_lane)
    ed = jnp.exp((ls - 1.0 - t_col) * lg_lane)
    dt_pos = ((ms.r & (ls - 1)) - (ms.q & (ls - 1))).astype(F32)
    dm = jnp.where(ms.incl, jnp.exp(jnp.where(ms.incl, dt_pos * lg_srow, 0.0)), 0.0)
    dec = jnp.broadcast_to(jnp.exp(lg_krow * float(ls)), (hdk, MIX_DV))

    def rope(x, cs, sn):
        sw = jnp.where(first_half, pltpu.roll(x, hdk - MIX_DK // 2, 1), pltpu.roll(x, MIX_DK // 2, 1))
        return x * cs + sw * sn

    cs = cos_ref[...]
    sn = sin_ref[...]
    qr = rope(z_ref[:, 0:256].astype(F32), cs, sn)
    kr = rope(z_ref[:, 256:512].astype(F32), cs, sn) * (MIX_DK ** -0.5)
    qb_all = qr.astype(BF16)
    kb_all = kr.astype(BF16)
    qe_all = (qr * eg).astype(BF16)
    kd_all = (kr * ed).astype(BF16)
    chunks = []
    for ci in range(cpb):
        rows = slice(ci * c, (ci + 1) * c)
        v = sk.values(z_ref[rows, 512:1024])
        a = (_mm_nt(sk.keys(qb_all[rows, :]), sk.keys(kb_all[rows, :])) * dm).astype(BF16)
        chunks.append(dict(rows=rows, qe=sk.keys(qe_all[rows, :]), kd=sk.keys(kd_all[rows, :]),
                           v=v, o=_mm(a, v)))
    for ch in chunks:
        o = ch["o"]
        for b in range(sk.nseq):
            s = s_scr[...] if seq_mode else s0_ref[b].reshape(hdk, MIX_DV)
            o = o + _mm(sk.rows_of(b, ch["qe"]), s.astype(BF16))
            s_new = s * dec + _mm_tn(sk.rows_of(b, ch["kd"]), ch["v"])
            if seq_mode:
                s_scr[...] = s_new
            else:
                s1_ref[b] = s_new.reshape(MIX_H, MIX_DK, MIX_DV)
        rows = ch["rows"]
        _write_heads(o_ref, rows, o, c, gain_ref[...], z_ref[rows, 1024:1536].astype(F32))
    if seq_mode:
        _seq_epilogue(s_scr, s1_ref)


def _gdn_kernel(*refs, c, cpb, ls, seq_mode):
    if seq_mode:
        z_ref, zs_ref, cw_ref, prow_ref, gain_ref, o_ref, s1_ref, tail_ref, s_scr, prev_scr = refs
        s0_ref = prev_ref = None
        _seq_prologue(s_scr, prev_scr)
        _conv_tail(z_ref, tail_ref, 0, GDN_CONV_C)
    else:
        z_ref, zs_ref, cw_ref, prow_ref, gain_ref, prev_ref, s0_ref, o_ref, s1_ref = refs
        s_scr = prev_scr = None
    hdk = MIX_H * MIX_DK
    mk = _Masks(c, ls)
    mkb = _Masks(min(cpb * c, CUMSUM_ROWS), ls)
    sk = _Stack(c, ls)
    hs, mks = sk.hs, sk.masks
    shift_m = mk.shift_matrix(1 if seq_mode else 2)
    r = lax.broadcasted_iota(jnp.int32, (hdk, hdk), 0)
    q = lax.broadcasted_iota(jnp.int32, (hdk, hdk), 1)
    ones_bd = jnp.where((r >> 6) == (q >> 6), 1.0, 0.0).astype(BF16)
    e_beta_k = _expander(LANE_BETA, MIX_DK, hdk)
    e_beta_v = _expander(LANE_BETA, MIX_DV, MIX_H * MIX_DV)
    e_g_k = _expander(LANE_GDA, MIX_DK, hdk)
    e_g_v = _expander(LANE_GDA, MIX_DV, MIX_H * MIX_DV)
    a_row = -jnp.exp(prow_ref[0:1, :])
    b_row = prow_ref[1:2, :]
    cw = cw_ref[...]
    lane0 = jnp.where(lax.broadcasted_iota(jnp.int32, (hs, SMALL_W), 1) == 0, 1.0, 0.0).astype(BF16)
    stack_k, stack_v, srows_of = sk.keys, sk.values, sk.rows_of

    conv = []
    for ci in range(cpb):
        u = z_ref[ci * c:(ci + 1) * c, 0:GDN_CONV_C]
        if seq_mode:
            prev_parts = [prev_scr[...].astype(BF16)]
        else:
            prev_parts = list(_split(prev_ref[...]))
        conv.append(_causal_conv(u, prev_parts, cw, mk, shift_m))
        if seq_mode:
            prev_scr[...] = u.astype(F32)
    cqkv = _silu(conv[0] if cpb == 1 else jnp.concatenate(conv, axis=0))
    cq = cqkv[:, 0:256]
    ck = cqkv[:, 256:512]
    cv = cqkv[:, 512:1024]
    qn = cq * lax.rsqrt(_mm_xl(cq * cq, ones_bd) + EPS) * (MIX_DK ** -0.5)
    kn = ck * lax.rsqrt(_mm_xl(ck * ck, ones_bd) + EPS)
    zs = zs_ref[...]
    beta = _sigmoid(zs)
    lgd = a_row * _softplus(zs + b_row)
    gc_all = _row_blocks(lambda t: _mm_xr(mkb.tri, t), lgd)
    glast = _row_blocks(lambda t: _mm_xr(mkb.lastsel, t), gc_all)
    beta_k = _mm_xl(beta, e_beta_k)
    beta_v = _mm_xl(beta, e_beta_v)
    eg_k = _mm_xl(jnp.exp(gc_all), e_g_k)
    dl_k = _mm_xl(jnp.exp(glast - gc_all), e_g_k)
    el_v_all = _mm_xl(jnp.exp(glast), e_g_v)
    kbeta = kn * beta_k
    kn_all = kn.astype(BF16)
    kbeta_all = kbeta.astype(BF16)
    qn_all = qn.astype(BF16)
    vbeta_all = (cv * beta_v).astype(BF16)
    kbe_all = (kbeta * eg_k).astype(BF16)
    qe_all = (qn * eg_k).astype(BF16)
    kd_all = (kn * dl_k).astype(BF16)

    chunks = []
    for ci in range(cpb):
        rows = slice(ci * c, (ci + 1) * c)
        gc = gc_all[rows, :]
        k_st = stack_k(kn_all[rows, :])
        g_col = jnp.concatenate([gc[:, LANE_GDA + h:LANE_GDA + h + 1] for h in range(MIX_H)], axis=0)
        g_hi, g_lo = _split(jnp.broadcast_to(g_col, (hs, SMALL_W)))
        g_row = _mm_nt(lane0, g_hi) + _mm_nt(lane0, g_lo)
        dm = _pair_decay(g_col, g_row, mks.incl)
        m = _mm_nt(stack_k(kbeta_all[rows, :]), k_st) * jnp.where(mks.strict, dm, 0.0)
        chunks.append(dict(
            rows=rows, m=m, p=mks.eye - m, mp=m,
            a=(_mm_nt(stack_k(qn_all[rows, :]), k_st) * dm).astype(BF16),
            vbeta=stack_v(vbeta_all[rows, :]),
            kbe=stack_k(kbe_all[rows, :]),
            qe=stack_k(qe_all[rows, :]),
            kd=stack_k(kd_all[rows, :]),
            el_v=el_v_all[rows, :]))
    n = 2
    while n < ls:
        for ch in chunks:
            mpb = ch["mp"].astype(BF16)
            ch["mp"] = _mm(mpb, mpb)
        for ch in chunks:
            ch["p"] = ch["p"] + _mm(ch["p"].astype(BF16), ch["mp"].astype(BF16))
        n *= 2
    for ch in chunks:
        tinv = ch["p"].astype(BF16)
        ch["uu"] = _mm(tinv, ch["vbeta"])
        ch["ww"] = _mm(tinv, ch["kbe"]).astype(BF16)
    for ch in chunks:
        rows = ch["rows"]
        states = []
        vn = ch["uu"]
        for b in range(mk.nseq):
            s = s_scr[...] if seq_mode else s0_ref[b].reshape(hdk, MIX_DV)
            states.append(s)
            vn = vn - _mm(srows_of(b, ch["ww"]), s.astype(BF16))
        vnb = vn.astype(BF16)
        o = _mm(ch["a"], vnb)
        for b in range(mk.nseq):
            s = states[b]
            o = o + _mm(srows_of(b, ch["qe"]), s.astype(BF16))
            dec = jnp.concatenate(
                [jnp.broadcast_to(ch["el_v"][b * ls:b * ls + 1, MIX_DV * h:MIX_DV * (h + 1)], (MIX_DK, MIX_DV))
                 for h in range(MIX_H)], axis=0)
            s_new = s * dec + _mm_tn(srows_of(b, ch["kd"]), vnb)
            if seq_mode:
                s_scr[...] = s_new
            else:
                s1_ref[b] = s_new.reshape(MIX_H, MIX_DK, MIX_DV)
        _write_heads(o_ref, rows, o, c, gain_ref[...], z_ref[rows, GDN_CONV_C:GDN_CONV_C + 512].astype(F32))
    if seq_mode:
        _seq_epilogue(s_scr, s1_ref)


def _ssd_kernel(*refs, c, cpb, ls, seq_mode):
    gw = SSD_DI // SSD_G
    hpg = SSD_H // SSD_G
    if seq_mode:
        z_ref, zs_ref, cw_ref, cb_ref, prow_ref, dx_ref, gain_ref, o_ref, s1_ref, tail_ref, s_scr, prev_scr = refs
        s0_ref = prev_ref = None
        _seq_prologue(s_scr, prev_scr)
        _conv_tail(z_ref, tail_ref, SSD_DI, SSD_DI + SSD_CONV_C)
    else:
        z_ref, zs_ref, cw_ref, cb_ref, prow_ref, dx_ref, gain_ref, prev_ref, s0_ref, o_ref, s1_ref = refs
        s_scr = prev_scr = None
    mk = _Masks(c, ls)
    mkb = _Masks(min(cpb * c, CUMSUM_ROWS), ls)
    shift_m = mk.shift_matrix(1 if seq_mode else 2)
    e_x = _expander(LANE_SDT, SSD_P, SSD_DI)
    a_row = -jnp.exp(prow_ref[0:1, :])
    b_row = prow_ref[1:2, :]
    cw = cw_ref[...]
    lane_g = lax.broadcasted_iota(jnp.int32, (1, gw), 1) >> 6

    def get_state(b, g):
        if seq_mode:
            return s_scr[g]
        return jnp.concatenate([s0_ref[b, g * hpg + j] for j in range(hpg)], axis=-1)

    def put_state(b, g, val):
        if seq_mode:
            s_scr[g] = val
        else:
            for j in range(hpg):
                s1_ref[b, g * hpg + j] = val[:, SSD_P * j:SSD_P * (j + 1)]

    conv = []
    for ci in range(cpb):
        u = z_ref[ci * c:(ci + 1) * c, SSD_DI:SSD_DI + SSD_CONV_C]
        if seq_mode:
            prev_parts = [prev_scr[...].astype(BF16)]
        else:
            prev_parts = list(_split(prev_ref[...]))
        conv.append(_causal_conv(u, prev_parts, cw, mk, shift_m))
        if seq_mode:
            prev_scr[...] = u.astype(F32)
    xbc = _silu((conv[0] if cpb == 1 else jnp.concatenate(conv, axis=0)) + cb_ref[...])
    sx_all = xbc[:, 0:SSD_DI]
    sb_all = xbc[:, SSD_DI:SSD_DI + SSD_G * SSD_N].astype(BF16)
    sc_all = xbc[:, SSD_DI + SSD_G * SSD_N:SSD_CONV_C].astype(BF16)
    dt = _softplus(zs_ref[...] + b_row)
    lsd_all = dt * a_row
    gc_all = _row_blocks(lambda t: _mm_xr(mkb.tri, t), lsd_all)
    glast = _row_blocks(lambda t: _mm_xr(mkb.lastsel, t), gc_all)
    eg_all = _mm_xl(jnp.exp(gc_all), e_x)
    el_all = _mm_xl(jnp.exp(glast), e_x)
    v_all = sx_all * _mm_xl(dt, e_x)
    vd_all = (v_all * _mm_xl(jnp.exp(glast - gc_all), e_x)).astype(BF16)
    skip_all = sx_all * dx_ref[...]
    gate_all = _silu(z_ref[:, 0:SSD_DI].astype(F32))

    for ci in range(cpb):
        rows = slice(ci * c, (ci + 1) * c)
        gc = gc_all[rows, :]
        lsd_hi, lsd_lo = _split(lsd_all[rows, :])
        gt = _mm_tn(lsd_hi, mk.tri_t) + _mm_tn(lsd_lo, mk.tri_t)
        for g in range(SSD_G):
            gl = slice(gw * g, gw * (g + 1))
            cg = sc_all[rows, SSD_N * g:SSD_N * (g + 1)]
            bg = sb_all[rows, SSD_N * g:SSD_N * (g + 1)]
            ag = _mm_nt(cg, bg)
            vg = v_all[rows, gl]
            o = jnp.zeros((c, gw), F32)
            for j in range(hpg):
                lane = LANE_SDT + g * hpg + j
                dm = _pair_decay(gc[:, lane:lane + 1], gt[lane:lane + 1, :], mk.incl)
                vj = jnp.where(lane_g == j, vg, 0.0).astype(BF16)
                o = o + _mm((ag * dm).astype(BF16), vj)
            o_state = jnp.zeros((c, gw), F32)
            for b in range(mk.nseq):
                s = get_state(b, g)
                o_state = o_state + _mm(mk.rows_of(b, cg), s.astype(BF16))
                dec = el_all[ci * c + b * ls:ci * c + b * ls + 1, gl]
                put_state(b, g, s * dec + _mm_tn(mk.rows_of(b, bg), vd_all[rows, gl]))
            y = (o + o_state * eg_all[rows, gl] + skip_all[rows, gl]) * gate_all[rows, gl]
            o_ref[rows, gl] = _rms(y, gain_ref[:, gl]).astype(o_ref.dtype)
    if seq_mode:
        @pl.when(pl.program_id(1) == pl.num_programs(1) - 1)
        def _():
            for g in range(SSD_G):
                for j in range(hpg):
                    s1_ref[0, g * hpg + j] = s_scr[g][:, SSD_P * j:SSD_P * (j + 1)]


def _full_spec(a, grid_rank):
    nd = a.ndim
    if grid_rank == 1:
        return pl.BlockSpec(a.shape, lambda i: (0,) * nd)
    return pl.BlockSpec(a.shape, lambda i, j: (0,) * nd)


def _mixer_seq(kernel_fn, name, z, extra_rows, params, nbatch, seqlen, state_shape, scratch, tail_cols=0):
    c, cpb = CHUNK, SEQ_BLOCK_CHUNKS
    rblk = c * cpb
    nblk = seqlen // rblk
    rowmap = lambda b, j: (b * nblk + j, 0)
    in_specs = [pl.BlockSpec((rblk, z.shape[1]), rowmap)]
    args = [z]
    for a, kind in extra_rows:
        if kind == "rows":
            in_specs.append(pl.BlockSpec((rblk, a.shape[1]), rowmap))
        else:
            in_specs.append(pl.BlockSpec((rblk, a.shape[1]), lambda b, j: (j, 0)))
        args.append(a)
    for a in params:
        in_specs.append(_full_spec(a, 2))
        args.append(a)
    sblk = (1,) + state_shape
    out_shape = [jax.ShapeDtypeStruct((nbatch * seqlen, BRANCH_W), BF16),
                 jax.ShapeDtypeStruct((nbatch,) + state_shape, F32)]
    out_specs = [pl.BlockSpec((rblk, BRANCH_W), rowmap),
                 pl.BlockSpec(sblk, lambda b, j: (b,) + (0,) * len(state_shape))]
    if tail_cols:
        out_shape.append(jax.ShapeDtypeStruct((nbatch, 8, tail_cols), F32))
        out_specs.append(pl.BlockSpec((1, 8, tail_cols), lambda b, j: (b, 0, 0)))
    return pl.pallas_call(
        functools.partial(kernel_fn, c=c, cpb=cpb, ls=c, seq_mode=True),
        grid=(nbatch, nblk),
        in_specs=in_specs,
        out_specs=out_specs,
        out_shape=out_shape,
        scratch_shapes=scratch,
        compiler_params=pltpu.CompilerParams(dimension_semantics=("parallel", "arbitrary"),
                                             vmem_limit_bytes=VMEM_LIMIT),
        name=name,
    )(*args)


def _mixer_batch(kernel_fn, name, z, row_off, extra_rows, params, per_seq, layer, stacked_prev,
                 nbatch, seqlen, state_shape):
    sb = BATCH_SEQS
    c = sb * seqlen
    nsteps = nbatch // sb
    off = row_off // c
    rowmap = lambda i: (off + i, 0)
    in_specs = [pl.BlockSpec((c, z.shape[1]), rowmap)]
    args = [z]
    for a, kind in extra_rows:
        if kind == "rows":
            in_specs.append(pl.BlockSpec((c, a.shape[1]), rowmap))
        else:
            in_specs.append(pl.BlockSpec((c, a.shape[1]), lambda i: (0, 0)))
        args.append(a)
    for a in params:
        in_specs.append(_full_spec(a, 1))
        args.append(a)
    for a in per_seq:
        if a.ndim == 2:
            in_specs.append(pl.BlockSpec((c, a.shape[1]), lambda i: (i, 0)))
        else:
            in_specs.append(pl.BlockSpec((None, sb) + a.shape[2:], lambda i: (layer, i) + (0,) * (a.ndim - 2)))
        args.append(a)
    zeros = (0,) * len(state_shape)
    if layer:
        in_specs.append(pl.BlockSpec((layer, sb) + state_shape, lambda i: (0, i) + zeros))
        args.append(stacked_prev)

    def body(*refs):
        *ins, o_ref, s1_ref = refs
        if layer:
            prev_ref = ins.pop()
            for l in range(layer):
                s1_ref[l] = prev_ref[l]
        kernel_fn(*ins, o_ref, s1_ref.at[layer], c=c, cpb=1, ls=seqlen, seq_mode=False)

    out_shape = [jax.ShapeDtypeStruct((nbatch * seqlen, BRANCH_W), BF16),
                 jax.ShapeDtypeStruct((layer + 1, nbatch) + state_shape, F32)]
    out_specs = [pl.BlockSpec((c, BRANCH_W), lambda i: (i, 0)),
                 pl.BlockSpec((layer + 1, sb) + state_shape, lambda i: (0, i) + zeros)]
    return pl.pallas_call(
        body,
        grid=(nsteps,),
        in_specs=in_specs,
        out_specs=out_specs,
        out_shape=out_shape,
        compiler_params=pltpu.CompilerParams(dimension_semantics=("parallel",), vmem_limit_bytes=VMEM_LIMIT),
        name=name,
    )(*args)


def _merge_kernel(*refs, nh, npt):
    branch_refs, zg = refs[:2 * N_BRANCH], refs[2 * N_BRANCH]
    h_refs, (wbr, wout, out_ref) = refs[2 * N_BRANCH + 1:2 * N_BRANCH + 1 + nh], refs[2 * N_BRANCH + 1 + nh:]
    acc = jnp.zeros(out_ref.shape, F32)
    for n in range(N_BRANCH):
        gate = _sigmoid(zg[:, n * D_MODEL:(n + 1) * D_MODEL].astype(F32))
        acc = acc + gate * _mm(_rows_read(branch_refs[2 * n:2 * n + 2], npt), wbr[n])
    out_ref[...] = _rows_read(h_refs, npt) + _mm(acc.astype(BF16), wout[...])


def _merge(branches, zg, h_parts, wbr, wout):
    n_p, n_s = branches[0][0].shape[0], branches[0][1].shape[0]
    n = n_p + n_s
    tm = _row_tile(n_p, n_s)
    row = lambda i: (i, 0)
    in_specs, args = [], []
    for pair in branches:
        in_specs += _rows_specs(pair, tm)
        args += list(pair)
    in_specs.append(pl.BlockSpec((tm, N_BRANCH * D_MODEL), row))
    in_specs += _rows_specs(h_parts, tm)
    in_specs += [pl.BlockSpec(wbr.shape, lambda i: (0, 0, 0)), pl.BlockSpec(wout.shape, lambda i: (0, 0))]
    return pl.pallas_call(
        functools.partial(_merge_kernel, nh=len(h_parts), npt=n_p // tm),
        grid=(n // tm,),
        in_specs=in_specs,
        out_specs=pl.BlockSpec((tm, D_MODEL), row),
        out_shape=jax.ShapeDtypeStruct((n, D_MODEL), F32),
        compiler_params=pltpu.CompilerParams(dimension_semantics=("parallel",), vmem_limit_bytes=VMEM_LIMIT),
        name="merge",
    )(*args, zg, *h_parts, wbr, wout)


def _ffn_kernel(h_ref, g_ref, wg_ref, wu_ref, wd_ref, out_ref):
    h = h_ref[...]
    u = _rms(h, g_ref[...]).astype(BF16)
    a = (_silu(_mm(u, wg_ref[...])) * _mm(u, wu_ref[...])).astype(BF16)
    out_ref[...] = h + _mm(a, wd_ref[...])


def _ffn(h, gain, wg, wu, wd):
    n = h.shape[0]
    tm = _row_tile(n)
    row = lambda i: (i, 0)
    const = lambda i: (0, 0)
    wspec = lambda w: pl.BlockSpec(w.shape, const, pipeline_mode=pl.Buffered(1))
    return pl.pallas_call(
        _ffn_kernel,
        grid=(n // tm,),
        in_specs=[pl.BlockSpec((tm, D_MODEL), row), pl.BlockSpec((1, D_MODEL), const),
                  wspec(wg), wspec(wu), wspec(wd)],
        out_specs=pl.BlockSpec((tm, D_MODEL), row),
        out_shape=jax.ShapeDtypeStruct((n, D_MODEL), F32),
        compiler_params=pltpu.CompilerParams(dimension_semantics=("parallel",), vmem_limit_bytes=VMEM_LIMIT),
        name="dense_ffn",
    )(h, gain, wg, wu, wd)


MOE_TOKENS = 768
MOE_CAP = 224


def _moe_kernel(h_ref, g_ref, rt_ref, wg_ref, wu_ref, wd_ref, out_ref,
                u_scr, acc_scr, w_scr, sel_scr, rank_scr, selt_scr, rankt_scr, xc_scr, yc_scr, cnt_scr):
    tt = h_ref.shape[0]
    cap = MOE_CAP
    e = pl.program_id(1)
    f = pl.program_id(2)
    nf = pl.num_programs(2)

    @pl.when(jnp.logical_and(e == 0, f == 0))
    def _():
        u = _rms(h_ref[...], g_ref[...])
        u_scr[...] = u.astype(BF16)
        acc_scr[...] = jnp.zeros_like(acc_scr)
        uh, ul = _split(u)
        rh, rl = _split(rt_ref[...])
        logits = _mm(uh, rh) + (_mm(uh, rl) + _mm(ul, rh))
        lane = lax.broadcasted_iota(jnp.int32, logits.shape, 1).astype(F32)
        neg = -3.0e38
        lg = jnp.where(lane < N_EXPERTS, logits, neg)
        m1 = jnp.max(lg, axis=-1, keepdims=True)
        i1 = jnp.min(jnp.where(lg == m1, lane, float(SMALL_W)), axis=-1, keepdims=True)
        lg2 = jnp.where(lane == i1, neg, lg)
        m2 = jnp.max(lg2, axis=-1, keepdims=True)
        i2 = jnp.min(jnp.where(lg2 == m2, lane, float(SMALL_W)), axis=-1, keepdims=True)
        e2 = jnp.exp(m2 - m1)
        w_scr[...] = jnp.where(lane == i1, 1.0 / (1.0 + e2), 0.0) + jnp.where(lane == i2, e2 / (1.0 + e2), 0.0)
        sel = jnp.where(jnp.logical_or(lane == i1, lane == i2), 1.0, 0.0)
        sel_scr[...] = sel
        selb = sel.astype(BF16)
        r = lax.broadcasted_iota(jnp.int32, (tt, tt), 0)
        q = lax.broadcasted_iota(jnp.int32, (tt, tt), 1)
        before = jnp.where(r < q, 1.0, 0.0).astype(BF16)
        ident = jnp.where(r == q, 1.0, 0.0).astype(BF16)
        rank_scr[...] = _mm_tn(before, selb)
        rankt_scr[...] = _mm_tn(selb, before)
        selt_scr[...] = _mm_tn(selb, ident)
        cnt = jnp.sum(sel, axis=0, keepdims=True)
        for ee in range(N_EXPERTS):
            cnt_scr[ee] = jnp.sum(jnp.where(lane[0:1, :] == float(ee), cnt, 0.0)).astype(jnp.int32)

    nsub = (cnt_scr[e] + (cap - 1)) // cap
    lane_w = lax.broadcasted_iota(jnp.int32, (tt, SMALL_W), 1)

    def col_of(ref):
        return jnp.sum(jnp.where(lane_w == e, ref[...], 0.0), axis=-1, keepdims=True)

    @pl.when(f == 0)
    def _():
        rank_row = rankt_scr[pl.ds(e, 1), :]
        sel_row = selt_scr[pl.ds(e, 1), :]
        jcol = lax.broadcasted_iota(jnp.int32, (cap, 1), 0).astype(F32)

        def gather(s, carry):
            base = (s * cap).astype(F32)
            hit = jnp.logical_and(rank_row == jcol + base, sel_row > 0.5)
            onehot = jnp.where(hit, 1.0, 0.0).astype(BF16)
            off = pl.multiple_of(s * cap, 16)
            xc_scr[pl.ds(off, cap), :] = _mm(onehot, u_scr[...]).astype(BF16)
            return carry

        lax.fori_loop(0, nsub, gather, 0)

    def expert(s, carry):
        off = pl.multiple_of(s * cap, 16)
        x = xc_scr[pl.ds(off, cap), :]
        a = (_silu(_mm(x, wg_ref[0])) * _mm(x, wu_ref[0])).astype(BF16)
        y = _mm(a, wd_ref[0])

        @pl.when(f == 0)
        def _():
            yc_scr[pl.ds(off, cap), :] = y

        @pl.when(f != 0)
        def _():
            yc_scr[pl.ds(off, cap), :] += y

        return carry

    lax.fori_loop(0, nsub, expert, 0)

    @pl.when(f == nf - 1)
    def _():
        rank_col = col_of(rank_scr)
        sel_col = col_of(sel_scr)
        w_col = col_of(w_scr)
        jrow = lax.broadcasted_iota(jnp.int32, (1, cap), 1).astype(F32)

        def scatter(s, carry):
            base = (s * cap).astype(F32)
            hit = jnp.logical_and(rank_col == jrow + base, sel_col > 0.5)
            onehot = jnp.where(hit, 1.0, 0.0).astype(BF16)
            off = pl.multiple_of(s * cap, 16)
            acc_scr[...] += w_col * _mm(onehot, yc_scr[pl.ds(off, cap), :].astype(BF16))
            return carry

        lax.fori_loop(0, nsub, scatter, 0)

    @pl.when(jnp.logical_and(e == pl.num_programs(1) - 1, f == nf - 1))
    def _():
        out_ref[...] = h_ref[...] + acc_scr[...]


def _moe(h, gain, router, wg, wu, wd):
    n = h.shape[0]
    tt = MOE_TOKENS if n % MOE_TOKENS == 0 else _row_tile(n)
    ne = wg.shape[0]
    tf = D_FF // FF_BLOCKS
    row = lambda i, e, f: (i, 0)
    nrows = pl.cdiv(tt, MOE_CAP) * MOE_CAP
    in_specs = [pl.BlockSpec((tt, D_MODEL), row), pl.BlockSpec((1, D_MODEL), lambda i, e, f: (0, 0)),
                pl.BlockSpec(router.shape, lambda i, e, f: (0, 0)),
                pl.BlockSpec((1, D_MODEL, tf), lambda i, e, f: (e, 0, f)),
                pl.BlockSpec((1, D_MODEL, tf), lambda i, e, f: (e, 0, f)),
                pl.BlockSpec((1, tf, D_MODEL), lambda i, e, f: (e, f, 0))]
    scratch = [pltpu.VMEM((tt, D_MODEL), BF16), pltpu.VMEM((tt, D_MODEL), F32),
               pltpu.VMEM((tt, SMALL_W), F32), pltpu.VMEM((tt, SMALL_W), F32), pltpu.VMEM((tt, SMALL_W), F32),
               pltpu.VMEM((SMALL_W, tt), F32), pltpu.VMEM((SMALL_W, tt), F32),
               pltpu.VMEM((nrows, D_MODEL), BF16), pltpu.VMEM((nrows, D_MODEL), F32),
               pltpu.SMEM((N_EXPERTS,), jnp.int32)]
    return pl.pallas_call(
        _moe_kernel,
        grid=(n // tt, ne, FF_BLOCKS),
        in_specs=in_specs,
        out_specs=pl.BlockSpec((tt, D_MODEL), row),
        out_shape=jax.ShapeDtypeStruct((n, D_MODEL), F32),
        scratch_shapes=scratch,
        compiler_params=pltpu.CompilerParams(dimension_semantics=("parallel", "arbitrary", "arbitrary"),
                                             vmem_limit_bytes=VMEM_LIMIT),
        name="moe",
    )(h, gain, router, wg, wu, wd)


def _ple_kernel(*refs, final, npt):
    if final:
        h_ref, pp_ref, ps_ref, g_ref, wgate, wproj, gf_ref, yp_ref, ys_ref = refs
    else:
        h_ref, pp_ref, ps_ref, g_ref, wgate, wproj, out_ref = refs
    h = h_ref[...]
    gate = _sigmoid(_mm(_rms(h, g_ref[...]).astype(BF16), wgate[...]))
    h = h + gate * _mm(_rows_read((pp_ref, ps_ref), npt).astype(BF16), wproj[...])
    if final:
        y = _rms(h, gf_ref[...])

        @pl.when(pl.program_id(0) < npt)
        def _():
            yp_ref[...] = y

        @pl.when(pl.program_id(0) >= npt)
        def _():
            ys_ref[...] = y
    else:
        out_ref[...] = h


def _ple(h, p_parts, layer, gain, wgate, wproj, gain_final=None):
    n_p, n_s = p_parts[0].shape[1], p_parts[1].shape[1]
    n = h.shape[0]
    tm = _row_tile(n_p, n_s)
    final = gain_final is not None
    row = lambda i: (i, 0)
    const = lambda i: (0, 0)
    in_specs = [pl.BlockSpec((tm, D_MODEL), row)] + _rows_specs(p_parts, tm, layer)
    in_specs += [pl.BlockSpec((1, D_MODEL), const), pl.BlockSpec(wgate.shape, const), pl.BlockSpec(wproj.shape, const)]
    args = [h, *p_parts, gain, wgate, wproj]
    if final:
        in_specs.append(pl.BlockSpec((1, D_MODEL), const))
        args.append(gain_final)
        out_shape = [jax.ShapeDtypeStruct((n_p, D_MODEL), F32), jax.ShapeDtypeStruct((n_s, D_MODEL), F32)]
        out_specs = _rows_specs(out_shape, tm)
    else:
        out_shape = jax.ShapeDtypeStruct((n, D_MODEL), F32)
        out_specs = pl.BlockSpec((tm, D_MODEL), row)
    return pl.pallas_call(
        functools.partial(_ple_kernel, final=final, npt=n_p // tm),
        grid=(n // tm,),
        in_specs=in_specs,
        out_specs=out_specs,
        out_shape=out_shape,
        compiler_params=pltpu.CompilerParams(dimension_semantics=("arbitrary",), vmem_limit_bytes=VMEM_LIMIT),
        name="ple",
    )(*args)


def _pad_lanes(x, lane0):
    w = x.shape[-1]
    pad = [(0, 0)] * (x.ndim - 1) + [(lane0, SMALL_W - lane0 - w)]
    return jnp.pad(x, pad)


def _in_proj_weights(w):
    cuts = np.cumsum(SPLITS)[:-1].tolist()
    (gq, gk, gv, glr, gg, rq, rk, rv, rg, dqkv, db, da, dg, sz, sxbc, sdt, mg) = jnp.split(w, cuts, axis=-1)
    cat = lambda *xs: jnp.concatenate(xs, axis=-1).astype(BF16)
    small = jnp.concatenate([glr, db, da, sdt], axis=-1)
    small = jnp.pad(small, ((0, 0), (0, SMALL_W - small.shape[1]))).astype(BF16)
    return (cat(gq, gk, gv, gg), cat(rq, rk, rv, rg), cat(dqkv, dg), cat(sz, sxbc), mg.astype(BF16),
            small)


def _rope_tables(pos):
    half = MIX_DK // 2
    inv = ROPE_BASE ** (-jnp.arange(half, dtype=F32) / half)
    ang = jnp.asarray(pos).astype(F32)[:, None] * inv[None, :]
    cos, sin = jnp.cos(ang), jnp.sin(ang)
    cos_t = jnp.tile(jnp.concatenate([cos, cos], axis=-1), (1, MIX_H))
    sin_t = jnp.tile(jnp.concatenate([-sin, sin], axis=-1), (1, MIX_H))
    return cos_t, sin_t


def _conv_prev(state_conv):
    b, _, cdim = state_conv.shape
    return jnp.pad(state_conv, ((0, 0), (1, 0), (0, 0))).reshape(b * CONV_W, cdim)


def kernel(x_prompt, x_sample, state_gla, state_ret, state_gdn, state_gdn_conv, state_ssd, state_ssd_conv, p_prompt, p_sample, norm_mix, w_in, gla_w_gk, gla_b_gk, gla_norm, ret_norm, gdn_conv_w, gdn_a_log, gdn_dt_bias, gdn_norm, ssd_conv_w, ssd_conv_b, ssd_a_log, ssd_dt_bias, ssd_d, ssd_norm, w_branch, w_out, norm_ffn, ffn_w_gate, ffn_w_up, ffn_w_down, moe_router, moe_w_gate, moe_w_up, moe_w_down, norm_ple, ple_w_gate, ple_w_proj, norm_final):
    bp, lp, _ = x_prompt.shape
    bs, lsmp, _ = x_sample.shape
    depth = w_in.shape[0]
    n_p = bp * lp
    n_s = bs * lsmp
    h_parts = (x_prompt.reshape(n_p, D_MODEL), x_sample.reshape(n_s, D_MODEL))
    p_parts = (p_prompt.reshape(depth, n_p, D_PLE), p_sample.reshape(depth, n_s, D_PLE))
    cos_p, sin_p = _rope_tables(np.arange(lp))
    cos_s, sin_s = _rope_tables(PAST_LEN + (np.arange(BATCH_SEQS * lsmp) % lsmp))
    mix_state = (MIX_H, MIX_DK, MIX_DV)
    stacked_state = (MIX_H * MIX_DK, MIX_DV)
    ssd_state = (SSD_H, SSD_N, SSD_P)

    outs_p = [[] for _ in range(6)]
    outs_s = [[] for _ in range(2)]
    gla_s = ret_s = gdn_s = ssd_s = None
    for i in range(depth):
        wts = _in_proj_weights(w_in[i])
        za, zb, zc, zd, zg, zs = _in_proj(h_parts, norm_mix[i][None, :], wts)

        wgk = jnp.pad(gla_w_gk[i], ((LANE_GLR, SMALL_W - LANE_GLR - GLA_RANK), (0, 0))).astype(BF16)
        bgk = gla_b_gk[i][None, :]
        gla_par = [wgk, bgk, gla_norm[i][None, :]]
        oa_p, gla_p = _mixer_seq(_gla_kernel, "gla_seq", za, [(zs, "rows")], gla_par, bp, lp, mix_state,
                                 [pltpu.VMEM(stacked_state, F32)])
        oa_s, gla_s = _mixer_batch(_gla_kernel, "gla_batch", za, n_p, [(zs, "rows")], gla_par,
                                   [state_gla], i, gla_s, bs, lsmp, mix_state)
        ret_par = [ret_norm[i][None, :]]
        ob_p, ret_p = _mixer_seq(_ret_kernel, "ret_seq", zb, [(cos_p, "pos"), (sin_p, "pos")], ret_par,
                                 bp, lp, mix_state, [pltpu.VMEM(stacked_state, F32)])
        ob_s, ret_s = _mixer_batch(_ret_kernel, "ret_batch", zb, n_p, [(cos_s, "pos"), (sin_s, "pos")],
                                   ret_par, [state_ret], i, ret_s, bs, lsmp, mix_state)
        gdn_prow = jnp.concatenate([_pad_lanes(gdn_a_log[i][None, :], LANE_GDA),
                                    _pad_lanes(gdn_dt_bias[i][None, :], LANE_GDA)], axis=0)
        gdn_par = [gdn_conv_w[i], gdn_prow, gdn_norm[i][None, :]]
        oc_p, gdn_p, gdn_tail = _mixer_seq(
            _gdn_kernel, "gdn_seq", zc, [(zs, "rows")], gdn_par, bp, lp, mix_state,
            [pltpu.VMEM(stacked_state, F32), pltpu.VMEM((CHUNK, GDN_CONV_C), F32)], tail_cols=GDN_CONV_C)
        oc_s, gdn_s = _mixer_batch(_gdn_kernel, "gdn_batch", zc, n_p, [(zs, "rows")], gdn_par,
                                   [_conv_prev(state_gdn_conv[i]), state_gdn], i, gdn_s, bs, lsmp, mix_state)
        ssd_prow = jnp.concatenate([_pad_lanes(ssd_a_log[i][None, :], LANE_SDT),
                                    _pad_lanes(ssd_dt_bias[i][None, :], LANE_SDT)], axis=0)
        ssd_par = [ssd_conv_w[i], ssd_conv_b[i][None, :], ssd_prow,
                   jnp.repeat(ssd_d[i], SSD_P)[None, :], ssd_norm[i][None, :]]
        ssd_scr = [pltpu.VMEM((SSD_G, SSD_N, SSD_DI // SSD_G), F32), pltpu.VMEM((CHUNK, SSD_CONV_C), F32)]
        od_p, ssd_p, ssd_tail = _mixer_seq(_ssd_kernel, "ssd_seq", zd, [(zs, "rows")], ssd_par,
                                           bp, lp, ssd_state, ssd_scr, tail_cols=SSD_CONV_C)
        od_s, ssd_s = _mixer_batch(_ssd_kernel, "ssd_batch", zd, n_p, [(zs, "rows")], ssd_par,
                                   [_conv_prev(state_ssd_conv[i]), state_ssd], i, ssd_s, bs, lsmp, ssd_state)

        gdnc_p = gdn_tail[:, 8 - (CONV_W - 1):]
        ssdc_p = ssd_tail[:, 8 - (CONV_W - 1):]
        zc_s = zc[n_p:].reshape(bs, lsmp, -1)[:, :, :GDN_CONV_C].astype(F32)
        zd_s = zd[n_p:].reshape(bs, lsmp, -1)[:, :, SSD_DI:].astype(F32)
        gdnc_s = jnp.concatenate([state_gdn_conv[i], zc_s], axis=1)[:, lsmp:]
        ssdc_s = jnp.concatenate([state_ssd_conv[i], zd_s], axis=1)[:, lsmp:]
        for lst, val in zip(outs_p, (gla_p, ret_p, gdn_p, gdnc_p, ssd_p, ssdc_p)):
            lst.append(val)
        for lst, val in zip(outs_s, (gdnc_s, ssdc_s)):
            lst.append(val)

        branches = ((oa_p, oa_s), (ob_p, ob_s), (oc_p, oc_s), (od_p, od_s))
        h = _merge(branches, zg, h_parts, w_branch[i].astype(BF16), w_out[i].astype(BF16))

        j = i // 2
        if i % 2 == 0:
            h = _ffn(h, norm_ffn[i][None, :], ffn_w_gate[j].astype(BF16), ffn_w_up[j].astype(BF16),
                     ffn_w_down[j].astype(BF16))
        else:
            router = jnp.pad(moe_router[j], ((0, 0), (0, SMALL_W - N_EXPERTS)))
            h = _moe(h, norm_ffn[i][None, :], router, moe_w_gate[j].astype(BF16), moe_w_up[j].astype(BF16),
                     moe_w_down[j].astype(BF16))
        h = _ple(h, p_parts, i, norm_ple[i][None, :], ple_w_gate[i].astype(BF16), ple_w_proj[i].astype(BF16),
                 gain_final=norm_final[None, :] if i == depth - 1 else None)
        h_parts = (h,)

    y_prompt = h[0].reshape(bp, lp, D_MODEL)
    y_sample = h[1].reshape(bs, lsmp, D_MODEL)
    gdnc_s, ssdc_s = (jnp.stack(l) for l in outs_s)
    return ((y_prompt, y_sample) + tuple(jnp.stack(l) for l in outs_p)
            + (gla_s, ret_s, gdn_s, gdnc_s, ssd_s, ssdc_s))
```

```python
import functools
import math

import numpy as np
import jax
import jax.numpy as jnp
from jax import lax
from jax.experimental import pallas as pl
from jax.experimental.pallas import tpu as pltpu

F32 = jnp.float32
BF16 = jnp.bfloat16
EPS = 1e-6

D_MODEL = 1024
D_PLE = 256
CONV_W = 4
N_BRANCH = 4
BRANCH_W = 512
MIX_H = 4
MIX_DK = 64
MIX_DV = 128
GLA_RANK = 16
GLA_GATE_NORM = 16.0
ROPE_BASE = 10000.0
GDN_CONV_C = 2 * MIX_H * MIX_DK + MIX_H * MIX_DV
SSD_H = 8
SSD_P = 64
SSD_N = 64
SSD_G = 2
SSD_DI = SSD_H * SSD_P
SSD_CONV_C = SSD_DI + 2 * SSD_G * SSD_N
D_FF = 2816
N_EXPERTS = 8
SPLITS = (256, 256, 512, GLA_RANK, 512,
          256, 256, 512, 512,
          GDN_CONV_C, MIX_H, MIX_H, 512,
          SSD_DI, SSD_CONV_C, SSD_H,
          N_BRANCH * D_MODEL)

LANE_GLR = 0
LANE_BETA = 16
LANE_GDA = 20
LANE_SDT = 24
SMALL_W = 128

CHUNK = 64
SEQ_BLOCK_CHUNKS = 16
BATCH_SEQS = 8
CUMSUM_ROWS = 256
ROW_TILE = 512
FF_BLOCKS = 2
VMEM_LIMIT = 56 * 1024 * 1024
PAST_LEN = 16384


def _row_tile(*counts):
    for tm in (ROW_TILE, 256, 128, 64, 32, 16):
        if all(n % tm == 0 for n in counts):
            return tm
    raise ValueError(f"row counts {counts} are not all multiples of 16")


def _rows_specs(parts, tm, layer=None):
    lead = () if layer is None else (layer,)
    blk = (tm, parts[0].shape[-1]) if layer is None else (None, tm, parts[0].shape[-1])
    if len(parts) == 1:
        return [pl.BlockSpec(blk, lambda i: lead + (i, 0))]
    npt = parts[0].shape[-2] // tm
    return [pl.BlockSpec(blk, lambda i: lead + (jnp.minimum(i, npt - 1), 0)),
            pl.BlockSpec(blk, lambda i: lead + (jnp.maximum(i - npt, 0), 0))]


def _rows_read(refs, npt):
    if len(refs) == 1:
        return refs[0][...]
    return jnp.where(pl.program_id(0) < npt, refs[0][...], refs[1][...])


def _mm(a, b):
    return jnp.dot(a, b, preferred_element_type=F32)


def _mm_nt(a, b):
    return lax.dot_general(a, b, (((1,), (1,)), ((), ())), preferred_element_type=F32)


def _mm_tn(a, b):
    return lax.dot_general(a, b, (((0,), (0,)), ((), ())), preferred_element_type=F32)


def _split(x):
    hi = x.astype(BF16)
    lo = (x - hi.astype(F32)).astype(BF16)
    return hi, lo


def _mm_xl(x, m):
    hi, lo = _split(x)
    return _mm(hi, m) + _mm(lo, m)


def _mm_xr(m, x):
    hi, lo = _split(x)
    return _mm(m, hi) + _mm(m, lo)


def _sigmoid(x):
    return 1.0 / (1.0 + jnp.exp(-x))


def _silu(x):
    return x * _sigmoid(x)


def _softplus(x):
    return jnp.maximum(x, 0.0) + jnp.log1p(jnp.exp(-jnp.abs(x)))


def _rms(x, gain):
    ms = jnp.mean(x * x, axis=-1, keepdims=True)
    return x * lax.rsqrt(ms + EPS) * gain


def _log2(n):
    k = int(round(math.log2(n)))
    assert (1 << k) == n, n
    return k


class _Masks:
    def __init__(self, c, ls):
        self.c, self.ls = c, ls
        sh = _log2(ls)
        r = lax.broadcasted_iota(jnp.int32, (c, c), 0)
        q = lax.broadcasted_iota(jnp.int32, (c, c), 1)
        same = (r >> sh) == (q >> sh)
        self.incl = jnp.logical_and(same, q <= r)
        self.strict = jnp.logical_and(same, q < r)
        self.tri = jnp.where(self.incl, 1.0, 0.0).astype(BF16)
        self.tri_t = jnp.where(jnp.logical_and(same, r <= q), 1.0, 0.0).astype(BF16)
        last =((r >> sh) << sh) + (ls - 1)
        self.lastsel = jnp.where(q == last, 1.0, 0.0).astype(BF16)
        self.midsel = jnp.where(q == last - ls // 2, 1.0, 0.0).astype(BF16)
        self.eye = jnp.where(r == q, 1.0, 0.0).astype(F32)
        self.r, self.q = r, q
        rc = lax.broadcasted_iota(jnp.int32, (c, 1), 0)
        self.seq_of_row = rc >> sh
        self.t_col = (rc & (ls - 1)).astype(F32)
        self.nseq = c // ls

    def rows_of(self, b, x):
        if self.nseq == 1:
            return x
        return jnp.where(self.seq_of_row == b, x, jnp.zeros_like(x))

    def shift_matrix(self, nprev):
        c, ls = self.c, self.ls
        t = self.r & (ls - 1)
        blocks = []
        for s in (1, 2, 3):
            cur = jnp.logical_and(self.q == self.r - s, t >= s)
            prev = jnp.logical_and(self.q == self.r + (ls - s), t < s)
            cur = jnp.where(cur, 1.0, 0.0).astype(BF16)
            prev = jnp.where(prev, 1.0, 0.0).astype(BF16)
            blocks.append(jnp.concatenate([prev] * nprev + [cur], axis=1))
        return jnp.concatenate(blocks, axis=0)


def _expander(lane0, group, width):
    r = lax.broadcasted_iota(jnp.int32, (SMALL_W, width), 0)
    q = lax.broadcasted_iota(jnp.int32, (SMALL_W, width), 1)
    return jnp.where(r == lane0 + (q >> _log2(group)), 1.0, 0.0).astype(BF16)


def _causal_conv(u, prev_parts, w, masks, shift_m):
    c = masks.c
    x = jnp.concatenate(list(prev_parts) + [u], axis=0)
    y = _mm(shift_m, x)
    out = u.astype(F32) * w[3:4, :]
    for s in (1, 2, 3):
        out = out + y[(s - 1) * c:s * c, :] * w[3 - s:4 - s, :]
    return out


def _row_blocks(fn, x):
    n = x.shape[0]
    if n <= CUMSUM_ROWS:
        return fn(x)
    return jnp.concatenate([fn(x[i:i + CUMSUM_ROWS, :]) for i in range(0, n, CUMSUM_ROWS)], axis=0)


def _pair_decay(gc_col, gt_row, incl):
    d = gc_col - gt_row
    return jnp.where(incl, jnp.exp(jnp.where(incl, d, 0.0)), 0.0)


def _seq_prologue(s_scr, prev_scr=None):
    @pl.when(pl.program_id(1) == 0)
    def _():
        s_scr[...] = jnp.zeros_like(s_scr)
        if prev_scr is not None:
            prev_scr[...] = jnp.zeros_like(prev_scr)


def _seq_epilogue(s_scr, s1_ref):
    @pl.when(pl.program_id(1) == pl.num_programs(1) - 1)
    def _():
        s1_ref[0] = s_scr[...].reshape(s1_ref.shape[1:])


def _conv_tail(z_ref, tail_ref, col0, col1):
    @pl.when(pl.program_id(1) == pl.num_programs(1) - 1)
    def _():
        nrows = z_ref.shape[0]
        tail_ref[0] = z_ref[nrows - 8:nrows, col0:col1].astype(F32)


def _in_proj_kernel(*refs, nh, npt):
    h_refs, (g_ref, wa, wb, wc, wd, wg, ws, za, zb, zc, zd, zg, zs) = refs[:nh], refs[nh:]
    xn = _rms(_rows_read(h_refs, npt), g_ref[...]).astype(BF16)
    for w_ref, o_ref in ((wa, za), (wb, zb), (wc, zc), (wd, zd), (wg, zg)):
        width = o_ref.shape[1]
        for j in range(0, width, 512):
            jw = min(512, width - j)
            o_ref[:, j:j + jw] = _mm(xn, w_ref[:, j:j + jw]).astype(o_ref.dtype)
    zs[...] = _mm(xn, ws[...])


def _in_proj(h_parts, gain, wts):
    counts = [p.shape[0] for p in h_parts]
    n = sum(counts)
    tm = _row_tile(*counts)
    wa, wb, wc, wd, wg, ws = wts
    const = lambda i: (0, 0)
    row = lambda i: (i, 0)

    def wspec(w):
        return pl.BlockSpec(w.shape, const, pipeline_mode=pl.Buffered(1))

    outs = [jax.ShapeDtypeStruct((n, w.shape[1]), BF16) for w in (wa, wb, wc, wd, wg)]
    outs.append(jax.ShapeDtypeStruct((n, SMALL_W), F32))
    out_specs = [pl.BlockSpec((tm, w.shape[1]), row) for w in (wa, wb, wc, wd, wg)]
    out_specs.append(pl.BlockSpec((tm, SMALL_W), row))
    return pl.pallas_call(
        functools.partial(_in_proj_kernel, nh=len(h_parts), npt=counts[0] // tm),
        grid=(n // tm,),
        in_specs=_rows_specs(h_parts, tm) + [pl.BlockSpec((1, D_MODEL), const)] + [wspec(w) for w in wts],
        out_specs=out_specs,
        out_shape=outs,
        compiler_params=pltpu.CompilerParams(dimension_semantics=("parallel",), vmem_limit_bytes=VMEM_LIMIT),
        name="in_proj",
    )(*h_parts, gain, *wts)


class _Stack:
    def __init__(self, c, ls):
        self.c, self.ls = c, ls
        self.hs = MIX_H * c
        self.nseq = c // ls
        self.masks = _Masks(self.hs, ls)
        self.head_of_lane = lax.broadcasted_iota(jnp.int32, (1, MIX_H * MIX_DK), 1) >> _log2(MIX_DK)
        self.seq_of_row = (lax.broadcasted_iota(jnp.int32, (self.hs, 1), 0) & (c - 1)) >> _log2(ls)

    def keys(self, x):
        return jnp.concatenate(
            [jnp.where(self.head_of_lane == h, x, jnp.zeros_like(x)) for h in range(MIX_H)], axis=0)

    def values(self, x):
        return jnp.concatenate([x[:, MIX_DV * h:MIX_DV * (h + 1)] for h in range(MIX_H)], axis=0)

    def rows_of(self, b, x):
        if self.nseq == 1:
            return x
        return jnp.where(self.seq_of_row == b, x, jnp.zeros_like(x))


def _write_heads(o_ref, rows, o_stacked, c, gain, gate):
    for h in range(MIX_H):
        vl = slice(MIX_DV * h, MIX_DV * (h + 1))
        y = _rms(o_stacked[h * c:(h + 1) * c, :], gain) * _silu(gate[:, vl])
        o_ref[rows, vl] = y.astype(o_ref.dtype)


def _gla_kernel(*refs, c, cpb, ls, seq_mode):
    if seq_mode:
        z_ref, zs_ref, wgk_ref, bgk_ref, gain_ref, o_ref, s1_ref, s_scr = refs
        s0_ref = None
        _seq_prologue(s_scr)
    else:
        z_ref, zs_ref, wgk_ref, bgk_ref, gain_ref, s0_ref, o_ref, s1_ref = refs
        s_scr = None
    hdk = MIX_H * MIX_DK
    mkb = _Masks(min(cpb * c, CUMSUM_ROWS), ls)
    sk = _Stack(c, ls)
    r = lax.broadcasted_iota(jnp.int32, (hdk, hdk), 0)
    q = lax.broadcasted_iota(jnp.int32, (hdk, hdk), 1)
    eye_k = r == q
    ones_dv = jnp.ones((hdk, MIX_DV), BF16)
    qq = z_ref[:, 0:256].astype(F32) * (MIX_DK ** -0.5)
    kk = z_ref[:, 256:512].astype(F32)
    pre = _mm(zs_ref[...].astype(BF16), wgk_ref[...]) + bgk_ref[...]
    lg = -_softplus(-pre) * (1.0 / GLA_GATE_NORM)
    g = _row_blocks(lambda t: _mm_xr(mkb.tri, t), lg)
    glast = _row_blocks(lambda t: _mm_xr(mkb.lastsel, t), g)
    gmid = _row_blocks(lambda t: _mm_xr(mkb.midsel, t), g)
    qa_all = (qq * jnp.exp(g - gmid)).astype(BF16)
    ka_all = (kk * jnp.exp(gmid - g)).astype(BF16)
    qe_all = (qq * jnp.exp(g)).astype(BF16)
    kd_all = (kk * jnp.exp(glast - g)).astype(BF16)
    el_all = jnp.exp(glast)
    chunks = []
    for ci in range(cpb):
        rows = slice(ci * c, (ci + 1) * c)
        qe = sk.keys(qe_all[rows, :])
        kd = sk.keys(kd_all[rows, :])
        v = sk.values(z_ref[rows, 512:1024])
        a = jnp.where(sk.masks.incl, _mm_nt(sk.keys(qa_all[rows, :]), sk.keys(ka_all[rows, :])), 0.0).astype(BF16)
        el = el_all[rows, :]
        decays = []
        for b in range(sk.nseq):
            diag = jnp.where(eye_k, jnp.broadcast_to(el[b * ls:b * ls + 1, :], (hdk, hdk)), 0.0)
            decays.append(_mm_xl(diag, ones_dv))
        chunks.append(dict(rows=rows, qe=qe, kd=kd, v=v, o=_mm(a, v), decays=decays))
    for ch in chunks:
        o = ch["o"]
        for b in range(sk.nseq):
            s = s_scr[...] if seq_mode else s0_ref[b].reshape(hdk, MIX_DV)
            o = o + _mm(sk.rows_of(b, ch["qe"]), s.astype(BF16))
            s_new = s * ch["decays"][b] + _mm_tn(sk.rows_of(b, ch["kd"]), ch["v"])
            if seq_mode:
                s_scr[...] = s_new
            else:
                s1_ref[b] = s_new.reshape(MIX_H, MIX_DK, MIX_DV)
        rows = ch["rows"]
        _write_heads(o_ref, rows, o, c, gain_ref[...], z_ref[rows, 1024:1536].astype(F32))
    if seq_mode:
        _seq_epilogue(s_scr, s1_ref)


def _ret_kernel(*refs, c, cpb, ls, seq_mode):
    if seq_mode:
        z_ref, cos_ref, sin_ref, gain_ref, o_ref, s1_ref, s_scr = refs
        s0_ref = None
        _seq_prologue(s_scr)
    else:
        z_ref, cos_ref, sin_ref, gain_ref, s0_ref, o_ref, s1_ref = refs
        s_scr = None
    hdk = MIX_H * MIX_DK
    sk = _Stack(c, ls)
    ms = sk.masks
    t_col = (lax.broadcasted_iota(jnp.int32, (cpb * c, 1), 0) & (ls - 1)).astype(F32)
    lgam =[math.log(1.0 - 2.0 ** (-5.0 - h)) for h in range(MIX_H)]

    def per_head(idx):
        out = jnp.zeros(idx.shape, F32)
        for h in range(MIX_H):
            out = jnp.where(idx == h, lgam[h], out)
        return out

    lane = lax.broadcasted_iota(jnp.int32, (1, hdk), 1)
    lg_lane = per_head(lane >> _log2(MIX_DK))
    lg_srow = per_head(lax.broadcasted_iota(jnp.int32, (sk.hs, 1), 0) >> _log2(c))
    lg_krow = per_head(lax.broadcasted_iota(jnp.int32, (hdk, 1), 0) >> _log2(MIX_DK))
    first_half = (lane & (MIX_DK - 1)) < MIX_DK // 2
    eg = jnp.exp((t_col + 1.0) * lg_lane)
    ed = jnp.exp((ls - 1.0 - t_col) * lg_lane)
    dt_pos = ((ms.r & (ls - 1)) - (ms.q & (ls - 1))).astype(F32)
    dm = jnp.where(ms.incl, jnp.exp(jnp.where(ms.incl, dt_pos * lg_srow, 0.0)), 0.0)
    dec = jnp.broadcast_to(jnp.exp(lg_krow * float(ls)), (hdk, MIX_DV))

    def rope(x, cs, sn):
        sw = jnp.where(first_half, pltpu.roll(x, hdk - MIX_DK // 2, 1), pltpu.roll(x, MIX_DK // 2, 1))
        return x * cs + sw * sn

    cs = cos_ref[...]
    sn = sin_ref[...]
    qr = rope(z_ref[:, 0:256].astype(F32), cs, sn)
    kr = rope(z_ref[:, 256:512].astype(F32), cs, sn) * (MIX_DK ** -0.5)
    qb_all = qr.astype(BF16)
    kb_all = kr.astype(BF16)
    qe_all = (qr * eg).astype(BF16)
    kd_all = (kr * ed).astype(BF16)
    chunks = []
    for ci in range(cpb):
        rows = slice(ci * c, (ci + 1) * c)
        v = sk.values(z_ref[rows, 512:1024])
        a = (_mm_nt(sk.keys(qb_all[rows, :]), sk.keys(kb_all[rows, :])) * dm).astype(BF16)
        chunks.append(dict(rows=rows, qe=sk.keys(qe_all[rows, :]), kd=sk.keys(kd_all[rows, :]),
                           v=v, o=_mm(a, v)))
    for ch in chunks:
        o = ch["o"]
        for b in range(sk.nseq):
            s = s_scr[...] if seq_mode else s0_ref[b].reshape(hdk, MIX_DV)
            o = o + _mm(sk.rows_of(b, ch["qe"]), s.astype(BF16))
            s_new = s * dec + _mm_tn(sk.rows_of(b, ch["kd"]), ch["v"])
            if seq_mode:
                s_scr[...] = s_new
            else:
                s1_ref[b] = s_new.reshape(MIX_H, MIX_DK, MIX_DV)
        rows = ch["rows"]
        _write_heads(o_ref, rows, o, c, gain_ref[...], z_ref[rows, 1024:1536].astype(F32))
    if seq_mode:
        _seq_epilogue(s_scr, s1_ref)


def _gdn_kernel(*refs, c, cpb, ls, seq_mode):
    if seq_mode:
        z_ref, zs_ref, cw_ref, prow_ref, gain_ref, o_ref, s1_ref, tail_ref, s_scr, prev_scr = refs
        s0_ref = prev_ref = None
        _seq_prologue(s_scr, prev_scr)
        _conv_tail(z_ref, tail_ref, 0, GDN_CONV_C)
    else:
        z_ref, zs_ref, cw_ref, prow_ref, gain_ref, prev_ref, s0_ref, o_ref, s1_ref = refs
        s_scr = prev_scr = None
    hdk = MIX_H * MIX_DK
    mk = _Masks(c, ls)
    mkb = _Masks(min(cpb * c, CUMSUM_ROWS), ls)
    sk = _Stack(c, ls)
    hs, mks = sk.hs, sk.masks
    shift_m = mk.shift_matrix(1 if seq_mode else 2)
    r = lax.broadcasted_iota(jnp.int32, (hdk, hdk), 0)
    q = lax.broadcasted_iota(jnp.int32, (hdk, hdk), 1)
    ones_bd = jnp.where((r >> 6) == (q >> 6), 1.0, 0.0).astype(BF16)
    e_beta_k = _expander(LANE_BETA, MIX_DK, hdk)
    e_beta_v = _expander(LANE_BETA, MIX_DV, MIX_H * MIX_DV)
    e_g_k = _expander(LANE_GDA, MIX_DK, hdk)
    e_g_v = _expander(LANE_GDA, MIX_DV, MIX_H * MIX_DV)
    a_row = -jnp.exp(prow_ref[0:1, :])
    b_row = prow_ref[1:2, :]
    cw = cw_ref[...]
    lane0 = jnp.where(lax.broadcasted_iota(jnp.int32, (hs, SMALL_W), 1) == 0, 1.0, 0.0).astype(BF16)
    stack_k, stack_v, srows_of = sk.keys, sk.values, sk.rows_of

    conv = []
    for ci in range(cpb):
        u = z_ref[ci * c:(ci + 1) * c, 0:GDN_CONV_C]
        if seq_mode:
            prev_parts = [prev_scr[...].astype(BF16)]
        else:
            prev_parts = list(_split(prev_ref[...]))
        conv.append(_causal_conv(u, prev_parts, cw, mk, shift_m))
        if seq_mode:
            prev_scr[...] = u.astype(F32)
    cqkv = _silu(conv[0] if cpb == 1 else jnp.concatenate(conv, axis=0))
    cq = cqkv[:, 0:256]
    ck = cqkv[:, 256:512]
    cv = cqkv[:, 512:1024]
    qn = cq * lax.rsqrt(_mm_xl(cq * cq, ones_bd) + EPS) * (MIX_DK ** -0.5)
    kn = ck * lax.rsqrt(_mm_xl(ck * ck, ones_bd) + EPS)
    zs = zs_ref[...]
    beta = _sigmoid(zs)
    lgd = a_row * _softplus(zs + b_row)
    gc_all = _row_blocks(lambda t: _mm_xr(mkb.tri, t), lgd)
    glast = _row_blocks(lambda t: _mm_xr(mkb.lastsel, t), gc_all)
    beta_k = _mm_xl(beta, e_beta_k)
    beta_v = _mm_xl(beta, e_beta_v)
    eg_k = _mm_xl(jnp.exp(gc_all), e_g_k)
    dl_k = _mm_xl(jnp.exp(glast - gc_all), e_g_k)
    el_v_all = _mm_xl(jnp.exp(glast), e_g_v)
    kbeta = kn * beta_k
    kn_all = kn.astype(BF16)
    kbeta_all = kbeta.astype(BF16)
    qn_all = qn.astype(BF16)
    vbeta_all = (cv * beta_v).astype(BF16)
    kbe_all = (kbeta * eg_k).astype(BF16)
    qe_all = (qn * eg_k).astype(BF16)
    kd_all = (kn * dl_k).astype(BF16)

    chunks = []
    for ci in range(cpb):
        rows = slice(ci * c, (ci + 1) * c)
        gc = gc_all[rows, :]
        k_st = stack_k(kn_all[rows, :])
        g_col = jnp.concatenate([gc[:, LANE_GDA + h:LANE_GDA + h + 1] for h in range(MIX_H)], axis=0)
        g_hi, g_lo = _split(jnp.broadcast_to(g_col, (hs, SMALL_W)))
        g_row = _mm_nt(lane0, g_hi) + _mm_nt(lane0, g_lo)
        dm = _pair_decay(g_col, g_row, mks.incl)
        m = _mm_nt(stack_k(kbeta_all[rows, :]), k_st) * jnp.where(mks.strict, dm, 0.0)
        chunks.append(dict(
            rows=rows, m=m, p=mks.eye - m, mp=m,
            a=(_mm_nt(stack_k(qn_all[rows, :]), k_st) * dm).astype(BF16),
            vbeta=stack_v(vbeta_all[rows, :]),
            kbe=stack_k(kbe_all[rows, :]),
            qe=stack_k(qe_all[rows, :]),
            kd=stack_k(kd_all[rows, :]),
            el_v=el_v_all[rows, :]))
    n = 2
    while n < ls:
        for ch in chunks:
            mpb = ch["mp"].astype(BF16)
            ch["mp"] = _mm(mpb, mpb)
        for ch in chunks:
            ch["p"] = ch["p"] + _mm(ch["p"].astype(BF16), ch["mp"].astype(BF16))
        n *= 2
    for ch in chunks:
        tinv = ch["p"].astype(BF16)
        ch["uu"] = _mm(tinv, ch["vbeta"])
        ch["ww"] = _mm(tinv, ch["kbe"]).astype(BF16)
    for ch in chunks:
        rows = ch["rows"]
        states = []
        vn = ch["uu"]
        for b in range(mk.nseq):
            s = s_scr[...] if seq_mode else s0_ref[b].reshape(hdk, MIX_DV)
            states.append(s)
            vn = vn - _mm(srows_of(b, ch["ww"]), s.astype(BF16))
        vnb = vn.astype(BF16)
        o = _mm(ch["a"], vnb)
        for b in range(mk.nseq):
            s = states[b]
            o = o + _mm(srows_of(b, ch["qe"]), s.astype(BF16))
            dec = jnp.concatenate(
                [jnp.broadcast_to(ch["el_v"][b * ls:b * ls + 1, MIX_DV * h:MIX_DV * (h + 1)], (MIX_DK, MIX_DV))
                 for h in range(MIX_H)], axis=0)
            s_new = s * dec + _mm_tn(srows_of(b, ch["kd"]), vnb)
            if seq_mode:
                s_scr[...] = s_new
            else:
                s1_ref[b] = s_new.reshape(MIX_H, MIX_DK, MIX_DV)
        _write_heads(o_ref, rows, o, c, gain_ref[...], z_ref[rows, GDN_CONV_C:GDN_CONV_C + 512].astype(F32))
    if seq_mode:
        _seq_epilogue(s_scr, s1_ref)


def _ssd_kernel(*refs, c, cpb, ls, seq_mode):
    gw = SSD_DI // SSD_G
    hpg = SSD_H // SSD_G
    if seq_mode:
        z_ref, zs_ref, cw_ref, cb_ref, prow_ref, dx_ref, gain_ref, o_ref, s1_ref, tail_ref, s_scr, prev_scr = refs
        s0_ref = prev_ref = None
        _seq_prologue(s_scr, prev_scr)
        _conv_tail(z_ref, tail_ref, SSD_DI, SSD_DI + SSD_CONV_C)
    else:
        z_ref, zs_ref, cw_ref, cb_ref, prow_ref, dx_ref, gain_ref, prev_ref, s0_ref, o_ref, s1_ref = refs
        s_scr = prev_scr = None
    mk = _Masks(c, ls)
    mkb = _Masks(min(cpb * c, CUMSUM_ROWS), ls)
    shift_m = mk.shift_matrix(1 if seq_mode else 2)
    e_x = _expander(LANE_SDT, SSD_P, SSD_DI)
    a_row = -jnp.exp(prow_ref[0:1, :])
    b_row = prow_ref[1:2, :]
    cw = cw_ref[...]
    lane_g = lax.broadcasted_iota(jnp.int32, (1, gw), 1) >> 6

    def get_state(b, g):
        if seq_mode:
            return s_scr[g]
        return jnp.concatenate([s0_ref[b, g * hpg + j] for j in range(hpg)], axis=-1)

    def put_state(b, g, val):
        if seq_mode:
            s_scr[g] = val
        else:
            for j in range(hpg):
                s1_ref[b, g * hpg + j] = val[:, SSD_P * j:SSD_P * (j + 1)]

    conv = []
    for ci in range(cpb):
        u = z_ref[ci * c:(ci + 1) * c, SSD_DI:SSD_DI + SSD_CONV_C]
        if seq_mode:
            prev_parts = [prev_scr[...].astype(BF16)]
        else:
            prev_parts = list(_split(prev_ref[...]))
        conv.append(_causal_conv(u, prev_parts, cw, mk, shift_m))
        if seq_mode:
            prev_scr[...] = u.astype(F32)
    xbc = _silu((conv[0] if cpb == 1 else jnp.concatenate(conv, axis=0)) + cb_ref[...])
    sx_all = xbc[:, 0:SSD_DI]
    sb_all = xbc[:, SSD_DI:SSD_DI + SSD_G * SSD_N].astype(BF16)
    sc_all = xbc[:, SSD_DI + SSD_G * SSD_N:SSD_CONV_C].astype(BF16)
    dt = _softplus(zs_ref[...] + b_row)
    lsd_all = dt * a_row
    gc_all = _row_blocks(lambda t: _mm_xr(mkb.tri, t), lsd_all)
    glast = _row_blocks(lambda t: _mm_xr(mkb.lastsel, t), gc_all)
    eg_all = _mm_xl(jnp.exp(gc_all), e_x)
    el_all = _mm_xl(jnp.exp(glast), e_x)
    v_all = sx_all * _mm_xl(dt, e_x)
    vd_all = (v_all * _mm_xl(jnp.exp(glast - gc_all), e_x)).astype(BF16)
    skip_all = sx_all * dx_ref[...]
    gate_all = _silu(z_ref[:, 0:SSD_DI].astype(F32))

    for ci in range(cpb):
        rows = slice(ci * c, (ci + 1) * c)
        gc = gc_all[rows, :]
        lsd_hi, lsd_lo = _split(lsd_all[rows, :])
        gt = _mm_tn(lsd_hi, mk.tri_t) + _mm_tn(lsd_lo, mk.tri_t)
        for g in range(SSD_G):
            gl = slice(gw * g, gw * (g + 1))
            cg = sc_all[rows, SSD_N * g:SSD_N * (g + 1)]
            bg = sb_all[rows, SSD_N * g:SSD_N * (g + 1)]
            ag = _mm_nt(cg, bg)
            vg = v_all[rows, gl]
            o = jnp.zeros((c, gw), F32)
            for j in range(hpg):
                lane = LANE_SDT + g * hpg + j
                dm = _pair_decay(gc[:, lane:lane + 1], gt[lane:lane + 1, :], mk.incl)
                vj = jnp.where(lane_g == j, vg, 0.0).astype(BF16)
                o = o + _mm((ag * dm).astype(BF16), vj)
            o_state = jnp.zeros((c, gw), F32)
            for b in range(mk.nseq):
                s = get_state(b, g)
                o_state = o_state + _mm(mk.rows_of(b, cg), s.astype(BF16))
                dec = el_all[ci * c + b * ls:ci * c + b * ls + 1, gl]
                put_state(b, g, s * dec + _mm_tn(mk.rows_of(b, bg), vd_all[rows, gl]))
            y = (o + o_state * eg_all[rows, gl] + skip_all[rows, gl]) * gate_all[rows, gl]
            o_ref[rows, gl] = _rms(y, gain_ref[:, gl]).astype(o_ref.dtype)
    if seq_mode:
        @pl.when(pl.program_id(1) == pl.num_programs(1) - 1)
        def _():
            for g in range(SSD_G):
                for j in range(hpg):
                    s1_ref[0, g * hpg + j] = s_scr[g][:, SSD_P * j:SSD_P * (j + 1)]


def _full_spec(a, grid_rank):
    nd = a.ndim
    if grid_rank == 1:
        return pl.BlockSpec(a.shape, lambda i: (0,) * nd)
    return pl.BlockSpec(a.shape, lambda i, j: (0,) * nd)


def _mixer_seq(kernel_fn, name, z, extra_rows, params, nbatch, seqlen, state_shape, scratch, tail_cols=0):
    c, cpb = CHUNK, SEQ_BLOCK_CHUNKS
    rblk = c * cpb
    nblk = seqlen // rblk
    rowmap = lambda b, j: (b * nblk + j, 0)
    in_specs = [pl.BlockSpec((rblk, z.shape[1]), rowmap)]
    args = [z]
    for a, kind in extra_rows:
        if kind == "rows":
            in_specs.append(pl.BlockSpec((rblk, a.shape[1]), rowmap))
        else:
            in_specs.append(pl.BlockSpec((rblk, a.shape[1]), lambda b, j: (j, 0)))
        args.append(a)
    for a in params:
        in_specs.append(_full_spec(a, 2))
        args.append(a)
    sblk = (1,) + state_shape
    out_shape = [jax.ShapeDtypeStruct((nbatch * seqlen, BRANCH_W), BF16),
                 jax.ShapeDtypeStruct((nbatch,) + state_shape, F32)]
    out_specs = [pl.BlockSpec((rblk, BRANCH_W), rowmap),
                 pl.BlockSpec(sblk, lambda b, j: (b,) + (0,) * len(state_shape))]
    if tail_cols:
        out_shape.append(jax.ShapeDtypeStruct((nbatch, 8, tail_cols), F32))
        out_specs.append(pl.BlockSpec((1, 8, tail_cols), lambda b, j: (b, 0, 0)))
    return pl.pallas_call(
        functools.partial(kernel_fn, c=c, cpb=cpb, ls=c, seq_mode=True),
        grid=(nbatch, nblk),
        in_specs=in_specs,
        out_specs=out_specs,
        out_shape=out_shape,
        scratch_shapes=scratch,
        compiler_params=pltpu.CompilerParams(dimension_semantics=("parallel", "arbitrary"),
                                             vmem_limit_bytes=VMEM_LIMIT),
        name=name,
    )(*args)


def _mixer_batch(kernel_fn, name, z, row_off, extra_rows, params, per_seq, layer, stacked_prev,
                 nbatch, seqlen, state_shape):
    sb = BATCH_SEQS
    c = sb * seqlen
    nsteps = nbatch // sb
    off = row_off // c
    rowmap = lambda i: (off + i, 0)
    in_specs = [pl.BlockSpec((c, z.shape[1]), rowmap)]
    args = [z]
    for a, kind in extra_rows:
        if kind == "rows":
            in_specs.append(pl.BlockSpec((c, a.shape[1]), rowmap))
        else:
            in_specs.append(pl.BlockSpec((c, a.shape[1]), lambda i: (0, 0)))
        args.append(a)
    for a in params:
        in_specs.append(_full_spec(a, 1))
        args.append(a)
    for a in per_seq:
        if a.ndim == 2:
            in_specs.append(pl.BlockSpec((c, a.shape[1]), lambda i: (i, 0)))
        else:
            in_specs.append(pl.BlockSpec((None, sb) + a.shape[2:], lambda i: (layer, i) + (0,) * (a.ndim - 2)))
        args.append(a)
    zeros = (0,) * len(state_shape)
    if layer:
        in_specs.append(pl.BlockSpec((layer, sb) + state_shape, lambda i: (0, i) + zeros))
        args.append(stacked_prev)

    def body(*refs):
        *ins, o_ref, s1_ref = refs
        if layer:
            prev_ref = ins.pop()
            for l in range(layer):
                s1_ref[l] = prev_ref[l]
        kernel_fn(*ins, o_ref, s1_ref.at[layer], c=c, cpb=1, ls=seqlen, seq_mode=False)

    out_shape = [jax.ShapeDtypeStruct((nbatch * seqlen, BRANCH_W), BF16),
                 jax.ShapeDtypeStruct((layer + 1, nbatch) + state_shape, F32)]
    out_specs = [pl.BlockSpec((c, BRANCH_W), lambda i: (i, 0)),
                 pl.BlockSpec((layer + 1, sb) + state_shape, lambda i: (0, i) + zeros)]
    return pl.pallas_call(
        body,
        grid=(nsteps,),
        in_specs=in_specs,
        out_specs=out_specs,
        out_shape=out_shape,
        compiler_params=pltpu.CompilerParams(dimension_semantics=("parallel",), vmem_limit_bytes=VMEM_LIMIT),
        name=name,
    )(*args)


def _merge_kernel(*refs, nh, npt):
    branch_refs, zg = refs[:2 * N_BRANCH], refs[2 * N_BRANCH]
    h_refs, (wbr, wout, out_ref) = refs[2 * N_BRANCH + 1:2 * N_BRANCH + 1 + nh], refs[2 * N_BRANCH + 1 + nh:]
    acc = jnp.zeros(out_ref.shape, F32)
    for n in range(N_BRANCH):
        gate = _sigmoid(zg[:, n * D_MODEL:(n + 1) * D_MODEL].astype(F32))
        acc = acc + gate * _mm(_rows_read(branch_refs[2 * n:2 * n + 2], npt), wbr[n])
    out_ref[...] = _rows_read(h_refs, npt) + _mm(acc.astype(BF16), wout[...])


def _merge(branches, zg, h_parts, wbr, wout):
    n_p, n_s = branches[0][0].shape[0], branches[0][1].shape[0]
    n = n_p + n_s
    tm = _row_tile(n_p, n_s)
    row = lambda i: (i, 0)
    in_specs, args = [], []
    for pair in branches:
        in_specs += _rows_specs(pair, tm)
        args += list(pair)
    in_specs.append(pl.BlockSpec((tm, N_BRANCH * D_MODEL), row))
    in_specs += _rows_specs(h_parts, tm)
    in_specs += [pl.BlockSpec(wbr.shape, lambda i: (0, 0, 0)), pl.BlockSpec(wout.shape, lambda i: (0, 0))]
    return pl.pallas_call(
        functools.partial(_merge_kernel, nh=len(h_parts), npt=n_p // tm),
        grid=(n // tm,),
        in_specs=in_specs,
        out_specs=pl.BlockSpec((tm, D_MODEL), row),
        out_shape=jax.ShapeDtypeStruct((n, D_MODEL), F32),
        compiler_params=pltpu.CompilerParams(dimension_semantics=("parallel",), vmem_limit_bytes=VMEM_LIMIT),
        name="merge",
    )(*args, zg, *h_parts, wbr, wout)


def _ffn_kernel(h_ref, g_ref, wg_ref, wu_ref, wd_ref, pp_ref, ps_ref, gp_ref, wgate, wproj, out_ref, *, npt):
    h = h_ref[...]
    u = _rms(h, g_ref[...]).astype(BF16)
    a = (_silu(_mm(u, wg_ref[...])) * _mm(u, wu_ref[...])).astype(BF16)
    h = h + _mm(a, wd_ref[...])
    gate = _sigmoid(_mm(_rms(h, gp_ref[...]).astype(BF16), wgate[...]))
    out_ref[...] = h + gate * _mm(_rows_read((pp_ref, ps_ref), npt).astype(BF16), wproj[...])


def _ffn_ple(h, gain, wg, wu, wd, p_parts, layer, gain_ple, wgate, wproj):
    n_p, n_s = p_parts[0].shape[1], p_parts[1].shape[1]
    n = h.shape[0]
    tm = _row_tile(n_p, n_s)
    row = lambda i: (i, 0)
    const = lambda i: (0, 0)
    wspec = lambda w: pl.BlockSpec(w.shape, const, pipeline_mode=pl.Buffered(1))
    return pl.pallas_call(
        functools.partial(_ffn_kernel, npt=n_p // tm),
        grid=(n // tm,),
        in_specs=[pl.BlockSpec((tm, D_MODEL), row), pl.BlockSpec((1, D_MODEL), const),
                  wspec(wg), wspec(wu), wspec(wd)] + _rows_specs(p_parts, tm, layer)
        + [pl.BlockSpec((1, D_MODEL), const), wspec(wgate), wspec(wproj)],
        out_specs=pl.BlockSpec((tm, D_MODEL), row),
        out_shape=jax.ShapeDtypeStruct((n, D_MODEL), F32),
        compiler_params=pltpu.CompilerParams(dimension_semantics=("parallel",), vmem_limit_bytes=VMEM_LIMIT),
        name="dense_ffn_ple",
    )(h, gain, wg, wu, wd, *p_parts, gain_ple, wgate, wproj)


MOE_TOKENS = 768
MOE_CAP = 224


def _moe_kernel(h_ref, g_ref, rt_ref, wg_ref, wu_ref, wd_ref, out_ref,
                u_scr, acc_scr, w_scr, sel_scr, rank_scr, selt_scr, rankt_scr, xc_scr, yc_scr, cnt_scr):
    tt = h_ref.shape[0]
    cap = MOE_CAP
    e = pl.program_id(1)
    f = pl.program_id(2)
    nf = pl.num_programs(2)

    @pl.when(jnp.logical_and(e == 0, f == 0))
    def _():
        u = _rms(h_ref[...], g_ref[...])
        u_scr[...] = u.astype(BF16)
        acc_scr[...] = jnp.zeros_like(acc_scr)
        uh, ul = _split(u)
        rh, rl = _split(rt_ref[...])
        logits = _mm(uh, rh) + (_mm(uh, rl) + _mm(ul, rh))
        lane = lax.broadcasted_iota(jnp.int32, logits.shape, 1).astype(F32)
        neg = -3.0e38
        lg = jnp.where(lane < N_EXPERTS, logits, neg)
        m1 = jnp.max(lg, axis=-1, keepdims=True)
        i1 = jnp.min(jnp.where(lg == m1, lane, float(SMALL_W)), axis=-1, keepdims=True)
        lg2 = jnp.where(lane == i1, neg, lg)
        m2 = jnp.max(lg2, axis=-1, keepdims=True)
        i2 = jnp.min(jnp.where(lg2 == m2, lane, float(SMALL_W)), axis=-1, keepdims=True)
        e2 = jnp.exp(m2 - m1)
        w_scr[...] = jnp.where(lane == i1, 1.0 / (1.0 + e2), 0.0) + jnp.where(lane == i2, e2 / (1.0 + e2), 0.0)
        sel = jnp.where(jnp.logical_or(lane == i1, lane == i2), 1.0, 0.0)
        sel_scr[...] = sel
        selb = sel.astype(BF16)
        r = lax.broadcasted_iota(jnp.int32, (tt, tt), 0)
        q = lax.broadcasted_iota(jnp.int32, (tt, tt), 1)
        before = jnp.where(r < q, 1.0, 0.0).astype(BF16)
        ident = jnp.where(r == q, 1.0, 0.0).astype(BF16)
        rank_scr[...] = _mm_tn(before, selb)
        rankt_scr[...] = _mm_tn(selb, before)
        selt_scr[...] = _mm_tn(selb, ident)
        cnt = jnp.sum(sel, axis=0, keepdims=True)
        for ee in range(N_EXPERTS):
            cnt_scr[ee] = jnp.sum(jnp.where(lane[0:1, :] == float(ee), cnt, 0.0)).astype(jnp.int32)

    nsub = (cnt_scr[e] + (cap - 1)) // cap
    lane_w = lax.broadcasted_iota(jnp.int32, (tt, SMALL_W), 1)

    def col_of(ref):
        return jnp.sum(jnp.where(lane_w == e, ref[...], 0.0), axis=-1, keepdims=True)

    @pl.when(f == 0)
    def _():
        rank_row = rankt_scr[pl.ds(e, 1), :]
        sel_row = selt_scr[pl.ds(e, 1), :]
        jcol = lax.broadcasted_iota(jnp.int32, (cap, 1), 0).astype(F32)

        def gather(s, carry):
            base = (s * cap).astype(F32)
            hit = jnp.logical_and(rank_row == jcol + base, sel_row > 0.5)
            onehot = jnp.where(hit, 1.0, 0.0).astype(BF16)
            off = pl.multiple_of(s * cap, 16)
            xc_scr[pl.ds(off, cap), :] = _mm(onehot, u_scr[...]).astype(BF16)
            return carry

        lax.fori_loop(0, nsub, gather, 0)

    def expert(s, carry):
        off = pl.multiple_of(s * cap, 16)
        x = xc_scr[pl.ds(off, cap), :]
        a = (_silu(_mm(x, wg_ref[0])) * _mm(x, wu_ref[0])).astype(BF16)
        y = _mm(a, wd_ref[0])

        @pl.when(f == 0)
        def _():
            yc_scr[pl.ds(off, cap), :] = y

        @pl.when(f != 0)
        def _():
            yc_scr[pl.ds(off, cap), :] += y

        return carry

    lax.fori_loop(0, nsub, expert, 0)

    @pl.when(f == nf - 1)
    def _():
        rank_col = col_of(rank_scr)
        sel_col = col_of(sel_scr)
        w_col = col_of(w_scr)
        jrow = lax.broadcasted_iota(jnp.int32, (1, cap), 1).astype(F32)

        def scatter(s, carry):
            base = (s * cap).astype(F32)
            hit = jnp.logical_and(rank_col == jrow + base, sel_col > 0.5)
            onehot = jnp.where(hit, 1.0, 0.0).astype(BF16)
            off = pl.multiple_of(s * cap, 16)
            acc_scr[...] += w_col * _mm(onehot, yc_scr[pl.ds(off, cap), :].astype(BF16))
            return carry

        lax.fori_loop(0, nsub, scatter, 0)

    @pl.when(jnp.logical_and(e == pl.num_programs(1) - 1, f == nf - 1))
    def _():
        out_ref[...] = h_ref[...] + acc_scr[...]


def _moe(h, gain, router, wg, wu, wd):
    n = h.shape[0]
    tt = MOE_TOKENS if n % MOE_TOKENS == 0 else _row_tile(n)
    ne = wg.shape[0]
    tf = D_FF // FF_BLOCKS
    row = lambda i, e, f: (i, 0)
    nrows = pl.cdiv(tt, MOE_CAP) * MOE_CAP
    in_specs = [pl.BlockSpec((tt, D_MODEL), row), pl.BlockSpec((1, D_MODEL), lambda i, e, f: (0, 0)),
                pl.BlockSpec(router.shape, lambda i, e, f: (0, 0)),
                pl.BlockSpec((1, D_MODEL, tf), lambda i, e, f: (e, 0, f)),
                pl.BlockSpec((1, D_MODEL, tf), lambda i, e, f: (e, 0, f)),
                pl.BlockSpec((1, tf, D_MODEL), lambda i, e, f: (e, f, 0))]
    scratch = [pltpu.VMEM((tt, D_MODEL), BF16), pltpu.VMEM((tt, D_MODEL), F32),
               pltpu.VMEM((tt, SMALL_W), F32), pltpu.VMEM((tt, SMALL_W), F32), pltpu.VMEM((tt, SMALL_W), F32),
               pltpu.VMEM((SMALL_W, tt), F32), pltpu.VMEM((SMALL_W, tt), F32),
               pltpu.VMEM((nrows, D_MODEL), BF16), pltpu.VMEM((nrows, D_MODEL), F32),
               pltpu.SMEM((N_EXPERTS,), jnp.int32)]
    return pl.pallas_call(
        _moe_kernel,
        grid=(n // tt, ne, FF_BLOCKS),
        in_specs=in_specs,
        out_specs=pl.BlockSpec((tt, D_MODEL), row),
        out_shape=jax.ShapeDtypeStruct((n, D_MODEL), F32),
        scratch_shapes=scratch,
        compiler_params=pltpu.CompilerParams(dimension_semantics=("parallel", "arbitrary", "arbitrary"),
                                             vmem_limit_bytes=VMEM_LIMIT),
        name="moe",
    )(h, gain, router, wg, wu, wd)


def _ple_kernel(*refs, final, npt):
    if final:
        h_ref, pp_ref, ps_ref, g_ref, wgate, wproj, gf_ref, yp_ref, ys_ref = refs
    else:
        h_ref, pp_ref, ps_ref, g_ref, wgate, wproj, out_ref = refs
    h = h_ref[...]
    gate = _sigmoid(_mm(_rms(h, g_ref[...]).astype(BF16), wgate[...]))
    h = h + gate * _mm(_rows_read((pp_ref, ps_ref), npt).astype(BF16), wproj[...])
    if final:
        y = _rms(h, gf_ref[...])

        @pl.when(pl.program_id(0) < npt)
        def _():
            yp_ref[...] = y

        @pl.when(pl.program_id(0) >= npt)
        def _():
            ys_ref[...] = y
    else:
        out_ref[...] = h


def _ple(h, p_parts, layer, gain, wgate, wproj, gain_final=None):
    n_p, n_s = p_parts[0].shape[1], p_parts[1].shape[1]
    n = h.shape[0]
    tm = _row_tile(n_p, n_s)
    final = gain_final is not None
    row = lambda i: (i, 0)
    const = lambda i: (0, 0)
    in_specs = [pl.BlockSpec((tm, D_MODEL), row)] + _rows_specs(p_parts, tm, layer)
    in_specs += [pl.BlockSpec((1, D_MODEL), const), pl.BlockSpec(wgate.shape, const), pl.BlockSpec(wproj.shape, const)]
    args = [h, *p_parts, gain, wgate, wproj]
    if final:
        in_specs.append(pl.BlockSpec((1, D_MODEL), const))
        args.append(gain_final)
        out_shape = [jax.ShapeDtypeStruct((n_p, D_MODEL), F32), jax.ShapeDtypeStruct((n_s, D_MODEL), F32)]
        out_specs = _rows_specs(out_shape, tm)
    else:
        out_shape = jax.ShapeDtypeStruct((n, D_MODEL), F32)
        out_specs = pl.BlockSpec((tm, D_MODEL), row)
    return pl.pallas_call(
        functools.partial(_ple_kernel, final=final, npt=n_p // tm),
        grid=(n // tm,),
        in_specs=in_specs,
        out_specs=out_specs,
        out_shape=out_shape,
        compiler_params=pltpu.CompilerParams(dimension_semantics=("arbitrary",), vmem_limit_bytes=VMEM_LIMIT),
        name="ple",
    )(*args)


def _pad_lanes(x, lane0):
    w = x.shape[-1]
    pad = [(0, 0)] * (x.ndim - 1) + [(lane0, SMALL_W - lane0 - w)]
    return jnp.pad(x, pad)


def _in_proj_weights(w):
    cuts = np.cumsum(SPLITS)[:-1].tolist()
    (gq, gk, gv, glr, gg, rq, rk, rv, rg, dqkv, db, da, dg, sz, sxbc, sdt, mg) = jnp.split(w, cuts, axis=-1)
    cat = lambda *xs: jnp.concatenate(xs, axis=-1).astype(BF16)
    small = jnp.concatenate([glr, db, da, sdt], axis=-1)
    small = jnp.pad(small, ((0, 0), (0, SMALL_W - small.shape[1]))).astype(BF16)
    return (cat(gq, gk, gv, gg), cat(rq, rk, rv, rg), cat(dqkv, dg), cat(sz, sxbc), mg.astype(BF16),
            small)


def _rope_tables(pos):
    half = MIX_DK // 2
    inv = ROPE_BASE ** (-jnp.arange(half, dtype=F32) / half)
    ang = jnp.asarray(pos).astype(F32)[:, None] * inv[None, :]
    cos, sin = jnp.cos(ang), jnp.sin(ang)
    cos_t = jnp.tile(jnp.concatenate([cos, cos], axis=-1), (1, MIX_H))
    sin_t = jnp.tile(jnp.concatenate([-sin, sin], axis=-1), (1, MIX_H))
    return cos_t, sin_t


def _conv_prev(state_conv):
    b, _, cdim = state_conv.shape
    return jnp.pad(state_conv, ((0, 0), (1, 0), (0, 0))).reshape(b * CONV_W, cdim)


def kernel(x_prompt, x_sample, state_gla, state_ret, state_gdn, state_gdn_conv, state_ssd, state_ssd_conv, p_prompt, p_sample, norm_mix, w_in, gla_w_gk, gla_b_gk, gla_norm, ret_norm, gdn_conv_w, gdn_a_log, gdn_dt_bias, gdn_norm, ssd_conv_w, ssd_conv_b, ssd_a_log, ssd_dt_bias, ssd_d, ssd_norm, w_branch, w_out, norm_ffn, ffn_w_gate, ffn_w_up, ffn_w_down, moe_router, moe_w_gate, moe_w_up, moe_w_down, norm_ple, ple_w_gate, ple_w_proj, norm_final):
    bp, lp, _ = x_prompt.shape
    bs, lsmp, _ = x_sample.shape
    depth = w_in.shape[0]
    n_p = bp * lp
    n_s = bs * lsmp
    h_parts = (x_prompt.reshape(n_p, D_MODEL), x_sample.reshape(n_s, D_MODEL))
    p_parts = (p_prompt.reshape(depth, n_p, D_PLE), p_sample.reshape(depth, n_s, D_PLE))
    cos_p, sin_p = _rope_tables(np.arange(lp))
    cos_s, sin_s = _rope_tables(PAST_LEN + (np.arange(BATCH_SEQS * lsmp) % lsmp))
    mix_state = (MIX_H, MIX_DK, MIX_DV)
    stacked_state = (MIX_H * MIX_DK, MIX_DV)
    ssd_state = (SSD_H, SSD_N, SSD_P)

    outs_p = [[] for _ in range(6)]
    outs_s = [[] for _ in range(2)]
    gla_s = ret_s = gdn_s = ssd_s = None
    for i in range(depth):
        wts = _in_proj_weights(w_in[i])
        za, zb, zc, zd, zg, zs = _in_proj(h_parts, norm_mix[i][None, :], wts)

        wgk = jnp.pad(gla_w_gk[i], ((LANE_GLR, SMALL_W - LANE_GLR - GLA_RANK), (0, 0))).astype(BF16)
        bgk = gla_b_gk[i][None, :]
        gla_par = [wgk, bgk, gla_norm[i][None, :]]
        oa_p, gla_p = _mixer_seq(_gla_kernel, "gla_seq", za, [(zs, "rows")], gla_par, bp, lp, mix_state,
                                 [pltpu.VMEM(stacked_state, F32)])
        oa_s, gla_s = _mixer_batch(_gla_kernel, "gla_batch", za, n_p, [(zs, "rows")], gla_par,
                                   [state_gla], i, gla_s, bs, lsmp, mix_state)
        ret_par = [ret_norm[i][None, :]]
        ob_p, ret_p = _mixer_seq(_ret_kernel, "ret_seq", zb, [(cos_p, "pos"), (sin_p, "pos")], ret_par,
                                 bp, lp, mix_state, [pltpu.VMEM(stacked_state, F32)])
        ob_s, ret_s = _mixer_batch(_ret_kernel, "ret_batch", zb, n_p, [(cos_s, "pos"), (sin_s, "pos")],
                                   ret_par, [state_ret], i, ret_s, bs, lsmp, mix_state)
        gdn_prow = jnp.concatenate([_pad_lanes(gdn_a_log[i][None, :], LANE_GDA),
                                    _pad_lanes(gdn_dt_bias[i][None, :], LANE_GDA)], axis=0)
        gdn_par = [gdn_conv_w[i], gdn_prow, gdn_norm[i][None, :]]
        oc_p, gdn_p, gdn_tail = _mixer_seq(
            _gdn_kernel, "gdn_seq", zc, [(zs, "rows")], gdn_par, bp, lp, mix_state,
            [pltpu.VMEM(stacked_state, F32), pltpu.VMEM((CHUNK, GDN_CONV_C), F32)], tail_cols=GDN_CONV_C)
        oc_s, gdn_s = _mixer_batch(_gdn_kernel, "gdn_batch", zc, n_p, [(zs, "rows")], gdn_par,
                                   [_conv_prev(state_gdn_conv[i]), state_gdn], i, gdn_s, bs, lsmp, mix_state)
        ssd_prow = jnp.concatenate([_pad_lanes(ssd_a_log[i][None, :], LANE_SDT),
                                    _pad_lanes(ssd_dt_bias[i][None, :], LANE_SDT)], axis=0)
        ssd_par = [ssd_conv_w[i], ssd_conv_b[i][None, :], ssd_prow,
                   jnp.repeat(ssd_d[i], SSD_P)[None, :], ssd_norm[i][None, :]]
        ssd_scr = [pltpu.VMEM((SSD_G, SSD_N, SSD_DI // SSD_G), F32), pltpu.VMEM((CHUNK, SSD_CONV_C), F32)]
        od_p, ssd_p, ssd_tail = _mixer_seq(_ssd_kernel, "ssd_seq", zd, [(zs, "rows")], ssd_par,
                                           bp, lp, ssd_state, ssd_scr, tail_cols=SSD_CONV_C)
        od_s, ssd_s = _mixer_batch(_ssd_kernel, "ssd_batch", zd, n_p, [(zs, "rows")], ssd_par,
                                   [_conv_prev(state_ssd_conv[i]), state_ssd], i, ssd_s, bs, lsmp, ssd_state)

        gdnc_p = gdn_tail[:, 8 - (CONV_W - 1):]
        ssdc_p = ssd_tail[:, 8 - (CONV_W - 1):]
        zc_s = zc[n_p:].reshape(bs, lsmp, -1)[:, :, :GDN_CONV_C].astype(F32)
        zd_s = zd[n_p:].reshape(bs, lsmp, -1)[:, :, SSD_DI:].astype(F32)
        gdnc_s = jnp.concatenate([state_gdn_conv[i], zc_s], axis=1)[:, lsmp:]
        ssdc_s = jnp.concatenate([state_ssd_conv[i], zd_s], axis=1)[:, lsmp:]
        for lst, val in zip(outs_p, (gla_p, ret_p, gdn_p, gdnc_p, ssd_p, ssdc_p)):
            lst.append(val)
        for lst, val in zip(outs_s, (gdnc_s, ssdc_s)):
            lst.append(val)

        branches = ((oa_p, oa_s), (ob_p, ob_s), (oc_p, oc_s), (od_p, od_s))
        h = _merge(branches, zg, h_parts, w_branch[i].astype(BF16), w_out[i].astype(BF16))

        j = i // 2
        ple_w = (norm_ple[i][None, :], ple_w_gate[i].astype(BF16), ple_w_proj[i].astype(BF16))
        if i % 2 == 0:
            assert i != depth - 1, "a dense-FFN layer is expected to be followed by an expert layer"
            h = _ffn_ple(h, norm_ffn[i][None, :], ffn_w_gate[j].astype(BF16), ffn_w_up[j].astype(BF16),
                         ffn_w_down[j].astype(BF16), p_parts, i, *ple_w)
        else:
            router = jnp.pad(moe_router[j], ((0, 0), (0, SMALL_W - N_EXPERTS)))
            h = _moe(h, norm_ffn[i][None, :], router, moe_w_gate[j].astype(BF16), moe_w_up[j].astype(BF16),
                     moe_w_down[j].astype(BF16))
            h = _ple(h, p_parts, i, *ple_w, gain_final=norm_final[None, :] if i == depth - 1 else None)
        h_parts = (h,)

    y_prompt = h[0].reshape(bp, lp, D_MODEL)
    y_sample = h[1].reshape(bs, lsmp, D_MODEL)
    gdnc_s, ssdc_s = (jnp.stack(l) for l in outs_s)
    return ((y_prompt, y_sample) + tuple(jnp.stack(l) for l in outs_p)
            + (gla_s, ret_s, gdn_s, gdnc_s, ssd_s, ssdc_s))
```
